```python
import jax, jax.numpy as jnp
from jax import lax
import numpy as np

D_MODEL = 1024
BATCH = 4
SEQ = 4096
DEPTH = 2

HEAD_DIM = 64
ROPE_THETA = 10000.0
DIL_CONFIGS = ((128, 1), (512, 4), (2048, 16))
N_DIL_GROUPS = 3
DIL_HEADS = 4
DIL_WIDTH = N_DIL_GROUPS * DIL_HEADS * HEAD_DIM
DIL_OUT = DIL_HEADS * HEAD_DIM
SB_HEADS = 8
SB_WIDTH = SB_HEADS * HEAD_DIM
SB_BLOCK = 128
POOL_WINDOWS = (2, 4, 8, 16)
POOL_GROUP = 128
POOL_WIDTH = len(POOL_WINDOWS) * POOL_GROUP
N_BRANCH = 3
IN_WIDTH = 3 * DIL_WIDTH + 3 * SB_WIDTH + POOL_WIDTH
MOE_GROUPS = 4
EXP_PER_GROUP = 4
N_EXPERTS = MOE_GROUPS * EXP_PER_GROUP
TOP_K_INNER = 2
EXP_HIDDEN = 256
PLE_DIM = 256
EPS = 1e-6

kernel_name = "hybrid_dilated_stickbreak_pool_hmoe"


def rmsnorm(x, g):
    xf = x.astype(jnp.float32)
    y = xf * lax.rsqrt(jnp.mean(xf * xf, axis=-1, keepdims=True) + EPS)
    return (y * g.astype(jnp.float32)).astype(x.dtype)


def rope(t, positions):
    half = t.shape[-1] // 2
    inv = ROPE_THETA ** (-jnp.arange(half, dtype=jnp.float32) / half)
    ang = positions.astype(jnp.float32)[..., None] * inv
    cos = jnp.cos(ang)[:, :, None, :]
    sin = jnp.sin(ang)[:, :, None, :]
    t1 = t[..., :half].astype(jnp.float32)
    t2 = t[..., half:].astype(jnp.float32)
    return jnp.concatenate([t1 * cos - t2 * sin, t2 * cos + t1 * sin], axis=-1).astype(t.dtype)


def dilated_attention(q, k, v, window, dilation):
    b, s, h, c = q.shape
    n = window // dilation
    L = s // dilation
    nb = -(-L // n)
    pad = nb * n - L

    def to_blocks(t):
        t = t.reshape(b, L, dilation, h, c).transpose(0, 2, 1, 3, 4)
        t = jnp.pad(t, ((0, 0), (0, 0), (0, pad), (0, 0), (0, 0)))
        return t.reshape(b, dilation, nb, n, h, c)

    def with_prev(t):
        prev = jnp.pad(t[:, :, :-1], ((0, 0), (0, 0), (1, 0), (0, 0), (0, 0), (0, 0)))
        return jnp.concatenate([prev, t], axis=3)

    qb = to_blocks(q)
    kk = with_prev(to_blocks(k))
    vv = with_prev(to_blocks(v))
    scores = jnp.einsum('brnqhc,brnkhc->brnhqk', qb, kk,
                        preferred_element_type=jnp.float32) * (c ** -0.5)
    qi = jnp.arange(n)[:, None]
    kj = jnp.arange(2 * n)[None, :]
    band = (kj >= qi) & (kj <= qi + n)
    has_prev = (jnp.arange(nb) > 0)[:, None, None]
    mask = band[None] & (has_prev | (kj >= n)[None])
    scores = jnp.where(mask[:, None], scores, -jnp.inf)
    m = jnp.max(scores, axis=-1, keepdims=True)
    e = jnp.exp(scores - m)
    den = jnp.sum(e, axis=-1, keepdims=True)
    lse = (m + jnp.log(den))[..., 0]
    o = jnp.einsum('brnhqk,brnkhc->brnqhc', (e / den).astype(v.dtype), vv)
    o = o.reshape(b, dilation, nb * n, h, c)[:, :, :L].transpose(0, 2, 1, 3, 4).reshape(b, s, h, c)
    lse = lse.transpose(0, 1, 2, 4, 3).reshape(b, dilation, nb * n, h)[:, :, :L]
    lse = lse.transpose(0, 2, 1, 3).reshape(b, s, h)
    return o, lse


def dilated_mixture(q, k, v):
    outs, lses = [], []
    for g, (w, d) in enumerate(DIL_CONFIGS):
        o, l = dilated_attention(q[:, :, g], k[:, :, g], v[:, :, g], w, d)
        outs.append(o.astype(jnp.float32))
        lses.append(l)
    wts = jax.nn.softmax(jnp.stack(lses, axis=0), axis=0)
    o = jnp.sum(wts[..., None] * jnp.stack(outs, axis=0), axis=0)
    return o.astype(q.dtype)


def stick_breaking_attention(q, k, v):
    b, s, h, c = q.shape
    nb = s // SB_BLOCK
    kpos = jnp.arange(s)

    def block(ib):
        start = ib * SB_BLOCK
        qb = lax.dynamic_slice_in_dim(q, start, SB_BLOCK, axis=1)
        z = jnp.einsum('bqhc,bkhc->bhqk', qb, k,
                       preferred_element_type=jnp.float32) * (c ** -0.5)
        qpos = start + jnp.arange(SB_BLOCK)
        mask = kpos[None, :] < qpos[:, None]
        log_om = jnp.where(mask, jax.nn.log_sigmoid(-z), 0.0)
        after = lax.cumsum(log_om, axis=3, reverse=True) - log_om
        a = jnp.where(mask, jnp.exp(jax.nn.log_sigmoid(z) + after), 0.0)
        return jnp.einsum('bhqk,bkhc->bqhc', a.astype(v.dtype), v)

    o = lax.map(block, jnp.arange(nb))
    return o.transpose(1, 0, 2, 3, 4).reshape(b, s, h, c)


def pool_mixer(u, w_pool, pool_scale):
    b, s, _ = u.shape
    ng = len(POOL_WINDOWS)
    uf = u.astype(jnp.float32).reshape(b, s, ng, POOL_GROUP)
    cs = jnp.concatenate([jnp.zeros((b, 1, ng, POOL_GROUP), jnp.float32),
                          jnp.cumsum(uf, axis=1)], axis=1)
    pos = jnp.arange(s)
    outs = []
    for g, w in enumerate(POOL_WINDOWS):
        csg = cs[:, :, g]
        lower = jnp.pad(csg, ((0, 0), (w - 1, 0), (0, 0)))[:, :s]
        upper = csg[:, 1:]
        cnt = jnp.minimum(pos + 1, w).astype(jnp.float32)[None, :, None]
        outs.append((upper - lower) / cnt - uf[:, :, g])
    pooled = jnp.stack(outs, axis=2).astype(u.dtype)
    y = jnp.einsum('bsgc,gcd->bsgd', pooled, w_pool).reshape(b, s, POOL_WIDTH)
    return y * pool_scale


def hierarchical_moe(h, w_rg, b_rg, w_re, b_re, w_g, w_u, w_d):
    b, s, d = h.shape
    t = h.reshape(b * s, d)
    lg = jnp.einsum('td,dg->tg', t, w_rg, preferred_element_type=jnp.float32) + b_rg
    pg = jax.nn.softmax(lg, axis=-1)
    p1, gi = lax.top_k(pg, 1)
    le = jnp.einsum('td,gde->tge', t, w_re, preferred_element_type=jnp.float32) + b_re
    le_sel = jnp.take_along_axis(le, gi[:, :, None], axis=1)[:, 0]
    v2, ei = lax.top_k(le_sel, TOP_K_INNER)
    w2 = jax.nn.softmax(v2, axis=-1) * p1
    eid = gi * EXP_PER_GROUP + ei
    gate = jnp.sum(jax.nn.one_hot(eid, N_EXPERTS, dtype=jnp.float32) * w2[..., None], axis=1)
    hid = jax.nn.silu(jnp.einsum('td,edf->tef', t, w_g)) * jnp.einsum('td,edf->tef', t, w_u)
    y = jnp.einsum('tef,efd->td', hid * gate[:, :, None].astype(hid.dtype), w_d)
    return y.reshape(b, s, d)


def setup_inputs(seed: int = 0) -> dict:
    key = jax.random.key(seed)
    ks = jax.random.split(key, 26)
    f32 = jnp.float32
    nrm = lambda k, shape, scale: jax.random.normal(k, shape, f32) * scale
    L, D = DEPTH, D_MODEL
    return {
        "x": nrm(ks[0], (BATCH, SEQ, D), 1.0),
        "p": nrm(ks[1], (DEPTH, BATCH, SEQ, PLE_DIM), 1.0),
        "positions": jnp.broadcast_to(jnp.arange(SEQ, dtype=jnp.int32)[None, :], (BATCH, SEQ)),
        "norm_mix": 1.0 + nrm(ks[2], (L, D), 0.05),
        "w_in": nrm(ks[3], (L, D, IN_WIDTH), D ** -0.5),
        "w_gate": nrm(ks[4], (L, D, N_BRANCH * D), D ** -0.5),
        "b_gate": nrm(ks[5], (L, N_BRANCH * D), 0.1),
        "w_pool": nrm(ks[6], (L, len(POOL_WINDOWS), POOL_GROUP, POOL_GROUP), POOL_GROUP ** -0.5),
        "pool_scale": 1.0 + nrm(ks[7], (L, POOL_WIDTH), 0.1),
        "w_up_a": nrm(ks[8], (L, DIL_OUT, D), DIL_OUT ** -0.5),
        "w_up_b": nrm(ks[9], (L, SB_WIDTH, D), SB_WIDTH ** -0.5),
        "w_up_c": nrm(ks[10], (L, POOL_WIDTH, D), POOL_WIDTH ** -0.5),
        "w_out": nrm(ks[11], (L, D, D), D ** -0.5),
        "norm_moe": 1.0 + nrm(ks[12], (L, D), 0.05),
        "w_router_grp": nrm(ks[13], (L, D, MOE_GROUPS), D ** -0.5),
        "b_router_grp": nrm(ks[14], (L, MOE_GROUPS), 0.01),
        "w_router_exp": nrm(ks[15], (L, MOE_GROUPS, D, EXP_PER_GROUP), D ** -0.5),
        "b_router_exp": nrm(ks[16], (L, MOE_GROUPS, EXP_PER_GROUP), 0.01),
        "w_exp_gate": nrm(ks[17], (L, N_EXPERTS, D, EXP_HIDDEN), D ** -0.5),
        "w_exp_up": nrm(ks[18], (L, N_EXPERTS, D, EXP_HIDDEN), D ** -0.5),
        "w_exp_down": nrm(ks[19], (L, N_EXPERTS, EXP_HIDDEN, D), EXP_HIDDEN ** -0.5),
        "norm_ple": 1.0 + nrm(ks[20], (L, D), 0.05),
        "w_ple_in": nrm(ks[21], (L, PLE_DIM, D), PLE_DIM ** -0.5),
        "w_ple_gate": nrm(ks[22], (L, D, D), D ** -0.5),
        "norm_final": 1.0 + nrm(ks[23], (D,), 0.05),
    }


def reference(x, p, positions, norm_mix, w_in, w_gate, b_gate, w_pool, pool_scale,
              w_up_a, w_up_b, w_up_c, w_out, norm_moe, w_router_grp, b_router_grp,
              w_router_exp, b_router_exp, w_exp_gate, w_exp_up, w_exp_down,
              norm_ple, w_ple_in, w_ple_gate, norm_final):
    b, s, d = x.shape
    splits = [DIL_WIDTH, 2 * DIL_WIDTH, 3 * DIL_WIDTH, 3 * DIL_WIDTH + SB_WIDTH,
              3 * DIL_WIDTH + 2 * SB_WIDTH, 3 * DIL_WIDTH + 3 * SB_WIDTH]
    nd = N_DIL_GROUPS * DIL_HEADS
    for i in range(DEPTH):
        h = rmsnorm(x, norm_mix[i])
        proj = h @ w_in[i]
        qa, ka, va, qb, kb, vb, uc = jnp.split(proj, splits, axis=-1)
        qa = rope(qa.reshape(b, s, nd, HEAD_DIM), positions).reshape(b, s, N_DIL_GROUPS, DIL_HEADS, HEAD_DIM)
        ka = rope(ka.reshape(b, s, nd, HEAD_DIM), positions).reshape(b, s, N_DIL_GROUPS, DIL_HEADS, HEAD_DIM)
        va = va.reshape(b, s, N_DIL_GROUPS, DIL_HEADS, HEAD_DIM)
        ya = dilated_mixture(qa, ka, va).reshape(b, s, DIL_OUT) @ w_up_a[i]
        yb = stick_breaking_attention(qb.reshape(b, s, SB_HEADS, HEAD_DIM),
                                      kb.reshape(b, s, SB_HEADS, HEAD_DIM),
                                      vb.reshape(b, s, SB_HEADS, HEAD_DIM)).reshape(b, s, SB_WIDTH) @ w_up_b[i]
        yc = pool_mixer(uc, w_pool[i], pool_scale[i]) @ w_up_c[i]
        gates = jax.nn.sigmoid(h @ w_gate[i] + b_gate[i]).reshape(b, s, N_BRANCH, d)
        merged = gates[:, :, 0] * ya + gates[:, :, 1] * yb + gates[:, :, 2] * yc
        x = x + merged @ w_out[i]
        x = x + hierarchical_moe(rmsnorm(x, norm_moe[i]), w_router_grp[i], b_router_grp[i],
                                 w_router_exp[i], b_router_exp[i], w_exp_gate[i],
                                 w_exp_up[i], w_exp_down[i])
        ple_gate = jax.nn.sigmoid(rmsnorm(x, norm_ple[i]) @ w_ple_gate[i])
        x = x + (p[i] @ w_ple_in[i]) * ple_gate
    return rmsnorm(x, norm_final)
```

```python
import functools

import jax
import jax.numpy as jnp
from jax import lax
from jax.experimental import pallas as pl
from jax.experimental.pallas import tpu as pltpu

F32 = jnp.float32
BF16 = jnp.bfloat16

HEAD_DIM = 64
ROPE_THETA = 10000.0
DIL_CONFIGS = ((128, 1), (512, 4), (2048, 16))
DIL_HEADS = 4
DIL_OUT = DIL_HEADS * HEAD_DIM
DIL_WIDTH = len(DIL_CONFIGS) * DIL_OUT
SB_HEADS = 8
SB_WIDTH = SB_HEADS * HEAD_DIM
SB_BLOCK = 128
POOL_WINDOWS = (2, 4, 8, 16)
POOL_GROUP = 128
POOL_WIDTH = len(POOL_WINDOWS) * POOL_GROUP
POOL_HALO = 16
N_BRANCH = 3
MOE_GROUPS = 4
EXP_PER_GROUP = 4
N_EXPERTS = MOE_GROUPS * EXP_PER_GROUP
EPS = 1e-6

LANES = 128
NEG_BIG = -1e30
SB_DEAD_LOG = -105.0
VMEM_LIMIT = 56 * 1024 * 1024


def _cparams(n_axes):
    return pltpu.CompilerParams(
        dimension_semantics=("arbitrary",) * n_axes, vmem_limit_bytes=VMEM_LIMIT)


def _rms(x, g):
    ms = jnp.mean(x * x, axis=-1, keepdims=True)
    return x * lax.rsqrt(ms + EPS) * g


def _dot(a, b):
    return jnp.dot(a, b, preferred_element_type=F32)


def _dot_nt(a, b):
    return lax.dot_general(a, b, (((1,), (1,)), ((), ())), preferred_element_type=F32)


def _const_spec(shape):
    zeros = (0,) * len(shape)
    return pl.BlockSpec(shape, lambda *_: zeros)


def _rope_table_kernel(pos_ref, inv_ref, sgn_ref, cos_ref, sin_ref):
    ang = pos_ref[...].astype(F32) * inv_ref[...]
    cos_ref[...] = jnp.cos(ang)
    sin_ref[...] = jnp.sin(ang) * sgn_ref[...]


def _rope_tables(positions):
    t = positions.size
    half = HEAD_DIM // 2
    inv = ROPE_THETA ** (-jnp.arange(half, dtype=F32) / half)
    inv_row = jnp.tile(inv, LANES // half)[None, :]
    sgn_row = jnp.tile(jnp.concatenate([-jnp.ones(half, F32), jnp.ones(half, F32)]),
                       LANES // HEAD_DIM)[None, :]
    tm = 2048
    return pl.pallas_call(
        _rope_table_kernel,
        grid=(t // tm,),
        in_specs=[pl.BlockSpec((tm, 1), lambda i: (i, 0)),
                  _const_spec((1, LANES)), _const_spec((1, LANES))],
        out_specs=[pl.BlockSpec((tm, LANES), lambda i: (i, 0))] * 2,
        out_shape=[jax.ShapeDtypeStruct((t, LANES), F32)] * 2,
        compiler_params=_cparams(1),
    )(positions.reshape(t, 1), inv_row, sgn_row)


def _inproj_kernel(x_ref, g_ref, w_ref, cos_ref, sin_ref,
                   qa_ref, ka_ref, va_ref, qb_ref, kb_ref, vb_ref, uc_ref, h_scr):
    h_scr[...] = _rms(x_ref[...], g_ref[...]).astype(BF16)
    cos = cos_ref[...]
    sin = sin_ref[...]
    tm = cos.shape[0]
    lane = lax.broadcasted_iota(jnp.int32, (tm, LANES), 1)
    first_half = (lane & (HEAD_DIM // 2)) == 0
    scale = HEAD_DIM ** -0.5

    def rope(t):
        partner = jnp.where(first_half, pltpu.roll(t, LANES - HEAD_DIM // 2, 1),
                            pltpu.roll(t, HEAD_DIM // 2, 1))
        return t * cos + partner * sin

    h = h_scr[...]
    col = 0
    pq = _dot(h, w_ref[:, col:col + DIL_WIDTH])
    for c in range(DIL_WIDTH // LANES):
        sl = slice(c * LANES, (c + 1) * LANES)
        qa_ref[:, sl] = (rope(pq[:, sl]) * scale).astype(BF16)
    col += DIL_WIDTH
    pk = _dot(h, w_ref[:, col:col + DIL_WIDTH])
    for c in range(DIL_WIDTH // LANES):
        sl = slice(c * LANES, (c + 1) * LANES)
        ka_ref[:, sl] = rope(pk[:, sl]).astype(BF16)
    col += DIL_WIDTH
    va_ref[...] = _dot(h, w_ref[:, col:col + DIL_WIDTH]).astype(BF16)
    col += DIL_WIDTH
    qb_ref[...] = (_dot(h, w_ref[:, col:col + SB_WIDTH]) * scale).astype(BF16)
    col += SB_WIDTH
    kb_ref[...] = _dot(h, w_ref[:, col:col + SB_WIDTH]).astype(BF16)
    col += SB_WIDTH
    vb_ref[...] = _dot(h, w_ref[:, col:col + SB_WIDTH]).astype(BF16)
    col += SB_WIDTH
    uc_ref[...] = _dot(h, w_ref[:, col:col + POOL_WIDTH])


def _inproj(x2d, g, w_in, cos_t, sin_t):
    t, d = x2d.shape
    tm = 512
    row = lambda width: pl.BlockSpec((tm, width), lambda i: (i, 0))
    widths = (DIL_WIDTH,) * 3 + (SB_WIDTH,) * 3 + (POOL_WIDTH,)
    dtypes = (BF16,) * 6 + (F32,)
    return pl.pallas_call(
        _inproj_kernel,
        grid=(t // tm,),
        in_specs=[row(d), _const_spec((1, d)), _const_spec(w_in.shape),
                  row(LANES), row(LANES)],
        out_specs=[row(w) for w in widths],
        out_shape=[jax.ShapeDtypeStruct((t, w), dt) for w, dt in zip(widths, dtypes)],
        scratch_shapes=[pltpu.VMEM((tm, d), BF16)],
        compiler_params=_cparams(1),
    )(x2d, g, w_in, cos_t, sin_t)


def _dil_kernel(q_ref, kp_ref, kc_ref, vp_ref, vc_ref, o_ref, lse_ref):
    has_prev = pl.program_id(2) > 0
    n = q_ref.shape[0]
    qi = lax.broadcasted_iota(jnp.int32, (n, n), 0)
    kj = lax.broadcasted_iota(jnp.int32, (n, n), 1)
    mask_prev = jnp.logical_and(kj >= qi, has_prev)
    mask_cur = kj <= qi
    lane = lax.broadcasted_iota(jnp.int32, (n, LANES), 1)
    lo_head = lane < HEAD_DIM
    for pr in range(DIL_OUT // LANES):
        sl = slice(pr * LANES, (pr + 1) * LANES)
        q, kp, kc, vp, vc = (r[:, sl] for r in (q_ref, kp_ref, kc_ref, vp_ref, vc_ref))
        outs, lses = [], []
        for hh in range(2):
            sel = lo_head if hh == 0 else jnp.logical_not(lo_head)
            qm = jnp.where(sel, q, jnp.zeros_like(q))
            sp = jnp.where(mask_prev, _dot_nt(qm, kp), NEG_BIG)
            sc = jnp.where(mask_cur, _dot_nt(qm, kc), NEG_BIG)
            m = jnp.maximum(jnp.max(sp, axis=-1, keepdims=True),
                            jnp.max(sc, axis=-1, keepdims=True))
            ep = jnp.exp(sp - m)
            ec = jnp.exp(sc - m)
            den = jnp.sum(ep, axis=-1, keepdims=True) + jnp.sum(ec, axis=-1, keepdims=True)
            o = (_dot(ep.astype(BF16), vp) + _dot(ec.astype(BF16), vc)) / den
            outs.append(o)
            lses.append(jnp.broadcast_to(m + jnp.log(den), (n, LANES)))
        o_ref[:, sl] = jnp.where(lo_head, outs[0], outs[1])
        lse_ref[:, sl] = jnp.where(lo_head, lses[0], lses[1])


def _dilated_group(qa, ka, va, g, batch, seq):
    window, dil = DIL_CONFIGS[g]
    n = window // dil
    sub_len = seq // dil
    nb = sub_len // n
    n_grp = len(DIL_CONFIGS)
    view = lambda a: a.reshape(batch, sub_len, dil * DIL_WIDTH)
    cur = pl.BlockSpec((None, n, DIL_OUT), lambda b, r, i: (b, i, r * n_grp + g))
    prev = pl.BlockSpec((None, n, DIL_OUT),
                        lambda b, r, i: (b, jnp.maximum(i - 1, 0), r * n_grp + g))
    out = pl.BlockSpec((None, n, DIL_OUT), lambda b, r, i: (b, i, r))
    o, lse = pl.pallas_call(
        _dil_kernel,
        grid=(batch, dil, nb),
        in_specs=[cur, prev, cur, prev, cur],
        out_specs=[out, out],
        out_shape=[jax.ShapeDtypeStruct((batch, sub_len, dil * DIL_OUT), F32)] * 2,
        compiler_params=_cparams(3),
    )(view(qa), view(ka), view(ka), view(va), view(va))
    return o.reshape(batch * seq, DIL_OUT), lse.reshape(batch * seq, DIL_OUT)


def _sb_kernel(q_ref, k_ref, v_ref, o_ref):
    blk = SB_BLOCK
    nq = q_ref.shape[0] // blk
    row = lax.broadcasted_iota(jnp.int32, (blk, blk), 0)
    col = lax.broadcasted_iota(jnp.int32, (blk, blk), 1)
    tri = col < row
    lo_head = col < HEAD_DIM
    r2 = lax.broadcasted_iota(jnp.int32, (blk, 2 * blk), 0)
    c2 = lax.broadcasted_iota(jnp.int32, (blk, 2 * blk), 1)
    cum_mat = jnp.where(jnp.logical_or(c2 >= blk, r2 > c2), 1.0, 0.0).astype(BF16)

    def block(qm, kb, r_acc, acc, diag):
        off = pl.multiple_of(kb * blk, blk)
        k = k_ref[pl.ds(off, blk), :]
        v = v_ref[pl.ds(off, blk), :]
        z = _dot_nt(qm, k)
        sp = jnp.maximum(z, 0.0) + jnp.log1p(jnp.exp(-jnp.abs(z)))
        lom = -sp
        if diag:
            lom = jnp.where(tri, lom, 0.0)
        hi = lom.astype(BF16)
        lo = (lom - hi.astype(F32)).astype(BF16)
        cs = _dot(hi, cum_mat) + _dot(lo, cum_mat)
        a = jnp.exp(z - sp + cs[:, :blk] + r_acc)
        if diag:
            a = jnp.where(tri, a, 0.0)
        return r_acc + cs[:, blk:], acc + _dot(a.astype(BF16), v)

    def q_block(qi, carry):
        qoff = pl.multiple_of(qi * blk, blk)
        q = q_ref[pl.ds(qoff, blk), :]
        res = []
        for hh in range(2):
            sel = lo_head if hh == 0 else jnp.logical_not(lo_head)
            qm = jnp.where(sel, q, jnp.zeros_like(q))
            zero = jnp.zeros((blk, blk), F32)
            r_acc, acc = block(qm, qi, zero, zero, True)

            def cond(c):
                return jnp.logical_and(c[0] >= 0, c[3] > SB_DEAD_LOG)

            def body(c):
                r_new, acc_new = block(qm, c[0], c[1], c[2], False)
                return c[0] - 1, r_new, acc_new, jnp.max(r_new)

            _, _, acc, _ = lax.while_loop(cond, body, (qi - 1, r_acc, acc, jnp.max(r_acc)))
            res.append(acc)
        o_ref[pl.ds(qoff, blk), :] = jnp.where(lo_head, res[0], res[1]).astype(BF16)
        return carry

    lax.fori_loop(0, nq, q_block, 0)


def _stick_breaking(qb, kb, vb, batch, seq):
    view = lambda a: a.reshape(batch, seq, SB_WIDTH)
    spec = pl.BlockSpec((None, seq, LANES), lambda b, hp: (b, 0, hp))
    o = pl.pallas_call(
        _sb_kernel,
        grid=(batch, SB_WIDTH // LANES),
        in_specs=[spec, spec, spec],
        out_specs=spec,
        out_shape=jax.ShapeDtypeStruct((batch, seq, SB_WIDTH), BF16),
        compiler_params=_cparams(2),
    )(view(qb), view(kb), view(vb))
    return o.reshape(batch * seq, SB_WIDTH)


def _merge_kernel(seq, x_ref, o1_ref, o2_ref, o3_ref, l1_ref, l2_ref, l3_ref, sb_ref,
                  uc_ref, uh_ref, g_ref, wg_ref, bg_ref, wp_ref, ps_ref,
                  wa_ref, wb_ref, wc_ref, wo_ref, out_ref,
                  h_scr, ub_scr, yc_scr, m_scr):
    i = pl.program_id(0)
    tm, d = x_ref.shape
    x = x_ref[...]
    h_scr[...] = _rms(x, g_ref[...]).astype(BF16)

    l1, l2, l3 = l1_ref[...], l2_ref[...], l3_ref[...]
    m = jnp.maximum(jnp.maximum(l1, l2), l3)
    e1, e2, e3 = jnp.exp(l1 - m), jnp.exp(l2 - m), jnp.exp(l3 - m)
    dil = ((e1 * o1_ref[...] + e2 * o2_ref[...] + e3 * o3_ref[...]) / (e1 + e2 + e3)).astype(BF16)

    row0 = (i * tm) % seq
    halo = jnp.where(row0 == 0, 0.0, uh_ref[...])
    ub_scr[0:POOL_HALO, :] = halo
    ub_scr[POOL_HALO:POOL_HALO + tm, :] = uc_ref[...]
    t_in_seq = row0 + lax.broadcasted_iota(jnp.int32, (tm, 1), 0)
    for g, w in enumerate(POOL_WINDOWS):
        sl = slice(g * POOL_GROUP, (g + 1) * POOL_GROUP)
        u = ub_scr[POOL_HALO:POOL_HALO + tm, sl]
        acc = u
        for j in range(1, w):
            acc = acc + ub_scr[POOL_HALO - j:POOL_HALO - j + tm, sl]
        cnt = jnp.minimum(t_in_seq + 1, w).astype(F32)
        pooled = acc / cnt - u
        y = _dot(pooled.astype(BF16), wp_ref[g]) * ps_ref[:, sl]
        yc_scr[:, sl] = y.astype(BF16)

    h = h_scr[...]
    sb = sb_ref[...]
    yc_in = yc_scr[...]
    cw = 256
    for c in range(d // cw):
        sl = slice(c * cw, (c + 1) * cw)
        gates = []
        for br in range(N_BRANCH):
            gsl = slice(br * d + c * cw, br * d + (c + 1) * cw)
            gates.append(jax.nn.sigmoid(_dot(h, wg_ref[:, gsl]) + bg_ref[:, gsl]))
        merged = (gates[0] * _dot(dil, wa_ref[:, sl]) + gates[1] * _dot(sb, wb_ref[:, sl])
                  + gates[2] * _dot(yc_in, wc_ref[:, sl]))
        m_scr[:, sl] = merged.astype(BF16)
    out_ref[...] = x + _dot(m_scr[...], wo_ref[...])


def _merge(x2d, dil_outs, sb_out, uc, seq, g, w_gate, b_gate, w_pool, pool_scale,
           w_up_a, w_up_b, w_up_c, w_out):
    t, d = x2d.shape
    tm = 256
    row = lambda width: pl.BlockSpec((tm, width), lambda i: (i, 0))
    halo = pl.BlockSpec((POOL_HALO, POOL_WIDTH),
                        lambda i: (jnp.maximum(i * (tm // POOL_HALO) - 1, 0), 0))
    (o1, l1), (o2, l2), (o3, l3) = dil_outs
    weights = (g, w_gate, b_gate, w_pool, pool_scale, w_up_a, w_up_b, w_up_c, w_out)
    return pl.pallas_call(
        functools.partial(_merge_kernel, seq),
        grid=(t // tm,),
        in_specs=[row(d)] + [row(DIL_OUT)] * 6 + [row(SB_WIDTH), row(POOL_WIDTH), halo]
                 + [_const_spec(w.shape) for w in weights],
        out_specs=row(d),
        out_shape=jax.ShapeDtypeStruct((t, d), F32),
        scratch_shapes=[pltpu.VMEM((tm, d), BF16),
                        pltpu.VMEM((POOL_HALO + tm, POOL_WIDTH), F32),
                        pltpu.VMEM((tm, POOL_WIDTH), BF16),
                        pltpu.VMEM((tm, d), BF16)],
        compiler_params=_cparams(1),
    )(x2d, o1, o2, o3, l1, l2, l3, sb_out, uc, uc, *weights)


ROUTER_EXP_LANE0 = MOE_GROUPS


def _moe_kernel(x_ref, g_ref, wr_ref, br_ref, wg_ref, wu_ref, wd_ref, out_ref,
                xn_scr, gate_scr, acc_scr):
    e = pl.program_id(1)
    tm = x_ref.shape[0]
    lane = lax.broadcasted_iota(jnp.int32, (tm, LANES), 1)

    @pl.when(e == 0)
    def _route():
        xn = _rms(x_ref[...], g_ref[...])
        a_hi = xn.astype(BF16)
        xn_scr[...] = a_hi
        a_lo = (xn - a_hi.astype(F32)).astype(BF16)
        w = wr_ref[...]
        w_hi = w.astype(BF16)
        w_lo = (w - w_hi.astype(F32)).astype(BF16)
        logits = _dot(a_hi, w_hi) + _dot(a_hi, w_lo) + _dot(a_lo, w_hi) + br_ref[...]
        lane_f = lane.astype(F32)
        is_grp = lane < MOE_GROUPS
        lg = jnp.where(is_grp, logits, NEG_BIG)
        mg = jnp.max(lg, axis=-1, keepdims=True)
        p1 = 1.0 / jnp.sum(jnp.exp(lg - mg), axis=-1, keepdims=True)
        gi = jnp.min(jnp.where(lg == mg, lane_f, float(LANES)), axis=-1, keepdims=True)
        lane_grp = ((lane - ROUTER_EXP_LANE0) >> 2).astype(F32)
        in_grp = jnp.logical_and(
            jnp.logical_and(lane >= ROUTER_EXP_LANE0, lane < ROUTER_EXP_LANE0 + N_EXPERTS),
            lane_grp == gi)
        le = jnp.where(in_grp, logits, NEG_BIG)
        v1 = jnp.max(le, axis=-1, keepdims=True)
        i1 = jnp.min(jnp.where(le == v1, lane_f, float(LANES)), axis=-1, keepdims=True)
        le2 = jnp.where(lane_f == i1, NEG_BIG, le)
        v2 = jnp.max(le2, axis=-1, keepdims=True)
        i2 = jnp.min(jnp.where(le2 == v2, lane_f, float(LANES)), axis=-1, keepdims=True)
        t = jnp.exp(v2 - v1)
        w1 = p1 / (1.0 + t)
        w2 = p1 * t / (1.0 + t)
        gate_scr[...] = jnp.where(lane_f == i1, w1, 0.0) + jnp.where(lane_f == i2, w2, 0.0)
        acc_scr[...] = jnp.zeros_like(acc_scr)

    xn = xn_scr[...]
    gcol = jnp.sum(jnp.where(lane == e + ROUTER_EXP_LANE0, gate_scr[...], 0.0),
                   axis=-1, keepdims=True)
    gg = _dot(xn, wg_ref[...])
    uu = _dot(xn, wu_ref[...])
    hid = (gg * jax.nn.sigmoid(gg)) * uu * gcol
    acc_scr[...] += _dot(hid.astype(BF16), wd_ref[...])

    @pl.when(e == N_EXPERTS - 1)
    def _finish():
        out_ref[...] = x_ref[...] + acc_scr[...]


def _moe(x2d, g, w_router, b_router, w_exp_gate, w_exp_up, w_exp_down):
    t, d = x2d.shape
    f = w_exp_gate.shape[-1]
    tm = 1024
    row = pl.BlockSpec((tm, d), lambda i, e: (i, 0))
    return pl.pallas_call(
        _moe_kernel,
        grid=(t // tm, N_EXPERTS),
        in_specs=[row, _const_spec((1, d)), _const_spec(w_router.shape),
                  _const_spec(b_router.shape),
                  pl.BlockSpec((None, d, f), lambda i, e: (e, 0, 0)),
                  pl.BlockSpec((None, d, f), lambda i, e: (e, 0, 0)),
                  pl.BlockSpec((None, f, d), lambda i, e: (e, 0, 0))],
        out_specs=row,
        out_shape=jax.ShapeDtypeStruct((t, d), F32),
        scratch_shapes=[pltpu.VMEM((tm, d), BF16), pltpu.VMEM((tm, LANES), F32),
                        pltpu.VMEM((tm, d), F32)],
        compiler_params=_cparams(2),
    )(x2d, g, w_router, b_router, w_exp_gate, w_exp_up, w_exp_down)


def _router_weights(w_grp, b_grp, w_exp, b_exp):
    d = w_grp.shape[0]
    w = jnp.concatenate([w_grp, w_exp.transpose(1, 0, 2).reshape(d, N_EXPERTS)], axis=1)
    b = jnp.concatenate([b_grp, b_exp.reshape(N_EXPERTS)])
    pad = LANES - w.shape[1]
    return jnp.pad(w, ((0, 0), (0, pad))), jnp.pad(b, (0, pad))[None, :]


def _ple_kernel(final, x_ref, p_ref, g_ref, wgate_ref, win_ref, gf_ref, out_ref):
    x = x_ref[...]
    gate = jax.nn.sigmoid(_dot(_rms(x, g_ref[...]).astype(BF16), wgate_ref[...]))
    y = x + _dot(p_ref[...].astype(BF16), win_ref[...]) * gate
    if final:
        y = _rms(y, gf_ref[...])
    out_ref[...] = y


def _ple(x2d, p2d, g, w_ple_gate, w_ple_in, g_final, final):
    t, d = x2d.shape
    tm = 512
    row = lambda width: pl.BlockSpec((tm, width), lambda i: (i, 0))
    return pl.pallas_call(
        functools.partial(_ple_kernel, final),
        grid=(t // tm,),
        in_specs=[row(d), row(p2d.shape[1]), _const_spec((1, d)),
                  _const_spec(w_ple_gate.shape), _const_spec(w_ple_in.shape),
                  _const_spec((1, d))],
        out_specs=row(d),
        out_shape=jax.ShapeDtypeStruct((t, d), F32),
        compiler_params=_cparams(1),
    )(x2d, p2d, g, w_ple_gate, w_ple_in, g_final)


def kernel(x, p, positions, norm_mix, w_in, w_gate, b_gate, w_pool, pool_scale, w_up_a, w_up_b, w_up_c, w_out, norm_moe, w_router_grp, b_router_grp, w_router_exp, b_router_exp, w_exp_gate, w_exp_up, w_exp_down, norm_ple, w_ple_in, w_ple_gate, norm_final):
    batch, seq, d = x.shape
    depth = w_in.shape[0]
    t = batch * seq
    x2d = x.reshape(t, d)
    cos_t, sin_t = _rope_tables(positions)
    bf = lambda a: a.astype(BF16)
    for i in range(depth):
        qa, ka, va, qb, kb, vb, uc = _inproj(x2d, norm_mix[i][None, :], bf(w_in[i]), cos_t, sin_t)
        dil_outs = [_dilated_group(qa, ka, va, g, batch, seq) for g in range(len(DIL_CONFIGS))]
        sb_out = _stick_breaking(qb, kb, vb, batch, seq)
        x2d = _merge(x2d, dil_outs, sb_out, uc, seq, norm_mix[i][None, :], bf(w_gate[i]),
                     b_gate[i][None, :], bf(w_pool[i]), pool_scale[i][None, :],
                     bf(w_up_a[i]), bf(w_up_b[i]), bf(w_up_c[i]), bf(w_out[i]))
        w_r, b_r = _router_weights(w_router_grp[i], b_router_grp[i],
                                   w_router_exp[i], b_router_exp[i])
        x2d = _moe(x2d, norm_moe[i][None, :], w_r, b_r, bf(w_exp_gate[i]), bf(w_exp_up[i]),
                   bf(w_exp_down[i]))
        x2d = _ple(x2d, p[i].reshape(t, -1), norm_ple[i][None, :], bf(w_ple_gate[i]),
                   bf(w_ple_in[i]), norm_final[None, :], i == depth - 1)
    return x2d.reshape(batch, seq, d)
```

```python
import functools

import jax
import jax.numpy as jnp
from jax import lax
from jax.experimental import pallas as pl
from jax.experimental.pallas import tpu as pltpu

F32 = jnp.float32
BF16 = jnp.bfloat16

HEAD_DIM = 64
ROPE_THETA = 10000.0
DIL_CONFIGS = ((128, 1), (512, 4), (2048, 16))
DIL_HEADS = 4
DIL_OUT = DIL_HEADS * HEAD_DIM
DIL_WIDTH = len(DIL_CONFIGS) * DIL_OUT
SB_HEADS = 8
SB_WIDTH = SB_HEADS * HEAD_DIM
SB_BLOCK = 128
POOL_WINDOWS = (2, 4, 8, 16)
POOL_GROUP = 128
POOL_WIDTH = len(POOL_WINDOWS) * POOL_GROUP
POOL_HALO = 16
N_BRANCH = 3
MOE_GROUPS = 4
EXP_PER_GROUP = 4
N_EXPERTS = MOE_GROUPS * EXP_PER_GROUP
EPS = 1e-6

LANES = 128
NEG_BIG = -1e30
SB_DEAD_LOG = -105.0
SB_WINDOW = 3
SB_GROUP = 4
SB_SKEW = 8
VMEM_LIMIT = 56 * 1024 * 1024


def _cparams(n_axes):
    return pltpu.CompilerParams(
        dimension_semantics=("arbitrary",) * n_axes, vmem_limit_bytes=VMEM_LIMIT)


def _rms(x, g):
    ms = jnp.mean(x * x, axis=-1, keepdims=True)
    return x * lax.rsqrt(ms + EPS) * g


def _dot(a, b):
    return jnp.dot(a, b, preferred_element_type=F32)


def _dot_nt(a, b):
    return lax.dot_general(a, b, (((1,), (1,)), ((), ())), preferred_element_type=F32)


def _const_spec(shape):
    zeros = (0,) * len(shape)
    return pl.BlockSpec(shape, lambda *_: zeros)


def _rope_table_kernel(pos_ref, inv_ref, sgn_ref, cos_ref, sin_ref):
    ang = pos_ref[...].astype(F32) * inv_ref[...]
    cos_ref[...] = jnp.cos(ang)
    sin_ref[...] = jnp.sin(ang) * sgn_ref[...]


def _rope_tables(positions):
    t = positions.size
    half = HEAD_DIM // 2
    inv = ROPE_THETA ** (-jnp.arange(half, dtype=F32) / half)
    inv_row = jnp.tile(inv, LANES // half)[None, :]
    sgn_row = jnp.tile(jnp.concatenate([-jnp.ones(half, F32), jnp.ones(half, F32)]),
                       LANES // HEAD_DIM)[None, :]
    tm = 2048
    return pl.pallas_call(
        _rope_table_kernel,
        grid=(t // tm,),
        in_specs=[pl.BlockSpec((tm, 1), lambda i: (i, 0)),
                  _const_spec((1, LANES)), _const_spec((1, LANES))],
        out_specs=[pl.BlockSpec((tm, LANES), lambda i: (i, 0))] * 2,
        out_shape=[jax.ShapeDtypeStruct((t, LANES), F32)] * 2,
        compiler_params=_cparams(1),
        name="rope_table",
    )(positions.reshape(t, 1), inv_row, sgn_row)


def _inproj_kernel(x_ref, g_ref, w_ref, cos_ref, sin_ref,
                   qa_ref, ka_ref, va_ref, qb_ref, kb_ref, vb_ref, uc_ref, h_scr):
    h_scr[...] = _rms(x_ref[...], g_ref[...]).astype(BF16)
    cos = cos_ref[...]
    sin = sin_ref[...]
    tm = cos.shape[0]
    lane = lax.broadcasted_iota(jnp.int32, (tm, LANES), 1)
    first_half = (lane & (HEAD_DIM // 2)) == 0
    scale = HEAD_DIM ** -0.5

    def rope(t):
        partner = jnp.where(first_half, pltpu.roll(t, LANES - HEAD_DIM // 2, 1),
                            pltpu.roll(t, HEAD_DIM // 2, 1))
        return t * cos + partner * sin

    h = h_scr[...]
    col = 0
    pq = _dot(h, w_ref[:, col:col + DIL_WIDTH])
    for c in range(DIL_WIDTH // LANES):
        sl = slice(c * LANES, (c + 1) * LANES)
        qa_ref[:, sl] = (rope(pq[:, sl]) * scale).astype(BF16)
    col += DIL_WIDTH
    pk = _dot(h, w_ref[:, col:col + DIL_WIDTH])
    for c in range(DIL_WIDTH // LANES):
        sl = slice(c * LANES, (c + 1) * LANES)
        ka_ref[:, sl] = rope(pk[:, sl]).astype(BF16)
    col += DIL_WIDTH
    va_ref[...] = _dot(h, w_ref[:, col:col + DIL_WIDTH]).astype(BF16)
    col += DIL_WIDTH
    qb_ref[...] = (_dot(h, w_ref[:, col:col + SB_WIDTH]) * scale).astype(BF16)
    col += SB_WIDTH
    kb_ref[...] = _dot(h, w_ref[:, col:col + SB_WIDTH]).astype(BF16)
    col += SB_WIDTH
    vb_ref[...] = _dot(h, w_ref[:, col:col + SB_WIDTH]).astype(BF16)
    col += SB_WIDTH
    uc_ref[...] = _dot(h, w_ref[:, col:col + POOL_WIDTH])


def _inproj(x2d, g, w_in, cos_t, sin_t):
    t, d = x2d.shape
    tm = 512
    row = lambda width: pl.BlockSpec((tm, width), lambda i: (i, 0))
    widths = (DIL_WIDTH,) * 3 + (SB_WIDTH,) * 3 + (POOL_WIDTH,)
    dtypes = (BF16,) * 6 + (F32,)
    return pl.pallas_call(
        _inproj_kernel,
        grid=(t // tm,),
        in_specs=[row(d), _const_spec((1, d)), _const_spec(w_in.shape),
                  row(LANES), row(LANES)],
        out_specs=[row(w) for w in widths],
        out_shape=[jax.ShapeDtypeStruct((t, w), dt) for w, dt in zip(widths, dtypes)],
        scratch_shapes=[pltpu.VMEM((tm, d), BF16)],
        compiler_params=_cparams(1),
        name="in_proj",
    )(x2d, g, w_in, cos_t, sin_t)


def _dil_kernel(q_ref, kp_ref, kc_ref, vp_ref, vc_ref, o_ref, lse_ref):
    has_prev = pl.program_id(2) > 0
    n = q_ref.shape[0]
    qi = lax.broadcasted_iota(jnp.int32, (n, n), 0)
    kj = lax.broadcasted_iota(jnp.int32, (n, n), 1)
    mask_prev = jnp.logical_and(kj >= qi, has_prev)
    mask_cur = kj <= qi
    lane = lax.broadcasted_iota(jnp.int32, (n, LANES), 1)
    lo_head = lane < HEAD_DIM
    for pr in range(DIL_OUT // LANES):
        sl = slice(pr * LANES, (pr + 1) * LANES)
        q, kp, kc, vp, vc = (r[:, sl] for r in (q_ref, kp_ref, kc_ref, vp_ref, vc_ref))
        outs, lses = [], []
        for hh in range(2):
            sel = lo_head if hh == 0 else jnp.logical_not(lo_head)
            qm = jnp.where(sel, q, jnp.zeros_like(q))
            sp = jnp.where(mask_prev, _dot_nt(qm, kp), NEG_BIG)
            sc = jnp.where(mask_cur, _dot_nt(qm, kc), NEG_BIG)
            m = jnp.maximum(jnp.max(sp, axis=-1, keepdims=True),
                            jnp.max(sc, axis=-1, keepdims=True))
            ep = jnp.exp(sp - m)
            ec = jnp.exp(sc - m)
            den = jnp.sum(ep, axis=-1, keepdims=True) + jnp.sum(ec, axis=-1, keepdims=True)
            o = (_dot(ep.astype(BF16), vp) + _dot(ec.astype(BF16), vc)) / den
            outs.append(o)
            lses.append(jnp.broadcast_to(m + jnp.log(den), (n, LANES)))
        o_ref[:, sl] = jnp.where(lo_head, outs[0], outs[1])
        lse_ref[:, sl] = jnp.where(lo_head, lses[0], lses[1])


def _dilated_group(qa, ka, va, g, batch, seq):
    window, dil = DIL_CONFIGS[g]
    n = window // dil
    sub_len = seq // dil
    nb = sub_len // n
    n_grp = len(DIL_CONFIGS)
    view = lambda a: a.reshape(batch, sub_len, dil * DIL_WIDTH)
    cur = pl.BlockSpec((None, n, DIL_OUT), lambda b, r, i: (b, i, r * n_grp + g))
    prev = pl.BlockSpec((None, n, DIL_OUT),
                        lambda b, r, i: (b, jnp.maximum(i - 1, 0), r * n_grp + g))
    out = pl.BlockSpec((None, n, DIL_OUT), lambda b, r, i: (b, i, r))
    o, lse = pl.pallas_call(
        _dil_kernel,
        grid=(batch, dil, nb),
        in_specs=[cur, prev, cur, prev, cur],
        out_specs=[out, out],
        out_shape=[jax.ShapeDtypeStruct((batch, sub_len, dil * DIL_OUT), F32)] * 2,
        compiler_params=_cparams(3),
        name=f"dilated_d{dil}",
    )(view(qa), view(ka), view(ka), view(va), view(va))
    return o.reshape(batch * seq, DIL_OUT), lse.reshape(batch * seq, DIL_OUT)


def _sb_kernel(q_ref, k_ref, v_ref, o_ref, acc_scr, r_scr):
    blk = SB_BLOCK
    nq = q_ref.shape[0] // blk
    row = lax.broadcasted_iota(jnp.int32, (blk, blk), 0)
    col = lax.broadcasted_iota(jnp.int32, (blk, blk), 1)
    tri = col < row
    lo_head = col < HEAD_DIM
    cum_mat = jnp.where(row > col, 1.0, 0.0).astype(BF16)

    def head_q(q, hh):
        sel = lo_head if hh == 0 else jnp.logical_not(lo_head)
        return jnp.where(sel, q, jnp.zeros_like(q))

    def tile_scores(z, mask):
        sp = jnp.maximum(z, 0.0) + jnp.log(1.0 + jnp.exp(-jnp.abs(z)))
        lom = -sp
        if mask is not None:
            lom = jnp.where(mask, lom, 0.0)
        hi = lom.astype(BF16)
        lo = (lom - hi.astype(F32)).astype(BF16)
        after = _dot(hi, cum_mat) + _dot(lo, cum_mat)
        return z - sp + after, jnp.sum(lom, axis=-1, keepdims=True)

    def tile_weights(logit, mask, r_acc):
        a = jnp.exp(logit + r_acc)
        if mask is not None:
            a = jnp.where(mask, a, 0.0)
        return a.astype(BF16)

    def q_group(blocks):
        chains = []
        for qi, n_tiles in blocks:
            qoff = qi * blk
            koff = (qi - (n_tiles - 1)) * blk
            if not isinstance(qi, int):
                qoff, koff = pl.multiple_of(qoff, blk), pl.multiple_of(koff, blk)
            q = q_ref[pl.ds(qoff, blk), :]
            kw = k_ref[pl.ds(koff, n_tiles * blk), :]
            vw = v_ref[pl.ds(koff, n_tiles * blk), :]
            for hh in range(2):
                chains.append(dict(z=_dot_nt(head_q(q, hh), kw), vw=vw, n=n_tiles,
                                   r=jnp.zeros((blk, 1), F32), acc=jnp.zeros((blk, blk), F32)))
        max_tiles = max(c["n"] for c in chains)
        tiles = [(c, c["n"] - 1 - j) for j in range(max_tiles) for c in chains if j < c["n"]]
        pending = []
        for step in range(len(tiles) + SB_SKEW):
            if step < len(tiles):
                c, w = tiles[step]
                mask = tri if w == c["n"] - 1 else None
                logit, rowsum = tile_scores(c["z"][:, w * blk:(w + 1) * blk], mask)
                pending.append((c, w, mask, logit, rowsum))
            if step >= SB_SKEW:
                c, w, mask, logit, rowsum = pending[step - SB_SKEW]
                a = tile_weights(logit, mask, c["r"])
                c["acc"] = c["acc"] + _dot(a, c["vw"][w * blk:(w + 1) * blk, :])
                c["r"] = c["r"] + rowsum
        for b, (qi, _) in enumerate(blocks):
            qoff = qi * blk if isinstance(qi, int) else pl.multiple_of(qi * blk, blk)
            c0, c1 = chains[2 * b], chains[2 * b + 1]
            r_scr[0, pl.ds(qoff, blk), :] = jnp.broadcast_to(c0["r"], (blk, blk))
            r_scr[1, pl.ds(qoff, blk), :] = jnp.broadcast_to(c1["r"], (blk, blk))
            acc_pair = jnp.where(lo_head, c0["acc"], c1["acc"])
            acc_scr[pl.ds(qoff, blk), :] = acc_pair
            o_ref[pl.ds(qoff, blk), :] = acc_pair.astype(BF16)

    first_tail = min(SB_WINDOW, nq)
    n_static = first_tail + (nq - first_tail) % SB_GROUP
    q_group([(qi, min(qi + 1, SB_WINDOW)) for qi in range(n_static)])

    def q_loop(it, carry):
        q0 = n_static + it * SB_GROUP
        q_group([(q0 + j, SB_WINDOW) for j in range(SB_GROUP)])
        return carry

    lax.fori_loop(0, (nq - n_static) // SB_GROUP, q_loop, 0)


    @pl.when(jnp.max(r_scr[:, first_tail * blk:, :]) > SB_DEAD_LOG)
    def _tail():
        def q_tail(qi, carry):
            qoff = pl.multiple_of(qi * blk, blk)
            q = q_ref[pl.ds(qoff, blk), :]
            acc0 = acc_scr[pl.ds(qoff, blk), :]
            res = []
            for hh in range(2):
                qm = head_q(q, hh)
                r0 = r_scr[hh, pl.ds(qoff, blk), :]

                def cond(c):
                    return jnp.logical_and(c[0] >= 0, c[3] > SB_DEAD_LOG)

                def body(c, qm=qm):
                    off = pl.multiple_of(c[0] * blk, blk)
                    z = _dot_nt(qm, k_ref[pl.ds(off, blk), :])
                    logit, rowsum = tile_scores(z, None)
                    a = tile_weights(logit, None, c[1])
                    r_new = c[1] + rowsum
                    acc_new = c[2] + _dot(a, v_ref[pl.ds(off, blk), :])
                    return c[0] - 1, r_new, acc_new, jnp.max(r_new)

                c0 = (qi - SB_WINDOW, r0, acc0, jnp.max(r0))
                res.append(lax.while_loop(cond, body, c0)[2])
            o_ref[pl.ds(qoff, blk), :] = jnp.where(lo_head, res[0], res[1]).astype(BF16)
            return carry

        lax.fori_loop(first_tail, nq, q_tail, 0)


def _stick_breaking(qb, kb, vb, batch, seq):
    view = lambda a: a.reshape(batch, seq, SB_WIDTH)
    spec = pl.BlockSpec((None, seq, LANES), lambda b, hp: (b, 0, hp))
    o = pl.pallas_call(
        _sb_kernel,
        grid=(batch, SB_WIDTH // LANES),
        in_specs=[spec, spec, spec],
        out_specs=spec,
        out_shape=jax.ShapeDtypeStruct((batch, seq, SB_WIDTH), BF16),
        scratch_shapes=[pltpu.VMEM((seq, LANES), F32), pltpu.VMEM((2, seq, LANES), F32)],
        compiler_params=_cparams(2),
        name="stick_breaking",
    )(view(qb), view(kb), view(vb))
    return o.reshape(batch * seq, SB_WIDTH)


def _merge_kernel(seq, x_ref, o1_ref, o2_ref, o3_ref, l1_ref, l2_ref, l3_ref, sb_ref,
                  uc_ref, uh_ref, g_ref, wg_ref, bg_ref, wp_ref, ps_ref,
                  wa_ref, wb_ref, wc_ref, wo_ref, out_ref,
                  h_scr, ub_scr, yc_scr, m_scr):
    i = pl.program_id(0)
    tm, d = x_ref.shape
    x = x_ref[...]
    h_scr[...] = _rms(x, g_ref[...]).astype(BF16)

    l1, l2, l3 = l1_ref[...], l2_ref[...], l3_ref[...]
    m = jnp.maximum(jnp.maximum(l1, l2), l3)
    e1, e2, e3 = jnp.exp(l1 - m), jnp.exp(l2 - m), jnp.exp(l3 - m)
    dil = ((e1 * o1_ref[...] + e2 * o2_ref[...] + e3 * o3_ref[...]) / (e1 + e2 + e3)).astype(BF16)

    row0 = (i * tm) % seq
    halo = jnp.where(row0 == 0, 0.0, uh_ref[...])
    ub_scr[0:POOL_HALO, :] = halo
    ub_scr[POOL_HALO:POOL_HALO + tm, :] = uc_ref[...]
    t_in_seq = row0 + lax.broadcasted_iota(jnp.int32, (tm, 1), 0)
    for g, w in enumerate(POOL_WINDOWS):
        sl = slice(g * POOL_GROUP, (g + 1) * POOL_GROUP)
        u = ub_scr[POOL_HALO:POOL_HALO + tm, sl]
        acc = u
        for j in range(1, w):
            acc = acc + ub_scr[POOL_HALO - j:POOL_HALO - j + tm, sl]
        cnt = jnp.minimum(t_in_seq + 1, w).astype(F32)
        pooled = acc / cnt - u
        y = _dot(pooled.astype(BF16), wp_ref[g]) * ps_ref[:, sl]
        yc_scr[:, sl] = y.astype(BF16)

    h = h_scr[...]
    sb = sb_ref[...]
    yc_in = yc_scr[...]
    cw = 256
    for c in range(d // cw):
        sl = slice(c * cw, (c + 1) * cw)
        gates = []
        for br in range(N_BRANCH):
            gsl = slice(br * d + c * cw, br * d + (c + 1) * cw)
            gates.append(jax.nn.sigmoid(_dot(h, wg_ref[:, gsl]) + bg_ref[:, gsl]))
        merged = (gates[0] * _dot(dil, wa_ref[:, sl]) + gates[1] * _dot(sb, wb_ref[:, sl])
                  + gates[2] * _dot(yc_in, wc_ref[:, sl]))
        m_scr[:, sl] = merged.astype(BF16)
    out_ref[...] = x + _dot(m_scr[...], wo_ref[...])


def _merge(x2d, dil_outs, sb_out, uc, seq, g, w_gate, b_gate, w_pool, pool_scale,
           w_up_a, w_up_b, w_up_c, w_out):
    t, d = x2d.shape
    tm = 256
    row = lambda width: pl.BlockSpec((tm, width), lambda i: (i, 0))
    halo = pl.BlockSpec((POOL_HALO, POOL_WIDTH),
                        lambda i: (jnp.maximum(i * (tm // POOL_HALO) - 1, 0), 0))
    (o1, l1), (o2, l2), (o3, l3) = dil_outs
    weights = (g, w_gate, b_gate, w_pool, pool_scale, w_up_a, w_up_b, w_up_c, w_out)
    return pl.pallas_call(
        functools.partial(_merge_kernel, seq),
        grid=(t // tm,),
        in_specs=[row(d)] + [row(DIL_OUT)] * 6 + [row(SB_WIDTH), row(POOL_WIDTH), halo]
                 + [_const_spec(w.shape) for w in weights],
        out_specs=row(d),
        out_shape=jax.ShapeDtypeStruct((t, d), F32),
        scratch_shapes=[pltpu.VMEM((tm, d), BF16),
                        pltpu.VMEM((POOL_HALO + tm, POOL_WIDTH), F32),
                        pltpu.VMEM((tm, POOL_WIDTH), BF16),
                        pltpu.VMEM((tm, d), BF16)],
        compiler_params=_cparams(1),
        name="merge",
    )(x2d, o1, o2, o3, l1, l2, l3, sb_out, uc, uc, *weights)


ROUTER_EXP_LANE0 = MOE_GROUPS


def _moe_kernel(x_ref, g_ref, wr_ref, br_ref, wg_ref, wu_ref, wd_ref, out_ref,
                xn_scr, gate_scr, acc_scr):
    e = pl.program_id(1)
    tm = x_ref.shape[0]
    lane = lax.broadcasted_iota(jnp.int32, (tm, LANES), 1)

    @pl.when(e == 0)
    def _route():
        xn = _rms(x_ref[...], g_ref[...])
        a_hi = xn.astype(BF16)
        xn_scr[...] = a_hi
        a_lo = (xn - a_hi.astype(F32)).astype(BF16)
        w = wr_ref[...]
        w_hi = w.astype(BF16)
        w_lo = (w - w_hi.astype(F32)).astype(BF16)
        logits = _dot(a_hi, w_hi) + _dot(a_hi, w_lo) + _dot(a_lo, w_hi) + br_ref[...]
        lane_f = lane.astype(F32)
        is_grp = lane < MOE_GROUPS
        lg = jnp.where(is_grp, logits, NEG_BIG)
        mg = jnp.max(lg, axis=-1, keepdims=True)
        p1 = 1.0 / jnp.sum(jnp.exp(lg - mg), axis=-1, keepdims=True)
        gi = jnp.min(jnp.where(lg == mg, lane_f, float(LANES)), axis=-1, keepdims=True)
        lane_grp = ((lane - ROUTER_EXP_LANE0) >> 2).astype(F32)
        in_grp = jnp.logical_and(
            jnp.logical_and(lane >= ROUTER_EXP_LANE0, lane < ROUTER_EXP_LANE0 + N_EXPERTS),
            lane_grp == gi)
        le = jnp.where(in_grp, logits, NEG_BIG)
        v1 = jnp.max(le, axis=-1, keepdims=True)
        i1 = jnp.min(jnp.where(le == v1, lane_f, float(LANES)), axis=-1, keepdims=True)
        le2 = jnp.where(lane_f == i1, NEG_BIG, le)
        v2 = jnp.max(le2, axis=-1, keepdims=True)
        i2 = jnp.min(jnp.where(le2 == v2, lane_f, float(LANES)), axis=-1, keepdims=True)
        t = jnp.exp(v2 - v1)
        w1 = p1 / (1.0 + t)
        w2 = p1 * t / (1.0 + t)
        gate_scr[...] = jnp.where(lane_f == i1, w1, 0.0) + jnp.where(lane_f == i2, w2, 0.0)
        acc_scr[...] = jnp.zeros_like(acc_scr)

    xn = xn_scr[...]
    gcol = jnp.sum(jnp.where(lane == e + ROUTER_EXP_LANE0, gate_scr[...], 0.0),
                   axis=-1, keepdims=True)
    gg = _dot(xn, wg_ref[...])
    uu = _dot(xn, wu_ref[...])
    hid = (gg * jax.nn.sigmoid(gg)) * uu * gcol
    acc_scr[...] += _dot(hid.astype(BF16), wd_ref[...])

    @pl.when(e == N_EXPERTS - 1)
    def _finish():
        out_ref[...] = x_ref[...] + acc_scr[...]


def _moe(x2d, g, w_router, b_router, w_exp_gate, w_exp_up, w_exp_down):
    t, d = x2d.shape
    f = w_exp_gate.shape[-1]
    tm = 1024
    row = pl.BlockSpec((tm, d), lambda i, e: (i, 0))
    return pl.pallas_call(
        _moe_kernel,
        grid=(t // tm, N_EXPERTS),
        in_specs=[row, _const_spec((1, d)), _const_spec(w_router.shape),
                  _const_spec(b_router.shape),
                  pl.BlockSpec((None, d, f), lambda i, e: (e, 0, 0)),
                  pl.BlockSpec((None, d, f), lambda i, e: (e, 0, 0)),
                  pl.BlockSpec((None, f, d), lambda i, e: (e, 0, 0))],
        out_specs=row,
        out_shape=jax.ShapeDtypeStruct((t, d), F32),
        scratch_shapes=[pltpu.VMEM((tm, d), BF16), pltpu.VMEM((tm, LANES), F32),
                        pltpu.VMEM((tm, d), F32)],
        compiler_params=_cparams(2),
        name="moe",
    )(x2d, g, w_router, b_router, w_exp_gate, w_exp_up, w_exp_down)


def _router_weights(w_grp, b_grp, w_exp, b_exp):
    d = w_grp.shape[0]
    w = jnp.concatenate([w_grp, w_exp.transpose(1, 0, 2).reshape(d, N_EXPERTS)], axis=1)
    b = jnp.concatenate([b_grp, b_exp.reshape(N_EXPERTS)])
    pad = LANES - w.shape[1]
    return jnp.pad(w, ((0, 0), (0, pad))), jnp.pad(b, (0, pad))[None, :]


def _ple_kernel(final, x_ref, p_ref, g_ref, wgate_ref, win_ref, gf_ref, out_ref):
    x = x_ref[...]
    gate = jax.nn.sigmoid(_dot(_rms(x, g_ref[...]).astype(BF16), wgate_ref[...]))
    y = x + _dot(p_ref[...].astype(BF16), win_ref[...]) * gate
    if final:
        y = _rms(y, gf_ref[...])
    out_ref[...] = y


def _ple(x2d, p2d, g, w_ple_gate, w_ple_in, g_final, final):
    t, d = x2d.shape
    tm = 512
    row = lambda width: pl.BlockSpec((tm, width), lambda i: (i, 0))
    return pl.pallas_call(
        functools.partial(_ple_kernel, final),
        grid=(t // tm,),
        in_specs=[row(d), row(p2d.shape[1]), _const_spec((1, d)),
                  _const_spec(w_ple_gate.shape), _const_spec(w_ple_in.shape),
                  _const_spec((1, d))],
        out_specs=row(d),
        out_shape=jax.ShapeDtypeStruct((t, d), F32),
        compiler_params=_cparams(1),
        name="ple",
    )(x2d, p2d, g, w_ple_gate, w_ple_in, g_final)


def kernel(x, p, positions, norm_mix, w_in, w_gate, b_gate, w_pool, pool_scale, w_up_a, w_up_b, w_up_c, w_out, norm_moe, w_router_grp, b_router_grp, w_router_exp, b_router_exp, w_exp_gate, w_exp_up, w_exp_down, norm_ple, w_ple_in, w_ple_gate, norm_final):
    batch, seq, d = x.shape
    depth = w_in.shape[0]
    t = batch * seq
    x2d = x.reshape(t, d)
    cos_t, sin_t = _rope_tables(positions)
    bf = lambda a: a.astype(BF16)
    for i in range(depth):
        qa, ka, va, qb, kb, vb, uc = _inproj(x2d, norm_mix[i][None, :], bf(w_in[i]), cos_t, sin_t)
        dil_outs = [_dilated_group(qa, ka, va, g, batch, seq) for g in range(len(DIL_CONFIGS))]
        sb_out = _stick_breaking(qb, kb, vb, batch, seq)
        x2d = _merge(x2d, dil_outs, sb_out, uc, seq, norm_mix[i][None, :], bf(w_gate[i]),
                     b_gate[i][None, :], bf(w_pool[i]), pool_scale[i][None, :],
                     bf(w_up_a[i]), bf(w_up_b[i]), bf(w_up_c[i]), bf(w_out[i]))
        w_r, b_r = _router_weights(w_router_grp[i], b_router_grp[i],
                                   w_router_exp[i], b_router_exp[i])
        x2d = _moe(x2d, norm_moe[i][None, :], w_r, b_r, bf(w_exp_gate[i]), bf(w_exp_up[i]),
                   bf(w_exp_down[i]))
        x2d = _ple(x2d, p[i].reshape(t, -1), norm_ple[i][None, :], bf(w_ple_gate[i]),
                   bf(w_ple_in[i]), norm_final[None, :], i == depth - 1)
    return x2d.reshape(batch, seq, d)
```

```python
import functools

import jax
import jax.numpy as jnp
from jax import lax
from jax.experimental import pallas as pl
from jax.experimental.pallas import tpu as pltpu

F32 = jnp.float32
BF16 = jnp.bfloat16

HEAD_DIM = 64
ROPE_THETA = 10000.0
DIL_CONFIGS = ((128, 1), (512, 4), (2048, 16))
DIL_HEADS = 4
DIL_OUT = DIL_HEADS * HEAD_DIM
DIL_WIDTH = len(DIL_CONFIGS) * DIL_OUT
DIL_BLOCK = 128
DIL_CHUNK = 2048
DIL_SKEW = 3
INPROJ_TILE = 512
DIL_PERMUTED = (16,)
SB_HEADS = 8
SB_WIDTH = SB_HEADS * HEAD_DIM
SB_BLOCK = 128
POOL_WINDOWS = (2, 4, 8, 16)
POOL_GROUP = 128
POOL_WIDTH = len(POOL_WINDOWS) * POOL_GROUP
POOL_HALO = 16
N_BRANCH = 3
MOE_GROUPS = 4
EXP_PER_GROUP = 4
N_EXPERTS = MOE_GROUPS * EXP_PER_GROUP
EPS = 1e-6

LANES = 128
NEG_BIG = -1e30
SB_DEAD_LOG = -105.0
SB_WINDOW = 3
SB_GROUP = 4
SB_SKEW = 8
VMEM_LIMIT = 56 * 1024 * 1024


def _cparams(n_axes):
    return pltpu.CompilerParams(
        dimension_semantics=("arbitrary",) * n_axes, vmem_limit_bytes=VMEM_LIMIT)


def _rms(x, g):
    ms = jnp.mean(x * x, axis=-1, keepdims=True)
    return x * lax.rsqrt(ms + EPS) * g


def _dot(a, b):
    return jnp.dot(a, b, preferred_element_type=F32)


def _dot_nt(a, b):
    return lax.dot_general(a, b, (((1,), (1,)), ((), ())), preferred_element_type=F32)


def _const_spec(shape):
    zeros = (0,) * len(shape)
    return pl.BlockSpec(shape, lambda *_: zeros)


def _rope_table_kernel(pos_ref, inv_ref, sgn_ref, cos_ref, sin_ref):
    ang = pos_ref[...].astype(F32) * inv_ref[...]
    cos_ref[...] = jnp.cos(ang)
    sin_ref[...] = jnp.sin(ang) * sgn_ref[...]


def _rope_tables(positions):
    t = positions.size
    half = HEAD_DIM // 2
    inv = ROPE_THETA ** (-jnp.arange(half, dtype=F32) / half)
    inv_row = jnp.tile(inv, LANES // half)[None, :]
    sgn_row = jnp.tile(jnp.concatenate([-jnp.ones(half, F32), jnp.ones(half, F32)]),
                       LANES // HEAD_DIM)[None, :]
    tm = 2048
    return pl.pallas_call(
        _rope_table_kernel,
        grid=(t // tm,),
        in_specs=[pl.BlockSpec((tm, 1), lambda i: (i, 0)),
                  _const_spec((1, LANES)), _const_spec((1, LANES))],
        out_specs=[pl.BlockSpec((tm, LANES), lambda i: (i, 0))] * 2,
        out_shape=[jax.ShapeDtypeStruct((t, LANES), F32)] * 2,
        compiler_params=_cparams(1),
        name="rope_table",
    )(positions.reshape(t, 1), inv_row, sgn_row)


def _inproj_kernel(x_ref, g_ref, w_ref, cos_ref, sin_ref,
                   qa_ref, ka_ref, va_ref, qb_ref, kb_ref, vb_ref, uc_ref, h_scr, perm_scr):
    h_scr[...] = _rms(x_ref[...], g_ref[...]).astype(BF16)
    cos = cos_ref[...]
    sin = sin_ref[...]
    tm = cos.shape[0]
    lane = lax.broadcasted_iota(jnp.int32, (tm, LANES), 1)
    first_half = (lane & (HEAD_DIM // 2)) == 0
    scale = HEAD_DIM ** -0.5

    def rope(t):
        partner = jnp.where(first_half, pltpu.roll(t, LANES - HEAD_DIM // 2, 1),
                            pltpu.roll(t, HEAD_DIM // 2, 1))
        return t * cos + partner * sin

    def store_dilated(ref, which, c, val):
        sl = slice(c * LANES, (c + 1) * LANES)
        dil = DIL_CONFIGS[c // (DIL_OUT // LANES)][1]
        if dil not in DIL_PERMUTED:
            ref[:, sl] = val
            return
        per = tm // dil
        perm_scr[which, c % (DIL_OUT // LANES)] = val
        for r in range(dil):
            ref[r * per:(r + 1) * per, sl] = perm_scr[which, c % (DIL_OUT // LANES),
                                                      pl.ds(r, per, stride=dil), :]

    h = h_scr[...]
    col = 0
    pq = _dot(h, w_ref[:, col:col + DIL_WIDTH])
    for c in range(DIL_WIDTH // LANES):
        store_dilated(qa_ref, 0, c, rope(pq[:, c * LANES:(c + 1) * LANES]) * scale)
    col += DIL_WIDTH
    pk = _dot(h, w_ref[:, col:col + DIL_WIDTH])
    for c in range(DIL_WIDTH // LANES):
        store_dilated(ka_ref, 1, c, rope(pk[:, c * LANES:(c + 1) * LANES]))
    col += DIL_WIDTH
    pv = _dot(h, w_ref[:, col:col + DIL_WIDTH])
    for c in range(DIL_WIDTH // LANES):
        store_dilated(va_ref, 2, c, pv[:, c * LANES:(c + 1) * LANES])
    col += DIL_WIDTH
    qb_ref[...] = (_dot(h, w_ref[:, col:col + SB_WIDTH]) * scale).astype(BF16)
    col += SB_WIDTH
    kb_ref[...] = _dot(h, w_ref[:, col:col + SB_WIDTH]).astype(BF16)
    col += SB_WIDTH
    vb_ref[...] = _dot(h, w_ref[:, col:col + SB_WIDTH]).astype(BF16)
    col += SB_WIDTH
    uc_ref[...] = _dot(h, w_ref[:, col:col + POOL_WIDTH])


def _inproj(x2d, g, w_in, cos_t, sin_t):
    t, d = x2d.shape
    tm = INPROJ_TILE
    row = lambda width: pl.BlockSpec((tm, width), lambda i: (i, 0))
    widths = (DIL_WIDTH,) * 3 + (SB_WIDTH,) * 3 + (POOL_WIDTH,)
    dtypes = (F32,) * 3 + (BF16,) * 3 + (F32,)
    return pl.pallas_call(
        _inproj_kernel,
        grid=(t // tm,),
        in_specs=[row(d), _const_spec((1, d)), _const_spec(w_in.shape),
                  row(LANES), row(LANES)],
        out_specs=[row(w) for w in widths],
        out_shape=[jax.ShapeDtypeStruct((t, w), dt) for w, dt in zip(widths, dtypes)],
        scratch_shapes=[pltpu.VMEM((tm, d), BF16),
                        pltpu.VMEM((3, DIL_OUT // LANES, tm, LANES), F32)],
        compiler_params=_cparams(1),
        name="in_proj",
    )(x2d, g, w_in, cos_t, sin_t)


def _dil_kernel(dil, q_ref, kp_ref, kc_ref, vp_ref, vc_ref, o_ref, lse_ref):
    has_prev = pl.program_id(1) > 0
    n = DIL_BLOCK
    n_blocks = q_ref.shape[0] // (n * dil)
    qi = lax.broadcasted_iota(jnp.int32, (n, n), 0)
    kj = lax.broadcasted_iota(jnp.int32, (n, n), 1)
    mask_prev = kj >= qi
    mask_prev_first = jnp.logical_and(mask_prev, has_prev)
    mask_cur = kj <= qi
    lo_head = kj < HEAD_DIM
    ones = jnp.ones((2 * n, LANES), BF16)

    def rows(ref, r, j):
        if dil in DIL_PERMUTED:
            per = INPROJ_TILE // dil
            pieces = [ref[ti * INPROJ_TILE + r * per:ti * INPROJ_TILE + (r + 1) * per, :]
                      for ti in range(j * (n // per), (j + 1) * (n // per))]
            return jnp.concatenate(pieces, axis=0).astype(BF16)
        start = j * n * dil + r
        return ref[pl.ds(start, n, stride=dil) if dil > 1 else pl.ds(start, n), :].astype(BF16)

    units = {}

    def unit(r, j):
        if (r, j) not in units:
            kc, vc = rows(kc_ref, r, j), rows(vc_ref, r, j)
            if j == 0:
                kp, vp = rows(kp_ref, r, 0), rows(vp_ref, r, 0)
            else:
                kp, vp = units[(r, j - 1)]["kc"], units[(r, j - 1)]["vc"]
            vo = jnp.concatenate([jnp.concatenate([vp, vc], axis=0), ones], axis=1)
            units[(r, j)] = dict(q=rows(q_ref, r, j), kc=kc, vc=vc, kp=kp, vo=vo,
                                 pm=mask_prev_first if j == 0 else mask_prev,
                                 start=j * n * dil + r)
        return units[(r, j)]

    def stage_scores(c):
        u = unit(c["r"], c["j"])
        sel = lo_head if c["hh"] == 0 else jnp.logical_not(lo_head)
        qm = jnp.where(sel, u["q"], jnp.zeros((n, LANES), BF16))
        c["sp"] = _dot_nt(qm, u["kp"])
        c["sc"] = _dot_nt(qm, u["kc"])

    def stage_max(c):
        u = unit(c["r"], c["j"])
        c["sp"] = jnp.where(u["pm"], c["sp"], NEG_BIG)
        c["sc"] = jnp.where(mask_cur, c["sc"], NEG_BIG)
        c["m"] = jnp.max(jnp.maximum(c["sp"], c["sc"]), axis=-1, keepdims=True)

    def stage_pv(c):
        u = unit(c["r"], c["j"])
        ep = jnp.exp(c.pop("sp") - c["m"]).astype(BF16)
        ec = jnp.exp(c.pop("sc") - c["m"]).astype(BF16)
        res = _dot(jnp.concatenate([ep, ec], axis=1), u["vo"])
        c["acc"] = res[:, :LANES]
        c["den"] = res[:, LANES:]

    def stage_out(c, pairs):
        c["o"] = c.pop("acc") / c["den"]
        c["lse"] = c["m"] + jnp.log(c.pop("den"))
        key = (c["r"], c["j"])
        pairs.setdefault(key, []).append(c)
        if len(pairs[key]) == 2:
            c0, c1 = pairs.pop(key)
            u = unit(c["r"], c["j"])
            idx = pl.ds(u["start"], n, stride=dil) if dil > 1 else pl.ds(u["start"], n)
            o_ref[idx, :] = jnp.where(lo_head, c0["o"], c1["o"])
            lse_ref[idx, :] = jnp.where(lo_head, c0["lse"], c1["lse"])

    chains = [dict(r=r, j=j, hh=hh) for r in range(dil) for j in range(n_blocks)
              for hh in range(2)]
    stages = (stage_scores, stage_max, stage_pv)
    pairs = {}
    total = len(chains)
    for step in range(total + len(stages) * DIL_SKEW):
        for s, fn in enumerate(stages):
            i = step - s * DIL_SKEW
            if 0 <= i < total:
                fn(chains[i])
        i = step - len(stages) * DIL_SKEW
        if 0 <= i < total:
            stage_out(chains[i], pairs)


def _dilated_group(qa, ka, va, g, batch, seq):
    window, dil = DIL_CONFIGS[g]
    assert window // dil == DIL_BLOCK and seq % DIL_CHUNK == 0
    nc = seq // DIL_CHUNK
    hist = DIL_BLOCK * dil
    t = batch * seq
    n_pairs = DIL_OUT // LANES
    cur = pl.BlockSpec((DIL_CHUNK, LANES), lambda b, c, pr: (b * nc + c, g * n_pairs + pr))
    prev = pl.BlockSpec(
        (hist, LANES),
        lambda b, c, pr: (jnp.maximum((b * nc + c) * (DIL_CHUNK // hist) - 1, 0),
                          g * n_pairs + pr))
    out = pl.BlockSpec((DIL_CHUNK, LANES), lambda b, c, pr: (b * nc + c, pr))
    return pl.pallas_call(
        functools.partial(_dil_kernel, dil),
        grid=(batch, nc, n_pairs),
        in_specs=[cur, prev, cur, prev, cur],
        out_specs=[out, out],
        out_shape=[jax.ShapeDtypeStruct((t, DIL_OUT), F32)] * 2,
        compiler_params=_cparams(3),
        name=f"dilated_d{dil}",
    )(qa, ka, ka, va, va)


def _sb_kernel(q_ref, k_ref, v_ref, o_ref, acc_scr, r_scr):
    blk = SB_BLOCK
    nq = q_ref.shape[0] // blk
    row = lax.broadcasted_iota(jnp.int32, (blk, blk), 0)
    col = lax.broadcasted_iota(jnp.int32, (blk, blk), 1)
    tri = col < row
    lo_head = col < HEAD_DIM
    cum_mat = jnp.where(row > col, 1.0, 0.0).astype(BF16)

    def head_q(q, hh):
        sel = lo_head if hh == 0 else jnp.logical_not(lo_head)
        return jnp.where(sel, q, jnp.zeros_like(q))

    def tile_scores(z, mask):
        sp = jnp.maximum(z, 0.0) + jnp.log(1.0 + jnp.exp(-jnp.abs(z)))
        lom = -sp
        if mask is not None:
            lom = jnp.where(mask, lom, 0.0)
        hi = lom.astype(BF16)
        lo = (lom - hi.astype(F32)).astype(BF16)
        after = _dot(hi, cum_mat) + _dot(lo, cum_mat)
        return z - sp + after, jnp.sum(lom, axis=-1, keepdims=True)

    def tile_weights(logit, mask, r_acc):
        a = jnp.exp(logit + r_acc)
        if mask is not None:
            a = jnp.where(mask, a, 0.0)
        return a.astype(BF16)

    def q_group(blocks):
        chains = []
        for qi, n_tiles in blocks:
            qoff = qi * blk
            koff = (qi - (n_tiles - 1)) * blk
            if not isinstance(qi, int):
                qoff, koff = pl.multiple_of(qoff, blk), pl.multiple_of(koff, blk)
            q = q_ref[pl.ds(qoff, blk), :]
            kw = k_ref[pl.ds(koff, n_tiles * blk), :]
            vw = v_ref[pl.ds(koff, n_tiles * blk), :]
            for hh in range(2):
                chains.append(dict(z=_dot_nt(head_q(q, hh), kw), vw=vw, n=n_tiles,
                                   r=jnp.zeros((blk, 1), F32), acc=jnp.zeros((blk, blk), F32)))
        max_tiles = max(c["n"] for c in chains)
        tiles = [(c, c["n"] - 1 - j) for j in range(max_tiles) for c in chains if j < c["n"]]
        pending = []
        for step in range(len(tiles) + SB_SKEW):
            if step < len(tiles):
                c, w = tiles[step]
                mask = tri if w == c["n"] - 1 else None
                logit, rowsum = tile_scores(c["z"][:, w * blk:(w + 1) * blk], mask)
                pending.append((c, w, mask, logit, rowsum))
            if step >= SB_SKEW:
                c, w, mask, logit, rowsum = pending[step - SB_SKEW]
                a = tile_weights(logit, mask, c["r"])
                c["acc"] = c["acc"] + _dot(a, c["vw"][w * blk:(w + 1) * blk, :])
                c["r"] = c["r"] + rowsum
        for b, (qi, _) in enumerate(blocks):
            qoff = qi * blk if isinstance(qi, int) else pl.multiple_of(qi * blk, blk)
            c0, c1 = chains[2 * b], chains[2 * b + 1]
            r_scr[0, pl.ds(qoff, blk), :] = jnp.broadcast_to(c0["r"], (blk, blk))
            r_scr[1, pl.ds(qoff, blk), :] = jnp.broadcast_to(c1["r"], (blk, blk))
            acc_pair = jnp.where(lo_head, c0["acc"], c1["acc"])
            acc_scr[pl.ds(qoff, blk), :] = acc_pair
            o_ref[pl.ds(qoff, blk), :] = acc_pair.astype(BF16)

    first_tail = min(SB_WINDOW, nq)
    n_static = first_tail + (nq - first_tail) % SB_GROUP
    q_group([(qi, min(qi + 1, SB_WINDOW)) for qi in range(n_static)])

    def q_loop(it, carry):
        q0 = n_static + it * SB_GROUP
        q_group([(q0 + j, SB_WINDOW) for j in range(SB_GROUP)])
        return carry

    lax.fori_loop(0, (nq - n_static) // SB_GROUP, q_loop, 0)


    @pl.when(jnp.max(r_scr[:, first_tail * blk:, :]) > SB_DEAD_LOG)
    def _tail():
        def q_tail(qi, carry):
            qoff = pl.multiple_of(qi * blk, blk)
            q = q_ref[pl.ds(qoff, blk), :]
            acc0 = acc_scr[pl.ds(qoff, blk), :]
            res = []
            for hh in range(2):
                qm = head_q(q, hh)
                r0 = r_scr[hh, pl.ds(qoff, blk), :]

                def cond(c):
                    return jnp.logical_and(c[0] >= 0, c[3] > SB_DEAD_LOG)

                def body(c, qm=qm):
                    off = pl.multiple_of(c[0] * blk, blk)
                    z = _dot_nt(qm, k_ref[pl.ds(off, blk), :])
                    logit, rowsum = tile_scores(z, None)
                    a = tile_weights(logit, None, c[1])
                    r_new = c[1] + rowsum
                    acc_new = c[2] + _dot(a, v_ref[pl.ds(off, blk), :])
                    return c[0] - 1, r_new, acc_new, jnp.max(r_new)

                c0 = (qi - SB_WINDOW, r0, acc0, jnp.max(r0))
                res.append(lax.while_loop(cond, body, c0)[2])
            o_ref[pl.ds(qoff, blk), :] = jnp.where(lo_head, res[0], res[1]).astype(BF16)
            return carry

        lax.fori_loop(first_tail, nq, q_tail, 0)


def _stick_breaking(qb, kb, vb, batch, seq):
    view = lambda a: a.reshape(batch, seq, SB_WIDTH)
    spec = pl.BlockSpec((None, seq, LANES), lambda b, hp: (b, 0, hp))
    o = pl.pallas_call(
        _sb_kernel,
        grid=(batch, SB_WIDTH // LANES),
        in_specs=[spec, spec, spec],
        out_specs=spec,
        out_shape=jax.ShapeDtypeStruct((batch, seq, SB_WIDTH), BF16),
        scratch_shapes=[pltpu.VMEM((seq, LANES), F32), pltpu.VMEM((2, seq, LANES), F32)],
        compiler_params=_cparams(2),
        name="stick_breaking",
    )(view(qb), view(kb), view(vb))
    return o.reshape(batch * seq, SB_WIDTH)


def _merge_kernel(seq, x_ref, o1_ref, o2_ref, o3_ref, l1_ref, l2_ref, l3_ref, sb_ref,
                  uc_ref, uh_ref, g_ref, wg_ref, bg_ref, wp_ref, ps_ref,
                  wa_ref, wb_ref, wc_ref, wo_ref, out_ref,
                  h_scr, ub_scr, yc_scr, m_scr):
    i = pl.program_id(0)
    tm, d = x_ref.shape
    x = x_ref[...]
    h_scr[...] = _rms(x, g_ref[...]).astype(BF16)

    l1, l2, l3 = l1_ref[...], l2_ref[...], l3_ref[...]
    m = jnp.maximum(jnp.maximum(l1, l2), l3)
    e1, e2, e3 = jnp.exp(l1 - m), jnp.exp(l2 - m), jnp.exp(l3 - m)
    dil = ((e1 * o1_ref[...] + e2 * o2_ref[...] + e3 * o3_ref[...]) / (e1 + e2 + e3)).astype(BF16)

    row0 = (i * tm) % seq
    halo = jnp.where(row0 == 0, 0.0, uh_ref[...])
    ub_scr[0:POOL_HALO, :] = halo
    ub_scr[POOL_HALO:POOL_HALO + tm, :] = uc_ref[...]
    t_in_seq = row0 + lax.broadcasted_iota(jnp.int32, (tm, 1), 0)
    for g, w in enumerate(POOL_WINDOWS):
        sl = slice(g * POOL_GROUP, (g + 1) * POOL_GROUP)
        u = ub_scr[POOL_HALO:POOL_HALO + tm, sl]
        acc = u
        for j in range(1, w):
            acc = acc + ub_scr[POOL_HALO - j:POOL_HALO - j + tm, sl]
        cnt = jnp.minimum(t_in_seq + 1, w).astype(F32)
        pooled = acc / cnt - u
        y = _dot(pooled.astype(BF16), wp_ref[g]) * ps_ref[:, sl]
        yc_scr[:, sl] = y.astype(BF16)

    h = h_scr[...]
    sb = sb_ref[...]
    yc_in = yc_scr[...]
    cw = 256
    for c in range(d // cw):
        sl = slice(c * cw, (c + 1) * cw)
        gates = []
        for br in range(N_BRANCH):
            gsl = slice(br * d + c * cw, br * d + (c + 1) * cw)
            gates.append(jax.nn.sigmoid(_dot(h, wg_ref[:, gsl]) + bg_ref[:, gsl]))
        merged = (gates[0] * _dot(dil, wa_ref[:, sl]) + gates[1] * _dot(sb, wb_ref[:, sl])
                  + gates[2] * _dot(yc_in, wc_ref[:, sl]))
        m_scr[:, sl] = merged.astype(BF16)
    out_ref[...] = x + _dot(m_scr[...], wo_ref[...])


def _merge(x2d, dil_outs, sb_out, uc, seq, g, w_gate, b_gate, w_pool, pool_scale,
           w_up_a, w_up_b, w_up_c, w_out):
    t, d = x2d.shape
    tm = 256
    row = lambda width: pl.BlockSpec((tm, width), lambda i: (i, 0))
    halo = pl.BlockSpec((POOL_HALO, POOL_WIDTH),
                        lambda i: (jnp.maximum(i * (tm // POOL_HALO) - 1, 0), 0))
    (o1, l1), (o2, l2), (o3, l3) = dil_outs
    weights = (g, w_gate, b_gate, w_pool, pool_scale, w_up_a, w_up_b, w_up_c, w_out)
    return pl.pallas_call(
        functools.partial(_merge_kernel, seq),
        grid=(t // tm,),
        in_specs=[row(d)] + [row(DIL_OUT)] * 6 + [row(SB_WIDTH), row(POOL_WIDTH), halo]
                 + [_const_spec(w.shape) for w in weights],
        out_specs=row(d),
        out_shape=jax.ShapeDtypeStruct((t, d), F32),
        scratch_shapes=[pltpu.VMEM((tm, d), BF16),
                        pltpu.VMEM((POOL_HALO + tm, POOL_WIDTH), F32),
                        pltpu.VMEM((tm, POOL_WIDTH), BF16),
                        pltpu.VMEM((tm, d), BF16)],
        compiler_params=_cparams(1),
        name="merge",
    )(x2d, o1, o2, o3, l1, l2, l3, sb_out, uc, uc, *weights)


ROUTER_EXP_LANE0 = MOE_GROUPS


def _moe_kernel(x_ref, g_ref, wr_ref, br_ref, wg_ref, wu_ref, wd_ref, out_ref,
                xn_scr, gate_scr, acc_scr):
    e = pl.program_id(1)
    tm = x_ref.shape[0]
    lane = lax.broadcasted_iota(jnp.int32, (tm, LANES), 1)

    @pl.when(e == 0)
    def _route():
        xn = _rms(x_ref[...], g_ref[...])
        a_hi = xn.astype(BF16)
        xn_scr[...] = a_hi
        a_lo = (xn - a_hi.astype(F32)).astype(BF16)
        w = wr_ref[...]
        w_hi = w.astype(BF16)
        w_lo = (w - w_hi.astype(F32)).astype(BF16)
        logits = _dot(a_hi, w_hi) + _dot(a_hi, w_lo) + _dot(a_lo, w_hi) + br_ref[...]
        lane_f = lane.astype(F32)
        is_grp = lane < MOE_GROUPS
        lg = jnp.where(is_grp, logits, NEG_BIG)
        mg = jnp.max(lg, axis=-1, keepdims=True)
        p1 = 1.0 / jnp.sum(jnp.exp(lg - mg), axis=-1, keepdims=True)
        gi = jnp.min(jnp.where(lg == mg, lane_f, float(LANES)), axis=-1, keepdims=True)
        lane_grp = ((lane - ROUTER_EXP_LANE0) >> 2).astype(F32)
        in_grp = jnp.logical_and(
            jnp.logical_and(lane >= ROUTER_EXP_LANE0, lane < ROUTER_EXP_LANE0 + N_EXPERTS),
            lane_grp == gi)
        le = jnp.where(in_grp, logits, NEG_BIG)
        v1 = jnp.max(le, axis=-1, keepdims=True)
        i1 = jnp.min(jnp.where(le == v1, lane_f, float(LANES)), axis=-1, keepdims=True)
        le2 = jnp.where(lane_f == i1, NEG_BIG, le)
        v2 = jnp.max(le2, axis=-1, keepdims=True)
        i2 = jnp.min(jnp.where(le2 == v2, lane_f, float(LANES)), axis=-1, keepdims=True)
        t = jnp.exp(v2 - v1)
        w1 = p1 / (1.0 + t)
        w2 = p1 * t / (1.0 + t)
        gate_scr[...] = jnp.where(lane_f == i1, w1, 0.0) + jnp.where(lane_f == i2, w2, 0.0)
        acc_scr[...] = jnp.zeros_like(acc_scr)

    xn = xn_scr[...]
    gcol = jnp.sum(jnp.where(lane == e + ROUTER_EXP_LANE0, gate_scr[...], 0.0),
                   axis=-1, keepdims=True)
    gg = _dot(xn, wg_ref[...])
    uu = _dot(xn, wu_ref[...])
    hid = (gg * jax.nn.sigmoid(gg)) * uu * gcol
    acc_scr[...] += _dot(hid.astype(BF16), wd_ref[...])

    @pl.when(e == N_EXPERTS - 1)
    def _finish():
        out_ref[...] = x_ref[...] + acc_scr[...]


def _moe(x2d, g, w_router, b_router, w_exp_gate, w_exp_up, w_exp_down):
    t, d = x2d.shape
    f = w_exp_gate.shape[-1]
    tm = 1024
    row = pl.BlockSpec((tm, d), lambda i, e: (i, 0))
    return pl.pallas_call(
        _moe_kernel,
        grid=(t // tm, N_EXPERTS),
        in_specs=[row, _const_spec((1, d)), _const_spec(w_router.shape),
                  _const_spec(b_router.shape),
                  pl.BlockSpec((None, d, f), lambda i, e: (e, 0, 0)),
                  pl.BlockSpec((None, d, f), lambda i, e: (e, 0, 0)),
                  pl.BlockSpec((None, f, d), lambda i, e: (e, 0, 0))],
        out_specs=row,
        out_shape=jax.ShapeDtypeStruct((t, d), F32),
        scratch_shapes=[pltpu.VMEM((tm, d), BF16), pltpu.VMEM((tm, LANES), F32),
                        pltpu.VMEM((tm, d), F32)],
        compiler_params=_cparams(2),
        name="moe",
    )(x2d, g, w_router, b_router, w_exp_gate, w_exp_up, w_exp_down)


def _router_weights(w_grp, b_grp, w_exp, b_exp):
    d = w_grp.shape[0]
    w = jnp.concatenate([w_grp, w_exp.transpose(1, 0, 2).reshape(d, N_EXPERTS)], axis=1)
    b = jnp.concatenate([b_grp, b_exp.reshape(N_EXPERTS)])
    pad = LANES - w.shape[1]
    return jnp.pad(w, ((0, 0), (0, pad))), jnp.pad(b, (0, pad))[None, :]


def _ple_kernel(final, x_ref, p_ref, g_ref, wgate_ref, win_ref, gf_ref, out_ref):
    x = x_ref[...]
    gate = jax.nn.sigmoid(_dot(_rms(x, g_ref[...]).astype(BF16), wgate_ref[...]))
    y = x + _dot(p_ref[...].astype(BF16), win_ref[...]) * gate
    if final:
        y = _rms(y, gf_ref[...])
    out_ref[...] = y


def _ple(x2d, p2d, g, w_ple_gate, w_ple_in, g_final, final):
    t, d = x2d.shape
    tm = 512
    row = lambda width: pl.BlockSpec((tm, width), lambda i: (i, 0))
    return pl.pallas_call(
        functools.partial(_ple_kernel, final),
        grid=(t // tm,),
        in_specs=[row(d), row(p2d.shape[1]), _const_spec((1, d)),
                  _const_spec(w_ple_gate.shape), _const_spec(w_ple_in.shape),
                  _const_spec((1, d))],
        out_specs=row(d),
        out_shape=jax.ShapeDtypeStruct((t, d), F32),
        compiler_params=_cparams(1),
        name="ple",
    )(x2d, p2d, g, w_ple_gate, w_ple_in, g_final)


def kernel(x, p, positions, norm_mix, w_in, w_gate, b_gate, w_pool, pool_scale, w_up_a, w_up_b, w_up_c, w_out, norm_moe, w_router_grp, b_router_grp, w_router_exp, b_router_exp, w_exp_gate, w_exp_up, w_exp_down, norm_ple, w_ple_in, w_ple_gate, norm_final):
    batch, seq, d = x.shape
    depth = w_in.shape[0]
    t = batch * seq
    x2d = x.reshape(t, d)
    cos_t, sin_t = _rope_tables(positions)
    bf = lambda a: a.astype(BF16)
    for i in range(depth):
        qa, ka, va, qb, kb, vb, uc = _inproj(x2d, norm_mix[i][None, :], bf(w_in[i]), cos_t, sin_t)
        dil_outs = [_dilated_group(qa, ka, va, g, batch, seq) for g in range(len(DIL_CONFIGS))]
        sb_out = _stick_breaking(qb, kb, vb, batch, seq)
        x2d = _merge(x2d, dil_outs, sb_out, uc, seq, norm_mix[i][None, :], bf(w_gate[i]),
                     b_gate[i][None, :], bf(w_pool[i]), pool_scale[i][None, :],
                     bf(w_up_a[i]), bf(w_up_b[i]), bf(w_up_c[i]), bf(w_out[i]))
        w_r, b_r = _router_weights(w_router_grp[i], b_router_grp[i],
                                   w_router_exp[i], b_router_exp[i])
        x2d = _moe(x2d, norm_moe[i][None, :], w_r, b_r, bf(w_exp_gate[i]), bf(w_exp_up[i]),
                   bf(w_exp_down[i]))
        x2d = _ple(x2d, p[i].reshape(t, -1), norm_ple[i][None, :], bf(w_ple_gate[i]),
                   bf(w_ple_in[i]), norm_final[None, :], i == depth - 1)
    return x2d.reshape(batch, seq, d)
```

```python
import functools

import jax
import jax.numpy as jnp
from jax import lax
from jax.experimental import pallas as pl
from jax.experimental.pallas import tpu as pltpu

F32 = jnp.float32
BF16 = jnp.bfloat16

HEAD_DIM = 64
ROPE_THETA = 10000.0
DIL_CONFIGS = ((128, 1), (512, 4), (2048, 16))
DIL_HEADS = 4
DIL_OUT = DIL_HEADS * HEAD_DIM
DIL_WIDTH = len(DIL_CONFIGS) * DIL_OUT
DIL_BLOCK = 128
DIL_CHUNK = 2048
DIL_SKEW = 3
INPROJ_TILE = 512
DIL_PERMUTED = (16,)
SB_HEADS = 8
SB_WIDTH = SB_HEADS * HEAD_DIM
SB_BLOCK = 128
POOL_WINDOWS = (2, 4, 8, 16)
POOL_GROUP = 128
POOL_WIDTH = len(POOL_WINDOWS) * POOL_GROUP
POOL_HALO = 16
N_BRANCH = 3
MOE_GROUPS = 4
EXP_PER_GROUP = 4
N_EXPERTS = MOE_GROUPS * EXP_PER_GROUP
EPS = 1e-6

LANES = 128
NEG_BIG = -1e30
SB_DEAD_LOG = -105.0
SB_WINDOW = 3
SB_GROUP = 4
SB_SKEW = 8
VMEM_LIMIT = 56 * 1024 * 1024


def _cparams(n_axes):
    return pltpu.CompilerParams(
        dimension_semantics=("arbitrary",) * n_axes, vmem_limit_bytes=VMEM_LIMIT)


def _rms(x, g):
    ms = jnp.mean(x * x, axis=-1, keepdims=True)
    return x * lax.rsqrt(ms + EPS) * g


def _dot(a, b):
    return jnp.dot(a, b, preferred_element_type=F32)


def _dot_nt(a, b):
    return lax.dot_general(a, b, (((1,), (1,)), ((), ())), preferred_element_type=F32)


def _const_spec(shape):
    zeros = (0,) * len(shape)
    return pl.BlockSpec(shape, lambda *_: zeros)


def _rope_table_kernel(pos_ref, inv_ref, sgn_ref, cos_ref, sin_ref):
    ang = pos_ref[...].astype(F32) * inv_ref[...]
    cos_ref[...] = jnp.cos(ang)
    sin_ref[...] = jnp.sin(ang) * sgn_ref[...]


def _rope_tables(positions):
    t = positions.size
    half = HEAD_DIM // 2
    inv = ROPE_THETA ** (-jnp.arange(half, dtype=F32) / half)
    inv_row = jnp.tile(inv, LANES // half)[None, :]
    sgn_row = jnp.tile(jnp.concatenate([-jnp.ones(half, F32), jnp.ones(half, F32)]),
                       LANES // HEAD_DIM)[None, :]
    tm = 2048
    return pl.pallas_call(
        _rope_table_kernel,
        grid=(t // tm,),
        in_specs=[pl.BlockSpec((tm, 1), lambda i: (i, 0)),
                  _const_spec((1, LANES)), _const_spec((1, LANES))],
        out_specs=[pl.BlockSpec((tm, LANES), lambda i: (i, 0))] * 2,
        out_shape=[jax.ShapeDtypeStruct((t, LANES), F32)] * 2,
        compiler_params=_cparams(1),
        name="rope_table",
    )(positions.reshape(t, 1), inv_row, sgn_row)


def _inproj_kernel(x_ref, g_ref, w_ref, cos_ref, sin_ref,
                   qa_ref, ka_ref, va_ref, qb_ref, kb_ref, vb_ref, uc_ref, h_scr, perm_scr):
    h_scr[...] = _rms(x_ref[...], g_ref[...]).astype(BF16)
    cos = cos_ref[...]
    sin = sin_ref[...]
    tm = cos.shape[0]
    lane = lax.broadcasted_iota(jnp.int32, (tm, LANES), 1)
    first_half = (lane & (HEAD_DIM // 2)) == 0
    scale = HEAD_DIM ** -0.5

    def rope(t):
        partner = jnp.where(first_half, pltpu.roll(t, LANES - HEAD_DIM // 2, 1),
                            pltpu.roll(t, HEAD_DIM // 2, 1))
        return t * cos + partner * sin

    def store_dilated(ref, which, c, val):
        sl = slice(c * LANES, (c + 1) * LANES)
        dil = DIL_CONFIGS[c // (DIL_OUT // LANES)][1]
        if dil not in DIL_PERMUTED:
            ref[:, sl] = val
            return
        per = tm // dil
        perm_scr[which, c % (DIL_OUT // LANES)] = val
        for r in range(dil):
            ref[r * per:(r + 1) * per, sl] = perm_scr[which, c % (DIL_OUT // LANES),
                                                      pl.ds(r, per, stride=dil), :]

    h = h_scr[...]
    col = 0
    pq = _dot(h, w_ref[:, col:col + DIL_WIDTH])
    for c in range(DIL_WIDTH // LANES):
        store_dilated(qa_ref, 0, c, rope(pq[:, c * LANES:(c + 1) * LANES]) * scale)
    col += DIL_WIDTH
    pk = _dot(h, w_ref[:, col:col + DIL_WIDTH])
    for c in range(DIL_WIDTH // LANES):
        store_dilated(ka_ref, 1, c, rope(pk[:, c * LANES:(c + 1) * LANES]))
    col += DIL_WIDTH
    pv = _dot(h, w_ref[:, col:col + DIL_WIDTH])
    for c in range(DIL_WIDTH // LANES):
        store_dilated(va_ref, 2, c, pv[:, c * LANES:(c + 1) * LANES])
    col += DIL_WIDTH
    qb_ref[...] = (_dot(h, w_ref[:, col:col + SB_WIDTH]) * scale).astype(BF16)
    col += SB_WIDTH
    kb_ref[...] = _dot(h, w_ref[:, col:col + SB_WIDTH]).astype(BF16)
    col += SB_WIDTH
    vb_ref[...] = _dot(h, w_ref[:, col:col + SB_WIDTH]).astype(BF16)
    col += SB_WIDTH
    uc_ref[...] = _dot(h, w_ref[:, col:col + POOL_WIDTH])


def _inproj(x2d, g, w_in, cos_t, sin_t):
    t, d = x2d.shape
    tm = INPROJ_TILE
    row = lambda width: pl.BlockSpec((tm, width), lambda i: (i, 0))
    widths = (DIL_WIDTH,) * 3 + (SB_WIDTH,) * 3 + (POOL_WIDTH,)
    dtypes = (F32,) * 3 + (BF16,) * 3 + (F32,)
    return pl.pallas_call(
        _inproj_kernel,
        grid=(t // tm,),
        in_specs=[row(d), _const_spec((1, d)), _const_spec(w_in.shape),
                  row(LANES), row(LANES)],
        out_specs=[row(w) for w in widths],
        out_shape=[jax.ShapeDtypeStruct((t, w), dt) for w, dt in zip(widths, dtypes)],
        scratch_shapes=[pltpu.VMEM((tm, d), BF16),
                        pltpu.VMEM((3, DIL_OUT // LANES, tm, LANES), F32)],
        compiler_params=_cparams(1),
        name="in_proj",
    )(x2d, g, w_in, cos_t, sin_t)


def _dil_kernel(dil, q_ref, kp_ref, kc_ref, vp_ref, vc_ref, o_ref, lse_ref):
    has_prev = pl.program_id(1) > 0
    n = DIL_BLOCK
    n_blocks = q_ref.shape[0] // (n * dil)
    qi = lax.broadcasted_iota(jnp.int32, (n, n), 0)
    kj = lax.broadcasted_iota(jnp.int32, (n, n), 1)
    mask_prev = kj >= qi
    mask_prev_first = jnp.logical_and(mask_prev, has_prev)
    mask_cur = kj <= qi
    lo_head = kj < HEAD_DIM
    ones = jnp.ones((2 * n, LANES), BF16)

    def rows(ref, r, j):
        if dil in DIL_PERMUTED:
            per = INPROJ_TILE // dil
            pieces = [ref[ti * INPROJ_TILE + r * per:ti * INPROJ_TILE + (r + 1) * per, :]
                      for ti in range(j * (n // per), (j + 1) * (n // per))]
            return jnp.concatenate(pieces, axis=0).astype(BF16)
        start = j * n * dil + r
        return ref[pl.ds(start, n, stride=dil) if dil > 1 else pl.ds(start, n), :].astype(BF16)

    units = {}

    def unit(r, j):
        if (r, j) not in units:
            kc, vc = rows(kc_ref, r, j), rows(vc_ref, r, j)
            if j == 0:
                kp, vp = rows(kp_ref, r, 0), rows(vp_ref, r, 0)
            else:
                kp, vp = units[(r, j - 1)]["kc"], units[(r, j - 1)]["vc"]
            vo = jnp.concatenate([jnp.concatenate([vp, vc], axis=0), ones], axis=1)
            units[(r, j)] = dict(q=rows(q_ref, r, j), kc=kc, vc=vc, kp=kp, vo=vo,
                                 pm=mask_prev_first if j == 0 else mask_prev,
                                 start=j * n * dil + r)
        return units[(r, j)]

    def stage_scores(c):
        u = unit(c["r"], c["j"])
        sel = lo_head if c["hh"] == 0 else jnp.logical_not(lo_head)
        qm = jnp.where(sel, u["q"], jnp.zeros((n, LANES), BF16))
        c["sp"] = _dot_nt(qm, u["kp"])
        c["sc"] = _dot_nt(qm, u["kc"])

    def stage_max(c):
        u = unit(c["r"], c["j"])
        c["sp"] = jnp.where(u["pm"], c["sp"], NEG_BIG)
        c["sc"] = jnp.where(mask_cur, c["sc"], NEG_BIG)
        c["m"] = jnp.max(jnp.maximum(c["sp"], c["sc"]), axis=-1, keepdims=True)

    def stage_pv(c):
        u = unit(c["r"], c["j"])
        ep = jnp.exp(c.pop("sp") - c["m"]).astype(BF16)
        ec = jnp.exp(c.pop("sc") - c["m"]).astype(BF16)
        res = _dot(jnp.concatenate([ep, ec], axis=1), u["vo"])
        c["acc"] = res[:, :LANES]
        c["den"] = res[:, LANES:]

    def stage_out(c, pairs):
        c["o"] = c.pop("acc") / c["den"]
        c["lse"] = c["m"] + jnp.log(c.pop("den"))
        key = (c["r"], c["j"])
        pairs.setdefault(key, []).append(c)
        if len(pairs[key]) == 2:
            c0, c1 = pairs.pop(key)
            u = unit(c["r"], c["j"])
            idx = pl.ds(u["start"], n, stride=dil) if dil > 1 else pl.ds(u["start"], n)
            o_ref[idx, :] = jnp.where(lo_head, c0["o"], c1["o"])
            lse_ref[idx, :] = jnp.where(lo_head, c0["lse"], c1["lse"])

    chains = [dict(r=r, j=j, hh=hh) for r in range(dil) for j in range(n_blocks)
              for hh in range(2)]
    stages = (stage_scores, stage_max, stage_pv)
    pairs = {}
    total = len(chains)
    for step in range(total + len(stages) * DIL_SKEW):
        for s, fn in enumerate(stages):
            i = step - s * DIL_SKEW
            if 0 <= i < total:
                fn(chains[i])
        i = step - len(stages) * DIL_SKEW
        if 0 <= i < total:
            stage_out(chains[i], pairs)


def _dilated_group(qa, ka, va, g, batch, seq):
    window, dil = DIL_CONFIGS[g]
    assert window // dil == DIL_BLOCK and seq % DIL_CHUNK == 0
    nc = seq // DIL_CHUNK
    hist = DIL_BLOCK * dil
    t = batch * seq
    n_pairs = DIL_OUT // LANES
    cur = pl.BlockSpec((DIL_CHUNK, LANES), lambda b, c, pr: (b * nc + c, g * n_pairs + pr))
    prev = pl.BlockSpec(
        (hist, LANES),
        lambda b, c, pr: (jnp.maximum((b * nc + c) * (DIL_CHUNK // hist) - 1, 0),
                          g * n_pairs + pr))
    out = pl.BlockSpec((DIL_CHUNK, LANES), lambda b, c, pr: (b * nc + c, pr))
    return pl.pallas_call(
        functools.partial(_dil_kernel, dil),
        grid=(batch, nc, n_pairs),
        in_specs=[cur, prev, cur, prev, cur],
        out_specs=[out, out],
        out_shape=[jax.ShapeDtypeStruct((t, DIL_OUT), F32)] * 2,
        compiler_params=_cparams(3),
        name=f"dilated_d{dil}",
    )(qa, ka, ka, va, va)


def _sb_kernel(q_ref, k_ref, v_ref, o_ref, acc_scr, r_scr):
    blk = SB_BLOCK
    nq = q_ref.shape[0] // blk
    row = lax.broadcasted_iota(jnp.int32, (blk, blk), 0)
    col = lax.broadcasted_iota(jnp.int32, (blk, blk), 1)
    tri = col < row
    lo_head = col < HEAD_DIM
    cum_mat = jnp.where(row > col, 1.0, 0.0).astype(BF16)

    def head_q(q, hh):
        sel = lo_head if hh == 0 else jnp.logical_not(lo_head)
        return jnp.where(sel, q, jnp.zeros_like(q))

    def tile_scores(z, mask):
        sp = jnp.maximum(z, 0.0) + jnp.log(1.0 + jnp.exp(-jnp.abs(z)))
        lom = -sp
        if mask is not None:
            lom = jnp.where(mask, lom, 0.0)
        hi = lom.astype(BF16)
        lo = (lom - hi.astype(F32)).astype(BF16)
        after = _dot(hi, cum_mat) + _dot(lo, cum_mat)
        return z - sp + after, jnp.sum(lom, axis=-1, keepdims=True)

    def tile_weights(logit, mask, r_acc):
        a = jnp.exp(logit + r_acc)
        if mask is not None:
            a = jnp.where(mask, a, 0.0)
        return a.astype(BF16)

    def q_group(blocks):
        chains = []
        for qi, n_tiles in blocks:
            qoff = qi * blk
            koff = (qi - (n_tiles - 1)) * blk
            if not isinstance(qi, int):
                qoff, koff = pl.multiple_of(qoff, blk), pl.multiple_of(koff, blk)
            q = q_ref[pl.ds(qoff, blk), :]
            kw = k_ref[pl.ds(koff, n_tiles * blk), :]
            vw = v_ref[pl.ds(koff, n_tiles * blk), :]
            for hh in range(2):
                chains.append(dict(z=_dot_nt(head_q(q, hh), kw), vw=vw, n=n_tiles,
                                   r=jnp.zeros((blk, 1), F32), acc=jnp.zeros((blk, blk), F32)))
        max_tiles = max(c["n"] for c in chains)
        tiles = [(c, c["n"] - 1 - j) for j in range(max_tiles) for c in chains if j < c["n"]]
        pending = []
        for step in range(len(tiles) + SB_SKEW):
            if step < len(tiles):
                c, w = tiles[step]
                mask = tri if w == c["n"] - 1 else None
                logit, rowsum = tile_scores(c["z"][:, w * blk:(w + 1) * blk], mask)
                pending.append((c, w, mask, logit, rowsum))
            if step >= SB_SKEW:
                c, w, mask, logit, rowsum = pending[step - SB_SKEW]
                a = tile_weights(logit, mask, c["r"])
                c["acc"] = c["acc"] + _dot(a, c["vw"][w * blk:(w + 1) * blk, :])
                c["r"] = c["r"] + rowsum
        for b, (qi, _) in enumerate(blocks):
            qoff = qi * blk if isinstance(qi, int) else pl.multiple_of(qi * blk, blk)
            c0, c1 = chains[2 * b], chains[2 * b + 1]
            r_scr[0, pl.ds(qoff, blk), :] = jnp.broadcast_to(c0["r"], (blk, blk))
            r_scr[1, pl.ds(qoff, blk), :] = jnp.broadcast_to(c1["r"], (blk, blk))
            acc_pair = jnp.where(lo_head, c0["acc"], c1["acc"])
            acc_scr[pl.ds(qoff, blk), :] = acc_pair
            o_ref[pl.ds(qoff, blk), :] = acc_pair.astype(BF16)

    first_tail = min(SB_WINDOW, nq)
    n_static = first_tail + (nq - first_tail) % SB_GROUP
    q_group([(qi, min(qi + 1, SB_WINDOW)) for qi in range(n_static)])

    def q_loop(it, carry):
        q0 = n_static + it * SB_GROUP
        q_group([(q0 + j, SB_WINDOW) for j in range(SB_GROUP)])
        return carry

    lax.fori_loop(0, (nq - n_static) // SB_GROUP, q_loop, 0)


    @pl.when(jnp.max(r_scr[:, first_tail * blk:, :]) > SB_DEAD_LOG)
    def _tail():
        def q_tail(qi, carry):
            qoff = pl.multiple_of(qi * blk, blk)
            q = q_ref[pl.ds(qoff, blk), :]
            acc0 = acc_scr[pl.ds(qoff, blk), :]
            res = []
            for hh in range(2):
                qm = head_q(q, hh)
                r0 = r_scr[hh, pl.ds(qoff, blk), :]

                def cond(c):
                    return jnp.logical_and(c[0] >= 0, c[3] > SB_DEAD_LOG)

                def body(c, qm=qm):
                    off = pl.multiple_of(c[0] * blk, blk)
                    z = _dot_nt(qm, k_ref[pl.ds(off, blk), :])
                    logit, rowsum = tile_scores(z, None)
                    a = tile_weights(logit, None, c[1])
                    r_new = c[1] + rowsum
                    acc_new = c[2] + _dot(a, v_ref[pl.ds(off, blk), :])
                    return c[0] - 1, r_new, acc_new, jnp.max(r_new)

                c0 = (qi - SB_WINDOW, r0, acc0, jnp.max(r0))
                res.append(lax.while_loop(cond, body, c0)[2])
            o_ref[pl.ds(qoff, blk), :] = jnp.where(lo_head, res[0], res[1]).astype(BF16)
            return carry

        lax.fori_loop(first_tail, nq, q_tail, 0)


def _stick_breaking(qb, kb, vb, batch, seq):
    view = lambda a: a.reshape(batch, seq, SB_WIDTH)
    spec = pl.BlockSpec((None, seq, LANES), lambda b, hp: (b, 0, hp))
    o = pl.pallas_call(
        _sb_kernel,
        grid=(batch, SB_WIDTH // LANES),
        in_specs=[spec, spec, spec],
        out_specs=spec,
        out_shape=jax.ShapeDtypeStruct((batch, seq, SB_WIDTH), BF16),
        scratch_shapes=[pltpu.VMEM((seq, LANES), F32), pltpu.VMEM((2, seq, LANES), F32)],
        compiler_params=_cparams(2),
        name="stick_breaking",
    )(view(qb), view(kb), view(vb))
    return o.reshape(batch * seq, SB_WIDTH)


def _merge_kernel(seq, x_ref, o1_ref, o2_ref, o3_ref, l1_ref, l2_ref, l3_ref, sb_ref,
                  uc_ref, uh_ref, g_ref, wg_ref, bg_ref, wp_ref, ps_ref,
                  wa_ref, wb_ref, wc_ref, wo_ref, out_ref,
                  h_scr, ub_scr, yc_scr, m_scr):
    i = pl.program_id(0)
    tm, d = x_ref.shape
    x = x_ref[...]
    h_scr[...] = _rms(x, g_ref[...]).astype(BF16)

    l1, l2, l3 = l1_ref[...], l2_ref[...], l3_ref[...]
    m = jnp.maximum(jnp.maximum(l1, l2), l3)
    e1, e2, e3 = jnp.exp(l1 - m), jnp.exp(l2 - m), jnp.exp(l3 - m)
    dil = ((e1 * o1_ref[...] + e2 * o2_ref[...] + e3 * o3_ref[...]) / (e1 + e2 + e3)).astype(BF16)

    row0 = (i * tm) % seq
    halo = jnp.where(row0 == 0, 0.0, uh_ref[...])
    ub_scr[0:POOL_HALO, :] = halo
    ub_scr[POOL_HALO:POOL_HALO + tm, :] = uc_ref[...]
    t_in_seq = row0 + lax.broadcasted_iota(jnp.int32, (tm, 1), 0)
    for g, w in enumerate(POOL_WINDOWS):
        sl = slice(g * POOL_GROUP, (g + 1) * POOL_GROUP)
        u = ub_scr[POOL_HALO:POOL_HALO + tm, sl]
        acc = u
        for j in range(1, w):
            acc = acc + ub_scr[POOL_HALO - j:POOL_HALO - j + tm, sl]
        cnt = jnp.minimum(t_in_seq + 1, w).astype(F32)
        pooled = acc / cnt - u
        y = _dot(pooled.astype(BF16), wp_ref[g]) * ps_ref[:, sl]
        yc_scr[:, sl] = y.astype(BF16)

    h = h_scr[...]
    sb = sb_ref[...]
    yc_in = yc_scr[...]
    cw = 256
    for c in range(d // cw):
        sl = slice(c * cw, (c + 1) * cw)
        gates = []
        for br in range(N_BRANCH):
            gsl = slice(br * d + c * cw, br * d + (c + 1) * cw)
            gates.append(jax.nn.sigmoid(_dot(h, wg_ref[:, gsl]) + bg_ref[:, gsl]))
        merged = (gates[0] * _dot(dil, wa_ref[:, sl]) + gates[1] * _dot(sb, wb_ref[:, sl])
                  + gates[2] * _dot(yc_in, wc_ref[:, sl]))
        m_scr[:, sl] = merged.astype(BF16)
    out_ref[...] = x + _dot(m_scr[...], wo_ref[...])


def _merge(x2d, dil_outs, sb_out, uc, seq, g, w_gate, b_gate, w_pool, pool_scale,
           w_up_a, w_up_b, w_up_c, w_out):
    t, d = x2d.shape
    tm = 256
    row = lambda width: pl.BlockSpec((tm, width), lambda i: (i, 0))
    halo = pl.BlockSpec((POOL_HALO, POOL_WIDTH),
                        lambda i: (jnp.maximum(i * (tm // POOL_HALO) - 1, 0), 0))
    (o1, l1), (o2, l2), (o3, l3) = dil_outs
    weights = (g, w_gate, b_gate, w_pool, pool_scale, w_up_a, w_up_b, w_up_c, w_out)
    return pl.pallas_call(
        functools.partial(_merge_kernel, seq),
        grid=(t // tm,),
        in_specs=[row(d)] + [row(DIL_OUT)] * 6 + [row(SB_WIDTH), row(POOL_WIDTH), halo]
                 + [_const_spec(w.shape) for w in weights],
        out_specs=row(d),
        out_shape=jax.ShapeDtypeStruct((t, d), F32),
        scratch_shapes=[pltpu.VMEM((tm, d), BF16),
                        pltpu.VMEM((POOL_HALO + tm, POOL_WIDTH), F32),
                        pltpu.VMEM((tm, POOL_WIDTH), BF16),
                        pltpu.VMEM((tm, d), BF16)],
        compiler_params=_cparams(1),
        name="merge",
    )(x2d, o1, o2, o3, l1, l2, l3, sb_out, uc, uc, *weights)


ROUTER_EXP_LANE0 = MOE_GROUPS
MOE_TILE = 1024
MOE_SUB = 256
MOE_SLOTS = 96


def _moe_kernel(x_ref, g_ref, wr_ref, br_ref, wg_ref, wu_ref, wd_ref, out_ref,
                xn_scr, gate_scr, ghi_scr, glo_scr, sel_scr, selt_scr, acc_scr, ovf_smem):
    grp = pl.program_id(1)
    tm, d = x_ref.shape
    n_sub = tm // MOE_SUB
    f = wd_ref.shape[0] // EXP_PER_GROUP
    lane_f = lax.broadcasted_iota(jnp.int32, (tm, LANES), 1).astype(F32)

    @pl.when(grp == 0)
    def _route():
        xn = _rms(x_ref[...], g_ref[...])
        a_hi = xn.astype(BF16)
        xn_scr[...] = a_hi
        a_lo = (xn - a_hi.astype(F32)).astype(BF16)
        w = wr_ref[...]
        w_hi = w.astype(BF16)
        w_lo = (w - w_hi.astype(F32)).astype(BF16)
        logits = _dot(a_hi, w_hi) + _dot(a_hi, w_lo) + _dot(a_lo, w_hi) + br_ref[...]
        is_grp = lane_f < MOE_GROUPS
        lg = jnp.where(is_grp, logits, NEG_BIG)
        mg = jnp.max(lg, axis=-1, keepdims=True)
        p1 = 1.0 / jnp.sum(jnp.exp(lg - mg), axis=-1, keepdims=True)
        gi = jnp.min(jnp.where(lg == mg, lane_f, float(LANES)), axis=-1, keepdims=True)
        lane_grp = jnp.floor((lane_f - ROUTER_EXP_LANE0) * (1.0 / EXP_PER_GROUP))
        in_grp = jnp.logical_and(
            jnp.logical_and(lane_f >= ROUTER_EXP_LANE0, lane_f < ROUTER_EXP_LANE0 + N_EXPERTS),
            lane_grp == gi)
        le = jnp.where(in_grp, logits, NEG_BIG)
        v1 = jnp.max(le, axis=-1, keepdims=True)
        i1 = jnp.min(jnp.where(le == v1, lane_f, float(LANES)), axis=-1, keepdims=True)
        le2 = jnp.where(lane_f == i1, NEG_BIG, le)
        v2 = jnp.max(le2, axis=-1, keepdims=True)
        i2 = jnp.min(jnp.where(le2 == v2, lane_f, float(LANES)), axis=-1, keepdims=True)
        t = jnp.exp(v2 - v1)
        w1 = p1 / (1.0 + t)
        w2 = p1 * t / (1.0 + t)
        gate = jnp.where(lane_f == i1, w1, 0.0) + jnp.where(lane_f == i2, w2, 0.0)
        gate_scr[...] = gate
        g_hi = gate.astype(BF16)
        ghi_scr[...] = g_hi
        glo_scr[...] = (gate - g_hi.astype(F32)).astype(BF16)
        acc_scr[...] = jnp.zeros_like(acc_scr)

        onehot = jnp.where(lane_f == gi, 1.0, 0.0).astype(BF16)
        r = lax.broadcasted_iota(jnp.int32, (MOE_SUB, MOE_SUB), 0)
        c = lax.broadcasted_iota(jnp.int32, (MOE_SUB, MOE_SUB), 1)
        earlier = jnp.where(c < r, 1.0, 0.0).astype(BF16)
        fullest = jnp.zeros((1, LANES), F32)
        gi_b = jnp.broadcast_to(gi, (tm, LANES))
        for s in range(n_sub):
            rs = slice(s * MOE_SUB, (s + 1) * MOE_SUB)
            oh = onehot[rs, :]
            ahead = _dot(earlier, oh)
            slot = jnp.sum(jnp.where(oh > 0, ahead, 0.0), axis=-1, keepdims=True)
            sel_b = jnp.where(slot < LANES, slot + float(LANES) * gi_b[rs, :], -1.0)
            sel_scr[rs, :] = sel_b
            selt_scr[:, rs] = jnp.transpose(sel_b)[:8, :]
            fullest = jnp.maximum(fullest, jnp.sum(oh.astype(F32), axis=0, keepdims=True))
        lane1 = lax.broadcasted_iota(jnp.int32, (1, LANES), 1)
        for gidx in range(MOE_GROUPS):
            most = jnp.max(jnp.where(lane1 == gidx, fullest, 0.0))
            ovf_smem[gidx] = (most > MOE_SLOTS).astype(jnp.int32)

    def experts(rows, gates):
        hg = _dot(rows, wg_ref[...])
        hu = _dot(rows, wu_ref[...])
        hid = (hg * jax.nn.sigmoid(hg)) * hu
        lane_r = lax.broadcasted_iota(jnp.int32, gates.shape, 1)
        parts = []
        for j in range(EXP_PER_GROUP):
            here = lane_r == ROUTER_EXP_LANE0 + EXP_PER_GROUP * grp + j
            gcol = jnp.sum(jnp.where(here, gates, 0.0), axis=-1, keepdims=True)
            parts.append((hid[:, j * f:(j + 1) * f] * gcol).astype(BF16))
        return _dot(jnp.concatenate(parts, axis=1), wd_ref[...])

    key = (grp * LANES).astype(F32)

    @pl.when(ovf_smem[grp] == 0)
    def _compact():
        slot_rows = lax.broadcasted_iota(jnp.int32, (MOE_SLOTS, MOE_SUB), 0).astype(F32) + key
        slot_cols = lax.broadcasted_iota(jnp.int32, (MOE_SUB, LANES), 1).astype(F32) + key
        xs, gs = [], []
        for s in range(n_sub):
            rs = slice(s * MOE_SUB, (s + 1) * MOE_SUB)
            pick = jnp.where(selt_scr[0:1, rs] == slot_rows, 1.0, 0.0).astype(BF16)
            xs.append(_dot(pick, xn_scr[rs, :]).astype(BF16))
            gs.append(_dot(pick, ghi_scr[rs, :]) + _dot(pick, glo_scr[rs, :]))
        y = experts(jnp.concatenate(xs, axis=0), jnp.concatenate(gs, axis=0))
        y = jnp.concatenate([y, jnp.zeros((LANES - MOE_SLOTS, d), F32)], axis=0)
        y_hi = y.astype(BF16)
        y_lo = (y - y_hi.astype(F32)).astype(BF16)
        for s in range(n_sub):
            rs = slice(s * MOE_SUB, (s + 1) * MOE_SUB)
            ys = slice(s * MOE_SLOTS, s * MOE_SLOTS + LANES)
            place = jnp.where(sel_scr[rs, :] == slot_cols, 1.0, 0.0).astype(BF16)
            acc_scr[rs, :] += _dot(jnp.concatenate([place, place], axis=1),
                                   jnp.concatenate([y_hi[ys, :], y_lo[ys, :]], axis=0))

    @pl.when(ovf_smem[grp] != 0)
    def _dense():
        for s in range(n_sub):
            rs = slice(s * MOE_SUB, (s + 1) * MOE_SUB)
            acc_scr[rs, :] += experts(xn_scr[rs, :], gate_scr[rs, :])

    @pl.when(grp == MOE_GROUPS - 1)
    def _finish():
        out_ref[...] = x_ref[...] + acc_scr[...]


def _moe(x2d, g, w_router, b_router, w_exp_gate, w_exp_up, w_exp_down):
    t, d = x2d.shape
    f = w_exp_gate.shape[-1]
    tm = MOE_TILE
    gf = EXP_PER_GROUP * f
    by_group = lambda w: w.reshape(MOE_GROUPS, EXP_PER_GROUP, d, f).transpose(
        0, 2, 1, 3).reshape(MOE_GROUPS, d, gf)
    row = pl.BlockSpec((tm, d), lambda i, e: (i, 0))
    return pl.pallas_call(
        _moe_kernel,
        grid=(t // tm, MOE_GROUPS),
        in_specs=[row, _const_spec((1, d)), _const_spec(w_router.shape),
                  _const_spec(b_router.shape),
                  pl.BlockSpec((None, d, gf), lambda i, e: (e, 0, 0)),
                  pl.BlockSpec((None, d, gf), lambda i, e: (e, 0, 0)),
                  pl.BlockSpec((None, gf, d), lambda i, e: (e, 0, 0))],
        out_specs=row,
        out_shape=jax.ShapeDtypeStruct((t, d), F32),
        scratch_shapes=[pltpu.VMEM((tm, d), BF16), pltpu.VMEM((tm, LANES), F32),
                        pltpu.VMEM((tm, LANES), BF16), pltpu.VMEM((tm, LANES), BF16),
                        pltpu.VMEM((tm, LANES), F32), pltpu.VMEM((8, tm), F32),
                        pltpu.VMEM((tm, d), F32), pltpu.SMEM((MOE_GROUPS,), jnp.int32)],
        compiler_params=_cparams(2),
        name="moe",
    )(x2d, g, w_router, b_router, by_group(w_exp_gate), by_group(w_exp_up),
      w_exp_down.reshape(MOE_GROUPS, gf, d))


def _router_weights(w_grp, b_grp, w_exp, b_exp):
    d = w_grp.shape[0]
    w = jnp.concatenate([w_grp, w_exp.transpose(1, 0, 2).reshape(d, N_EXPERTS)], axis=1)
    b = jnp.concatenate([b_grp, b_exp.reshape(N_EXPERTS)])
    pad = LANES - w.shape[1]
    return jnp.pad(w, ((0, 0), (0, pad))), jnp.pad(b, (0, pad))[None, :]


def _ple_kernel(final, x_ref, p_ref, g_ref, wgate_ref, win_ref, gf_ref, out_ref):
    x = x_ref[...]
    gate = jax.nn.sigmoid(_dot(_rms(x, g_ref[...]).astype(BF16), wgate_ref[...]))
    y = x + _dot(p_ref[...].astype(BF16), win_ref[...]) * gate
    if final:
        y = _rms(y, gf_ref[...])
    out_ref[...] = y


def _ple(x2d, p2d, g, w_ple_gate, w_ple_in, g_final, final):
    t, d = x2d.shape
    tm = 512
    row = lambda width: pl.BlockSpec((tm, width), lambda i: (i, 0))
    return pl.pallas_call(
        functools.partial(_ple_kernel, final),
        grid=(t // tm,),
        in_specs=[row(d), row(p2d.shape[1]), _const_spec((1, d)),
                  _const_spec(w_ple_gate.shape), _const_spec(w_ple_in.shape),
                  _const_spec((1, d))],
        out_specs=row(d),
        out_shape=jax.ShapeDtypeStruct((t, d), F32),
        compiler_params=_cparams(1),
        name="ple",
    )(x2d, p2d, g, w_ple_gate, w_ple_in, g_final)


def kernel(x, p, positions, norm_mix, w_in, w_gate, b_gate, w_pool, pool_scale, w_up_a, w_up_b, w_up_c, w_out, norm_moe, w_router_grp, b_router_grp, w_router_exp, b_router_exp, w_exp_gate, w_exp_up, w_exp_down, norm_ple, w_ple_in, w_ple_gate, norm_final):
    batch, seq, d = x.shape
    depth = w_in.shape[0]
    t = batch * seq
    x2d = x.reshape(t, d)
    cos_t, sin_t = _rope_tables(positions)
    bf = lambda a: a.astype(BF16)
    for i in range(depth):
        qa, ka, va, qb, kb, vb, uc = _inproj(x2d, norm_mix[i][None, :], bf(w_in[i]), cos_t, sin_t)
        dil_outs = [_dilated_group(qa, ka, va, g, batch, seq) for g in range(len(DIL_CONFIGS))]
        sb_out = _stick_breaking(qb, kb, vb, batch, seq)
        x2d = _merge(x2d, dil_outs, sb_out, uc, seq, norm_mix[i][None, :], bf(w_gate[i]),
                     b_gate[i][None, :], bf(w_pool[i]), pool_scale[i][None, :],
                     bf(w_up_a[i]), bf(w_up_b[i]), bf(w_up_c[i]), bf(w_out[i]))
        w_r, b_r = _router_weights(w_router_grp[i], b_router_grp[i],
                                   w_router_exp[i], b_router_exp[i])
        x2d = _moe(x2d, norm_moe[i][None, :], w_r, b_r, bf(w_exp_gate[i]), bf(w_exp_up[i]),
                   bf(w_exp_down[i]))
        x2d = _ple(x2d, p[i].reshape(t, -1), norm_ple[i][None, :], bf(w_ple_gate[i]),
                   bf(w_ple_in[i]), norm_final[None, :], i == depth - 1)
    return x2d.reshape(batch, seq, d)
```

```python
import functools

import jax
import jax.numpy as jnp
from jax import lax
from jax.experimental import pallas as pl
from jax.experimental.pallas import tpu as pltpu

F32 = jnp.float32
BF16 = jnp.bfloat16

HEAD_DIM = 64
ROPE_THETA = 10000.0
DIL_CONFIGS = ((128, 1), (512, 4), (2048, 16))
DIL_HEADS = 4
DIL_OUT = DIL_HEADS * HEAD_DIM
DIL_WIDTH = len(DIL_CONFIGS) * DIL_OUT
DIL_BLOCK = 128
DIL_CHUNK = 2048
DIL_SKEW = 3
INPROJ_TILE = 512
DIL_PERMUTED = (16,)
SB_HEADS = 8
SB_WIDTH = SB_HEADS * HEAD_DIM
SB_BLOCK = 128
POOL_WINDOWS = (2, 4, 8, 16)
POOL_GROUP = 128
POOL_WIDTH = len(POOL_WINDOWS) * POOL_GROUP
POOL_HALO = 16
N_BRANCH = 3
MOE_GROUPS = 4
EXP_PER_GROUP = 4
N_EXPERTS = MOE_GROUPS * EXP_PER_GROUP
EPS = 1e-6

LANES = 128
NEG_BIG = -1e30
SB_DEAD_LOG = -105.0
SB_WINDOW = 3
SB_GROUP = 4
SB_SKEW = 8
VMEM_LIMIT = 56 * 1024 * 1024


def _cparams(n_axes):
    return pltpu.CompilerParams(
        dimension_semantics=("arbitrary",) * n_axes, vmem_limit_bytes=VMEM_LIMIT)


def _rms(x, g):
    ms = jnp.mean(x * x, axis=-1, keepdims=True)
    return x * lax.rsqrt(ms + EPS) * g


def _dot(a, b):
    return jnp.dot(a, b, preferred_element_type=F32)


def _dot_nt(a, b):
    return lax.dot_general(a, b, (((1,), (1,)), ((), ())), preferred_element_type=F32)


def _const_spec(shape):
    zeros = (0,) * len(shape)
    return pl.BlockSpec(shape, lambda *_: zeros)


def _rope_table_kernel(pos_ref, inv_ref, sgn_ref, cos_ref, sin_ref):
    ang = pos_ref[...].astype(F32) * inv_ref[...]
    cos_ref[...] = jnp.cos(ang)
    sin_ref[...] = jnp.sin(ang) * sgn_ref[...]


def _rope_tables(positions):
    t = positions.size
    half = HEAD_DIM // 2
    inv = ROPE_THETA ** (-jnp.arange(half, dtype=F32) / half)
    inv_row = jnp.tile(inv, LANES // half)[None, :]
    sgn_row = jnp.tile(jnp.concatenate([-jnp.ones(half, F32), jnp.ones(half, F32)]),
                       LANES // HEAD_DIM)[None, :]
    tm = 2048
    return pl.pallas_call(
        _rope_table_kernel,
        grid=(t // tm,),
        in_specs=[pl.BlockSpec((tm, 1), lambda i: (i, 0)),
                  _const_spec((1, LANES)), _const_spec((1, LANES))],
        out_specs=[pl.BlockSpec((tm, LANES), lambda i: (i, 0))] * 2,
        out_shape=[jax.ShapeDtypeStruct((t, LANES), F32)] * 2,
        compiler_params=_cparams(1),
        name="rope_table",
    )(positions.reshape(t, 1), inv_row, sgn_row)


def _inproj_kernel(x_ref, g_ref, w_ref, cos_ref, sin_ref,
                   qa_ref, ka_ref, va_ref, qb_ref, kb_ref, vb_ref, uc_ref, h_scr, perm_scr):
    h_scr[...] = _rms(x_ref[...], g_ref[...]).astype(BF16)
    cos = cos_ref[...]
    sin = sin_ref[...]
    tm = cos.shape[0]
    lane = lax.broadcasted_iota(jnp.int32, (tm, LANES), 1)
    first_half = (lane & (HEAD_DIM // 2)) == 0
    scale = HEAD_DIM ** -0.5

    def rope(t):
        partner = jnp.where(first_half, pltpu.roll(t, LANES - HEAD_DIM // 2, 1),
                            pltpu.roll(t, HEAD_DIM // 2, 1))
        return t * cos + partner * sin

    def store_dilated(ref, which, c, val):
        sl = slice(c * LANES, (c + 1) * LANES)
        dil = DIL_CONFIGS[c // (DIL_OUT // LANES)][1]
        if dil not in DIL_PERMUTED:
            ref[:, sl] = val
            return
        per = tm // dil
        perm_scr[which, c % (DIL_OUT // LANES)] = val
        for r in range(dil):
            ref[r * per:(r + 1) * per, sl] = perm_scr[which, c % (DIL_OUT // LANES),
                                                      pl.ds(r, per, stride=dil), :]

    h = h_scr[...]
    col = 0
    pq = _dot(h, w_ref[:, col:col + DIL_WIDTH])
    for c in range(DIL_WIDTH // LANES):
        store_dilated(qa_ref, 0, c, rope(pq[:, c * LANES:(c + 1) * LANES]) * scale)
    col += DIL_WIDTH
    pk = _dot(h, w_ref[:, col:col + DIL_WIDTH])
    for c in range(DIL_WIDTH // LANES):
        store_dilated(ka_ref, 1, c, rope(pk[:, c * LANES:(c + 1) * LANES]))
    col += DIL_WIDTH
    pv = _dot(h, w_ref[:, col:col + DIL_WIDTH])
    for c in range(DIL_WIDTH // LANES):
        store_dilated(va_ref, 2, c, pv[:, c * LANES:(c + 1) * LANES])
    col += DIL_WIDTH
    qb_ref[...] = (_dot(h, w_ref[:, col:col + SB_WIDTH]) * scale).astype(BF16)
    col += SB_WIDTH
    kb_ref[...] = _dot(h, w_ref[:, col:col + SB_WIDTH]).astype(BF16)
    col += SB_WIDTH
    vb_ref[...] = _dot(h, w_ref[:, col:col + SB_WIDTH]).astype(BF16)
    col += SB_WIDTH
    uc_ref[...] = _dot(h, w_ref[:, col:col + POOL_WIDTH])


def _inproj(x2d, g, w_in, cos_t, sin_t):
    t, d = x2d.shape
    tm = INPROJ_TILE
    row = lambda width: pl.BlockSpec((tm, width), lambda i: (i, 0))
    widths = (DIL_WIDTH,) * 3 + (SB_WIDTH,) * 3 + (POOL_WIDTH,)
    dtypes = (F32,) * 3 + (BF16,) * 3 + (F32,)
    return pl.pallas_call(
        _inproj_kernel,
        grid=(t // tm,),
        in_specs=[row(d), _const_spec((1, d)), _const_spec(w_in.shape),
                  row(LANES), row(LANES)],
        out_specs=[row(w) for w in widths],
        out_shape=[jax.ShapeDtypeStruct((t, w), dt) for w, dt in zip(widths, dtypes)],
        scratch_shapes=[pltpu.VMEM((tm, d), BF16),
                        pltpu.VMEM((3, DIL_OUT // LANES, tm, LANES), F32)],
        compiler_params=_cparams(1),
        name="in_proj",
    )(x2d, g, w_in, cos_t, sin_t)


def _dil_kernel(dil, q_ref, kp_ref, kc_ref, vp_ref, vc_ref, o_ref, lse_ref):
    has_prev = pl.program_id(1) > 0
    n = DIL_BLOCK
    n_blocks = q_ref.shape[0] // (n * dil)
    qi = lax.broadcasted_iota(jnp.int32, (n, n), 0)
    kj = lax.broadcasted_iota(jnp.int32, (n, n), 1)
    mask_prev = kj >= qi
    mask_prev_first = jnp.logical_and(mask_prev, has_prev)
    mask_cur = kj <= qi
    lo_head = kj < HEAD_DIM
    ones = jnp.ones((2 * n, LANES), BF16)

    def rows(ref, r, j):
        if dil in DIL_PERMUTED:
            per = INPROJ_TILE // dil
            pieces = [ref[ti * INPROJ_TILE + r * per:ti * INPROJ_TILE + (r + 1) * per, :]
                      for ti in range(j * (n // per), (j + 1) * (n // per))]
            return jnp.concatenate(pieces, axis=0).astype(BF16)
        start = j * n * dil + r
        return ref[pl.ds(start, n, stride=dil) if dil > 1 else pl.ds(start, n), :].astype(BF16)

    units = {}

    def unit(r, j):
        if (r, j) not in units:
            kc, vc = rows(kc_ref, r, j), rows(vc_ref, r, j)
            if j == 0:
                kp, vp = rows(kp_ref, r, 0), rows(vp_ref, r, 0)
            else:
                kp, vp = units[(r, j - 1)]["kc"], units[(r, j - 1)]["vc"]
            vo = jnp.concatenate([jnp.concatenate([vp, vc], axis=0), ones], axis=1)
            units[(r, j)] = dict(q=rows(q_ref, r, j), kc=kc, vc=vc, kp=kp, vo=vo,
                                 pm=mask_prev_first if j == 0 else mask_prev,
                                 start=j * n * dil + r)
        return units[(r, j)]

    def stage_scores(c):
        u = unit(c["r"], c["j"])
        sel = lo_head if c["hh"] == 0 else jnp.logical_not(lo_head)
        qm = jnp.where(sel, u["q"], jnp.zeros((n, LANES), BF16))
        c["sp"] = _dot_nt(qm, u["kp"])
        c["sc"] = _dot_nt(qm, u["kc"])

    def stage_max(c):
        u = unit(c["r"], c["j"])
        c["sp"] = jnp.where(u["pm"], c["sp"], NEG_BIG)
        c["sc"] = jnp.where(mask_cur, c["sc"], NEG_BIG)
        c["m"] = jnp.max(jnp.maximum(c["sp"], c["sc"]), axis=-1, keepdims=True)

    def stage_pv(c):
        u = unit(c["r"], c["j"])
        ep = jnp.exp(c.pop("sp") - c["m"]).astype(BF16)
        ec = jnp.exp(c.pop("sc") - c["m"]).astype(BF16)
        res = _dot(jnp.concatenate([ep, ec], axis=1), u["vo"])
        c["acc"] = res[:, :LANES]
        c["den"] = res[:, LANES:]

    def stage_out(c, pairs):
        c["o"] = c.pop("acc") / c["den"]
        c["lse"] = c["m"] + jnp.log(c.pop("den"))
        key = (c["r"], c["j"])
        pairs.setdefault(key, []).append(c)
        if len(pairs[key]) == 2:
            c0, c1 = pairs.pop(key)
            u = unit(c["r"], c["j"])
            idx = pl.ds(u["start"], n, stride=dil) if dil > 1 else pl.ds(u["start"], n)
            o_ref[idx, :] = jnp.where(lo_head, c0["o"], c1["o"])
            lse_ref[idx, :] = jnp.where(lo_head, c0["lse"], c1["lse"])

    chains = [dict(r=r, j=j, hh=hh) for r in range(dil) for j in range(n_blocks)
              for hh in range(2)]
    stages = (stage_scores, stage_max, stage_pv)
    pairs = {}
    total = len(chains)
    for step in range(total + len(stages) * DIL_SKEW):
        for s, fn in enumerate(stages):
            i = step - s * DIL_SKEW
            if 0 <= i < total:
                fn(chains[i])
        i = step - len(stages) * DIL_SKEW
        if 0 <= i < total:
            stage_out(chains[i], pairs)


def _dilated_group(qa, ka, va, g, batch, seq):
    window, dil = DIL_CONFIGS[g]
    assert window // dil == DIL_BLOCK and seq % DIL_CHUNK == 0
    nc = seq // DIL_CHUNK
    hist = DIL_BLOCK * dil
    t = batch * seq
    n_pairs = DIL_OUT // LANES
    cur = pl.BlockSpec((DIL_CHUNK, LANES), lambda b, c, pr: (b * nc + c, g * n_pairs + pr))
    prev = pl.BlockSpec(
        (hist, LANES),
        lambda b, c, pr: (jnp.maximum((b * nc + c) * (DIL_CHUNK // hist) - 1, 0),
                          g * n_pairs + pr))
    out = pl.BlockSpec((DIL_CHUNK, LANES), lambda b, c, pr: (b * nc + c, pr))
    return pl.pallas_call(
        functools.partial(_dil_kernel, dil),
        grid=(batch, nc, n_pairs),
        in_specs=[cur, prev, cur, prev, cur],
        out_specs=[out, out],
        out_shape=[jax.ShapeDtypeStruct((t, DIL_OUT), F32)] * 2,
        compiler_params=_cparams(3),
        name=f"dilated_d{dil}",
    )(qa, ka, ka, va, va)


def _sb_kernel(q_ref, k_ref, v_ref, o_ref, acc_scr, r_scr):
    blk = SB_BLOCK
    nq = q_ref.shape[0] // blk
    row = lax.broadcasted_iota(jnp.int32, (blk, blk), 0)
    col = lax.broadcasted_iota(jnp.int32, (blk, blk), 1)
    tri = col < row
    lo_head = col < HEAD_DIM
    cum_mat = jnp.where(row > col, 1.0, 0.0).astype(BF16)

    def head_q(q, hh):
        sel = lo_head if hh == 0 else jnp.logical_not(lo_head)
        return jnp.where(sel, q, jnp.zeros_like(q))

    def tile_scores(z, mask):
        sp = jnp.maximum(z, 0.0) + jnp.log(1.0 + jnp.exp(-jnp.abs(z)))
        lom = -sp
        if mask is not None:
            lom = jnp.where(mask, lom, 0.0)
        hi = lom.astype(BF16)
        lo = (lom - hi.astype(F32)).astype(BF16)
        after = _dot(hi, cum_mat) + _dot(lo, cum_mat)
        return z - sp + after, jnp.sum(lom, axis=-1, keepdims=True)

    def tile_weights(logit, mask, r_acc):
        a = jnp.exp(logit + r_acc)
        if mask is not None:
            a = jnp.where(mask, a, 0.0)
        return a.astype(BF16)

    def q_group(blocks):
        chains = []
        for qi, n_tiles in blocks:
            qoff = qi * blk
            koff = (qi - (n_tiles - 1)) * blk
            if not isinstance(qi, int):
                qoff, koff = pl.multiple_of(qoff, blk), pl.multiple_of(koff, blk)
            q = q_ref[pl.ds(qoff, blk), :]
            kw = k_ref[pl.ds(koff, n_tiles * blk), :]
            vw = v_ref[pl.ds(koff, n_tiles * blk), :]
            for hh in range(2):
                chains.append(dict(z=_dot_nt(head_q(q, hh), kw), vw=vw, n=n_tiles,
                                   r=jnp.zeros((blk, 1), F32), acc=jnp.zeros((blk, blk), F32)))
        max_tiles = max(c["n"] for c in chains)
        tiles = [(c, c["n"] - 1 - j) for j in range(max_tiles) for c in chains if j < c["n"]]
        pending = []
        for step in range(len(tiles) + SB_SKEW):
            if step < len(tiles):
                c, w = tiles[step]
                mask = tri if w == c["n"] - 1 else None
                logit, rowsum = tile_scores(c["z"][:, w * blk:(w + 1) * blk], mask)
                pending.append((c, w, mask, logit, rowsum))
            if step >= SB_SKEW:
                c, w, mask, logit, rowsum = pending[step - SB_SKEW]
                a = tile_weights(logit, mask, c["r"])
                c["acc"] = c["acc"] + _dot(a, c["vw"][w * blk:(w + 1) * blk, :])
                c["r"] = c["r"] + rowsum
        for b, (qi, _) in enumerate(blocks):
            qoff = qi * blk if isinstance(qi, int) else pl.multiple_of(qi * blk, blk)
            c0, c1 = chains[2 * b], chains[2 * b + 1]
            r_scr[0, pl.ds(qoff, blk), :] = jnp.broadcast_to(c0["r"], (blk, blk))
            r_scr[1, pl.ds(qoff, blk), :] = jnp.broadcast_to(c1["r"], (blk, blk))
            acc_pair = jnp.where(lo_head, c0["acc"], c1["acc"])
            acc_scr[pl.ds(qoff, blk), :] = acc_pair
            o_ref[pl.ds(qoff, blk), :] = acc_pair.astype(BF16)

    first_tail = min(SB_WINDOW, nq)
    n_static = first_tail + (nq - first_tail) % SB_GROUP
    q_group([(qi, min(qi + 1, SB_WINDOW)) for qi in range(n_static)])

    def q_loop(it, carry):
        q0 = n_static + it * SB_GROUP
        q_group([(q0 + j, SB_WINDOW) for j in range(SB_GROUP)])
        return carry

    lax.fori_loop(0, (nq - n_static) // SB_GROUP, q_loop, 0)


    @pl.when(jnp.max(r_scr[:, first_tail * blk:, :]) > SB_DEAD_LOG)
    def _tail():
        def q_tail(qi, carry):
            qoff = pl.multiple_of(qi * blk, blk)
            q = q_ref[pl.ds(qoff, blk), :]
            acc0 = acc_scr[pl.ds(qoff, blk), :]
            res = []
            for hh in range(2):
                qm = head_q(q, hh)
                r0 = r_scr[hh, pl.ds(qoff, blk), :]

                def cond(c):
                    return jnp.logical_and(c[0] >= 0, c[3] > SB_DEAD_LOG)

                def body(c, qm=qm):
                    off = pl.multiple_of(c[0] * blk, blk)
                    z = _dot_nt(qm, k_ref[pl.ds(off, blk), :])
                    logit, rowsum = tile_scores(z, None)
                    a = tile_weights(logit, None, c[1])
                    r_new = c[1] + rowsum
                    acc_new = c[2] + _dot(a, v_ref[pl.ds(off, blk), :])
                    return c[0] - 1, r_new, acc_new, jnp.max(r_new)

                c0 = (qi - SB_WINDOW, r0, acc0, jnp.max(r0))
                res.append(lax.while_loop(cond, body, c0)[2])
            o_ref[pl.ds(qoff, blk), :] = jnp.where(lo_head, res[0], res[1]).astype(BF16)
            return carry

        lax.fori_loop(first_tail, nq, q_tail, 0)


def _stick_breaking(qb, kb, vb, batch, seq):
    view = lambda a: a.reshape(batch, seq, SB_WIDTH)
    spec = pl.BlockSpec((None, seq, LANES), lambda b, hp: (b, 0, hp))
    o = pl.pallas_call(
        _sb_kernel,
        grid=(batch, SB_WIDTH // LANES),
        in_specs=[spec, spec, spec],
        out_specs=spec,
        out_shape=jax.ShapeDtypeStruct((batch, seq, SB_WIDTH), BF16),
        scratch_shapes=[pltpu.VMEM((seq, LANES), F32), pltpu.VMEM((2, seq, LANES), F32)],
        compiler_params=_cparams(2),
        name="stick_breaking",
    )(view(qb), view(kb), view(vb))
    return o.reshape(batch * seq, SB_WIDTH)


def _merge_kernel(seq, x_ref, o1_ref, o2_ref, o3_ref, l1_ref, l2_ref, l3_ref, sb_ref,
                  uc_ref, uh_ref, g_ref, wg_ref, bg_ref, wp_ref, ps_ref,
                  wa_ref, wb_ref, wc_ref, wo_ref, out_ref,
                  h_scr, ub_scr, yc_scr, m_scr):
    i = pl.program_id(0)
    tm, d = x_ref.shape
    x = x_ref[...]
    h_scr[...] = _rms(x, g_ref[...]).astype(BF16)

    l1, l2, l3 = l1_ref[...], l2_ref[...], l3_ref[...]
    m = jnp.maximum(jnp.maximum(l1, l2), l3)
    e1, e2, e3 = jnp.exp(l1 - m), jnp.exp(l2 - m), jnp.exp(l3 - m)
    dil = ((e1 * o1_ref[...] + e2 * o2_ref[...] + e3 * o3_ref[...]) / (e1 + e2 + e3)).astype(BF16)

    row0 = (i * tm) % seq
    halo = jnp.where(row0 == 0, 0.0, uh_ref[...])
    ub_scr[0:POOL_HALO, :] = halo
    ub_scr[POOL_HALO:POOL_HALO + tm, :] = uc_ref[...]
    t_in_seq = row0 + lax.broadcasted_iota(jnp.int32, (tm, 1), 0)
    for g, w in enumerate(POOL_WINDOWS):
        sl = slice(g * POOL_GROUP, (g + 1) * POOL_GROUP)
        u = ub_scr[POOL_HALO:POOL_HALO + tm, sl]
        acc = u
        for j in range(1, w):
            acc = acc + ub_scr[POOL_HALO - j:POOL_HALO - j + tm, sl]
        cnt = jnp.minimum(t_in_seq + 1, w).astype(F32)
        pooled = acc / cnt - u
        y = _dot(pooled.astype(BF16), wp_ref[g]) * ps_ref[:, sl]
        yc_scr[:, sl] = y.astype(BF16)

    h = h_scr[...]
    sb = sb_ref[...]
    yc_in = yc_scr[...]
    cw = 256
    for c in range(d // cw):
        sl = slice(c * cw, (c + 1) * cw)
        gates = []
        for br in range(N_BRANCH):
            gsl = slice(br * d + c * cw, br * d + (c + 1) * cw)
            gates.append(jax.nn.sigmoid(_dot(h, wg_ref[:, gsl]) + bg_ref[:, gsl]))
        merged = (gates[0] * _dot(dil, wa_ref[:, sl]) + gates[1] * _dot(sb, wb_ref[:, sl])
                  + gates[2] * _dot(yc_in, wc_ref[:, sl]))
        m_scr[:, sl] = merged.astype(BF16)
    out_ref[...] = x + _dot(m_scr[...], wo_ref[...])


def _merge(x2d, dil_outs, sb_out, uc, seq, g, w_gate, b_gate, w_pool, pool_scale,
           w_up_a, w_up_b, w_up_c, w_out):
    t, d = x2d.shape
    tm = 256
    row = lambda width: pl.BlockSpec((tm, width), lambda i: (i, 0))
    halo = pl.BlockSpec((POOL_HALO, POOL_WIDTH),
                        lambda i: (jnp.maximum(i * (tm // POOL_HALO) - 1, 0), 0))
    (o1, l1), (o2, l2), (o3, l3) = dil_outs
    weights = (g, w_gate, b_gate, w_pool, pool_scale, w_up_a, w_up_b, w_up_c, w_out)
    return pl.pallas_call(
        functools.partial(_merge_kernel, seq),
        grid=(t // tm,),
        in_specs=[row(d)] + [row(DIL_OUT)] * 6 + [row(SB_WIDTH), row(POOL_WIDTH), halo]
                 + [_const_spec(w.shape) for w in weights],
        out_specs=row(d),
        out_shape=jax.ShapeDtypeStruct((t, d), F32),
        scratch_shapes=[pltpu.VMEM((tm, d), BF16),
                        pltpu.VMEM((POOL_HALO + tm, POOL_WIDTH), F32),
                        pltpu.VMEM((tm, POOL_WIDTH), BF16),
                        pltpu.VMEM((tm, d), BF16)],
        compiler_params=_cparams(1),
        name="merge",
    )(x2d, o1, o2, o3, l1, l2, l3, sb_out, uc, uc, *weights)


ROUTER_EXP_LANE0 = MOE_GROUPS
MOE_TILE = 1024
MOE_SUB = 256
MOE_SLOTS = 96


def _moe_kernel(final, x_ref, g_ref, wr_ref, br_ref, wg_ref, wu_ref, wd_ref,
                p_ref, gp_ref, wpg_ref, wpi_ref, gf_ref, out_ref,
                xn_scr, gate_scr, ghi_scr, glo_scr, sel_scr, selt_scr, acc_scr, ovf_smem):
    grp = pl.program_id(1)
    tm, d = x_ref.shape
    n_sub = tm // MOE_SUB
    lane_f = lax.broadcasted_iota(jnp.int32, (tm, LANES), 1).astype(F32)

    @pl.when(grp == 0)
    def _route():
        xn = _rms(x_ref[...], g_ref[...])
        a_hi = xn.astype(BF16)
        xn_scr[...] = a_hi
        a_lo = (xn - a_hi.astype(F32)).astype(BF16)
        r_hi = _dot(a_hi, wr_ref[...])
        logits = (r_hi[:, :LANES] + r_hi[:, LANES:] + _dot(a_lo, wr_ref[:, :LANES])
                  + br_ref[...])
        is_grp = lane_f < MOE_GROUPS
        lg = jnp.where(is_grp, logits, NEG_BIG)
        mg = jnp.max(lg, axis=-1, keepdims=True)
        p1 = 1.0 / jnp.sum(jnp.exp(lg - mg), axis=-1, keepdims=True)
        gi = jnp.min(jnp.where(lg == mg, lane_f, float(LANES)), axis=-1, keepdims=True)
        lane_grp = jnp.floor((lane_f - ROUTER_EXP_LANE0) * (1.0 / EXP_PER_GROUP))
        in_grp = jnp.logical_and(
            jnp.logical_and(lane_f >= ROUTER_EXP_LANE0, lane_f < ROUTER_EXP_LANE0 + N_EXPERTS),
            lane_grp == gi)
        le = jnp.where(in_grp, logits, NEG_BIG)
        v1 = jnp.max(le, axis=-1, keepdims=True)
        i1 = jnp.min(jnp.where(le == v1, lane_f, float(LANES)), axis=-1, keepdims=True)
        le2 = jnp.where(lane_f == i1, NEG_BIG, le)
        v2 = jnp.max(le2, axis=-1, keepdims=True)
        i2 = jnp.min(jnp.where(le2 == v2, lane_f, float(LANES)), axis=-1, keepdims=True)
        t = jnp.exp(v2 - v1)
        w1 = p1 / (1.0 + t)
        w2 = p1 * t / (1.0 + t)
        gate = jnp.where(lane_f == i1, w1, 0.0) + jnp.where(lane_f == i2, w2, 0.0)
        gate_scr[...] = gate
        g_hi = gate.astype(BF16)
        ghi_scr[...] = g_hi
        glo_scr[...] = (gate - g_hi.astype(F32)).astype(BF16)
        acc_scr[...] = jnp.zeros_like(acc_scr)

        onehot = jnp.where(lane_f == gi, 1.0, 0.0).astype(BF16)
        r = lax.broadcasted_iota(jnp.int32, (MOE_SUB, MOE_SUB), 0)
        c = lax.broadcasted_iota(jnp.int32, (MOE_SUB, MOE_SUB), 1)
        earlier = jnp.where(c < r, 1.0, 0.0).astype(BF16)
        fullest = jnp.zeros((1, LANES), F32)
        gi_b = jnp.broadcast_to(gi, (tm, LANES))
        for s in range(n_sub):
            rs = slice(s * MOE_SUB, (s + 1) * MOE_SUB)
            oh = onehot[rs, :]
            ahead = _dot(earlier, oh)
            slot = jnp.sum(jnp.where(oh > 0, ahead, 0.0), axis=-1, keepdims=True)
            sel_b = jnp.where(slot < LANES, slot + float(LANES) * gi_b[rs, :], -1.0)
            sel_scr[rs, :] = sel_b
            selt_scr[:, rs] = jnp.transpose(sel_b)[:8, :]
            fullest = jnp.maximum(fullest, jnp.sum(oh.astype(F32), axis=0, keepdims=True))
        lane1 = lax.broadcasted_iota(jnp.int32, (1, LANES), 1)
        for gidx in range(MOE_GROUPS):
            most = jnp.max(jnp.where(lane1 == gidx, fullest, 0.0))
            ovf_smem[gidx] = (most > MOE_SLOTS).astype(jnp.int32)

    def experts(rows, gates):
        lane_r = lax.broadcasted_iota(jnp.int32, gates.shape, 1)
        y = None
        for j in range(EXP_PER_GROUP):
            hg = _dot(rows, wg_ref[j])
            hu = _dot(rows, wu_ref[j])
            here = lane_r == ROUTER_EXP_LANE0 + EXP_PER_GROUP * grp + j
            gcol = jnp.sum(jnp.where(here, gates, 0.0), axis=-1, keepdims=True)
            hid = ((hg * jax.nn.sigmoid(hg)) * hu * gcol).astype(BF16)
            yj = _dot(hid, wd_ref[j])
            y = yj if y is None else y + yj
        return y

    key = (grp * LANES).astype(F32)

    @pl.when(ovf_smem[grp] == 0)
    def _compact():
        slot_rows = lax.broadcasted_iota(jnp.int32, (MOE_SLOTS, MOE_SUB), 0).astype(F32) + key
        slot_cols = lax.broadcasted_iota(jnp.int32, (MOE_SUB, LANES), 1).astype(F32) + key
        xs, gs = [], []
        for s in range(n_sub):
            rs = slice(s * MOE_SUB, (s + 1) * MOE_SUB)
            pick = jnp.where(selt_scr[0:1, rs] == slot_rows, 1.0, 0.0).astype(BF16)
            xs.append(_dot(pick, xn_scr[rs, :]).astype(BF16))
            gs.append(_dot(pick, ghi_scr[rs, :]) + _dot(pick, glo_scr[rs, :]))
        y = experts(jnp.concatenate(xs, axis=0), jnp.concatenate(gs, axis=0))
        y = jnp.concatenate([y, jnp.zeros((LANES - MOE_SLOTS, d), F32)], axis=0)
        y_hi = y.astype(BF16)
        y_lo = (y - y_hi.astype(F32)).astype(BF16)
        for s in range(n_sub):
            rs = slice(s * MOE_SUB, (s + 1) * MOE_SUB)
            ys = slice(s * MOE_SLOTS, s * MOE_SLOTS + LANES)
            place = jnp.where(sel_scr[rs, :] == slot_cols, 1.0, 0.0).astype(BF16)
            acc_scr[rs, :] += _dot(jnp.concatenate([place, place], axis=1),
                                   jnp.concatenate([y_hi[ys, :], y_lo[ys, :]], axis=0))

    @pl.when(ovf_smem[grp] != 0)
    def _dense():
        for s in range(n_sub):
            rs = slice(s * MOE_SUB, (s + 1) * MOE_SUB)
            acc_scr[rs, :] += experts(xn_scr[rs, :], gate_scr[rs, :])

    @pl.when(grp == MOE_GROUPS - 1)
    def _finish():
        for s in range(n_sub):
            rs = slice(s * MOE_SUB, (s + 1) * MOE_SUB)
            x = x_ref[rs, :] + acc_scr[rs, :]
            gate = jax.nn.sigmoid(_dot(_rms(x, gp_ref[...]).astype(BF16), wpg_ref[...]))
            y = x + _dot(p_ref[rs, :].astype(BF16), wpi_ref[...]) * gate
            if final:
                y = _rms(y, gf_ref[...])
            out_ref[rs, :] = y


def _moe_ple(x2d, g, w_router, b_router, w_exp_gate, w_exp_up, w_exp_down,
             p2d, g_ple, w_ple_gate, w_ple_in, g_final, final):
    t, d = x2d.shape
    f = w_exp_gate.shape[-1]
    tm = MOE_TILE
    row = pl.BlockSpec((tm, d), lambda i, e: (i, 0))
    group_w = lambda a, b: pl.BlockSpec((EXP_PER_GROUP, a, b), lambda i, e: (e, 0, 0))
    return pl.pallas_call(
        functools.partial(_moe_kernel, final),
        grid=(t // tm, MOE_GROUPS),
        in_specs=[row, _const_spec((1, d)), _const_spec(w_router.shape),
                  _const_spec(b_router.shape),
                  group_w(d, f), group_w(d, f), group_w(f, d),
                  pl.BlockSpec((tm, p2d.shape[1]), lambda i, e: (i, 0)),
                  _const_spec((1, d)), _const_spec(w_ple_gate.shape),
                  _const_spec(w_ple_in.shape), _const_spec((1, d))],
        out_specs=row,
        out_shape=jax.ShapeDtypeStruct((t, d), F32),
        scratch_shapes=[pltpu.VMEM((tm, d), BF16), pltpu.VMEM((tm, LANES), F32),
                        pltpu.VMEM((tm, LANES), BF16), pltpu.VMEM((tm, LANES), BF16),
                        pltpu.VMEM((tm, LANES), F32), pltpu.VMEM((8, tm), F32),
                        pltpu.VMEM((tm, d), F32), pltpu.SMEM((MOE_GROUPS,), jnp.int32)],
        compiler_params=_cparams(2),
        name="moe_ple",
    )(x2d, g, w_router, b_router, w_exp_gate, w_exp_up, w_exp_down,
      p2d, g_ple, w_ple_gate, w_ple_in, g_final)


def _router_weights(w_grp, b_grp, w_exp, b_exp):
    d = w_grp.shape[0]
    w = jnp.concatenate([w_grp, w_exp.transpose(1, 0, 2).reshape(d, N_EXPERTS)], axis=1)
    b = jnp.concatenate([b_grp, b_exp.reshape(N_EXPERTS)])
    pad = LANES - w.shape[1]
    w = jnp.pad(w, ((0, 0), (0, pad)))
    w_hi = w.astype(BF16)
    w_lo = (w - w_hi.astype(F32)).astype(BF16)
    return jnp.concatenate([w_hi, w_lo], axis=1), jnp.pad(b, (0, pad))[None, :]


def kernel(x, p, positions, norm_mix, w_in, w_gate, b_gate, w_pool, pool_scale, w_up_a, w_up_b, w_up_c, w_out, norm_moe, w_router_grp, b_router_grp, w_router_exp, b_router_exp, w_exp_gate, w_exp_up, w_exp_down, norm_ple, w_ple_in, w_ple_gate, norm_final):
    batch, seq, d = x.shape
    depth = w_in.shape[0]
    t = batch * seq
    x2d = x.reshape(t, d)
    cos_t, sin_t = _rope_tables(positions)
    bf = lambda a: a.astype(BF16)
    for i in range(depth):
        qa, ka, va, qb, kb, vb, uc = _inproj(x2d, norm_mix[i][None, :], bf(w_in[i]), cos_t, sin_t)
        dil_outs = [_dilated_group(qa, ka, va, g, batch, seq) for g in range(len(DIL_CONFIGS))]
        sb_out = _stick_breaking(qb, kb, vb, batch, seq)
        x2d = _merge(x2d, dil_outs, sb_out, uc, seq, norm_mix[i][None, :], bf(w_gate[i]),
                     b_gate[i][None, :], bf(w_pool[i]), pool_scale[i][None, :],
                     bf(w_up_a[i]), bf(w_up_b[i]), bf(w_up_c[i]), bf(w_out[i]))
        w_r, b_r = _router_weights(w_router_grp[i], b_router_grp[i],
                                   w_router_exp[i], b_router_exp[i])
        x2d = _moe_ple(x2d, norm_moe[i][None, :], w_r, b_r, bf(w_exp_gate[i]),
                       bf(w_exp_up[i]), bf(w_exp_down[i]), p[i].reshape(t, -1),
                       norm_ple[i][None, :], bf(w_ple_gate[i]), bf(w_ple_in[i]),
                       norm_final[None, :], i == depth - 1)
    return x2d.reshape(batch, seq, d)
```

```python
import functools

import jax
import jax.numpy as jnp
from jax import lax
from jax.experimental import pallas as pl
from jax.experimental.pallas import tpu as pltpu

F32 = jnp.float32
BF16 = jnp.bfloat16

HEAD_DIM = 64
ROPE_THETA = 10000.0
DIL_CONFIGS = ((128, 1), (512, 4), (2048, 16))
DIL_HEADS = 4
DIL_OUT = DIL_HEADS * HEAD_DIM
DIL_WIDTH = len(DIL_CONFIGS) * DIL_OUT
DIL_BLOCK = 128
DIL_CHUNK = 2048
DIL_SKEW = 3
INPROJ_TILE = 512
MERGE_TILE = 512
DIL_PERMUTED = (16,)
SB_HEADS = 8
SB_WIDTH = SB_HEADS * HEAD_DIM
SB_BLOCK = 128
POOL_WINDOWS = (2, 4, 8, 16)
POOL_GROUP = 128
POOL_WIDTH = len(POOL_WINDOWS) * POOL_GROUP
POOL_HALO = 16
N_BRANCH = 3
MOE_GROUPS = 4
EXP_PER_GROUP = 4
N_EXPERTS = MOE_GROUPS * EXP_PER_GROUP
EPS = 1e-6

LANES = 128
NEG_BIG = -1e30
SB_DEAD_LOG = -105.0
SB_WINDOW = 3
SB_GROUP = 4
SB_SKEW = 8
VMEM_LIMIT = 56 * 1024 * 1024


def _cparams(n_axes):
    return pltpu.CompilerParams(
        dimension_semantics=("arbitrary",) * n_axes, vmem_limit_bytes=VMEM_LIMIT)


def _rms(x, g):
    ms = jnp.mean(x * x, axis=-1, keepdims=True)
    return x * lax.rsqrt(ms + EPS) * g


def _dot(a, b):
    return jnp.dot(a, b, preferred_element_type=F32)


def _dot_nt(a, b):
    return lax.dot_general(a, b, (((1,), (1,)), ((), ())), preferred_element_type=F32)


def _const_spec(shape):
    zeros = (0,) * len(shape)
    return pl.BlockSpec(shape, lambda *_: zeros)


def _layer_weight_spec(w, layer):
    tail = (0,) * (w.ndim - 1)
    return pl.BlockSpec((None,) + w.shape[1:], lambda *_: (layer,) + tail,
                        pipeline_mode=pl.Buffered(1))


def _cast_weights_once(first_step, pairs):
    @pl.when(first_step)
    def _cast():
        for src, dst in pairs:
            if len(src.shape) == 3:
                for k in range(src.shape[0]):
                    dst[k] = src[k].astype(BF16)
            else:
                step = 2 * LANES
                for c in range(0, src.shape[1], step):
                    dst[:, c:c + step] = src[:, c:c + step].astype(BF16)


def _rope_table_kernel(pos_ref, inv_ref, sgn_ref, cos_ref, sin_ref):
    ang = pos_ref[...].astype(F32) * inv_ref[...]
    cos_ref[...] = jnp.cos(ang)
    sin_ref[...] = jnp.sin(ang) * sgn_ref[...]


def _rope_tables(positions):
    t = positions.size
    half = HEAD_DIM // 2
    inv = ROPE_THETA ** (-jnp.arange(half, dtype=F32) / half)
    inv_row = jnp.tile(inv, LANES // half)[None, :]
    sgn_row = jnp.tile(jnp.concatenate([-jnp.ones(half, F32), jnp.ones(half, F32)]),
                       LANES // HEAD_DIM)[None, :]
    tm = 2048
    return pl.pallas_call(
        _rope_table_kernel,
        grid=(t // tm,),
        in_specs=[pl.BlockSpec((tm, 1), lambda i: (i, 0)),
                  _const_spec((1, LANES)), _const_spec((1, LANES))],
        out_specs=[pl.BlockSpec((tm, LANES), lambda i: (i, 0))] * 2,
        out_shape=[jax.ShapeDtypeStruct((t, LANES), F32)] * 2,
        compiler_params=_cparams(1),
        name="rope_table",
    )(positions.reshape(t, 1), inv_row, sgn_row)


def _inproj_kernel(x_ref, g_ref, w32_ref, cos_ref, sin_ref,
                   qa_ref, ka_ref, va_ref, qb_ref, kb_ref, vb_ref, uc_ref,
                   h_scr, perm_scr, w_ref):
    _cast_weights_once(pl.program_id(0) == 0, [(w32_ref, w_ref)])
    h_scr[...] = _rms(x_ref[...], g_ref[...]).astype(BF16)
    cos = cos_ref[...]
    sin = sin_ref[...]
    tm = cos.shape[0]
    lane = lax.broadcasted_iota(jnp.int32, (tm, LANES), 1)
    first_half = (lane & (HEAD_DIM // 2)) == 0
    scale = HEAD_DIM ** -0.5

    def rope(t):
        partner = jnp.where(first_half, pltpu.roll(t, LANES - HEAD_DIM // 2, 1),
                            pltpu.roll(t, HEAD_DIM // 2, 1))
        return t * cos + partner * sin

    def store_dilated(ref, which, c, val):
        sl = slice(c * LANES, (c + 1) * LANES)
        dil = DIL_CONFIGS[c // (DIL_OUT // LANES)][1]
        if dil not in DIL_PERMUTED:
            ref[:, sl] = val
            return
        per = tm // dil
        perm_scr[which, c % (DIL_OUT // LANES)] = val
        for r in range(dil):
            ref[r * per:(r + 1) * per, sl] = perm_scr[which, c % (DIL_OUT // LANES),
                                                      pl.ds(r, per, stride=dil), :]

    h = h_scr[...]
    col = 0
    pq = _dot(h, w_ref[:, col:col + DIL_WIDTH])
    for c in range(DIL_WIDTH // LANES):
        store_dilated(qa_ref, 0, c, rope(pq[:, c * LANES:(c + 1) * LANES]) * scale)
    col += DIL_WIDTH
    pk = _dot(h, w_ref[:, col:col + DIL_WIDTH])
    for c in range(DIL_WIDTH // LANES):
        store_dilated(ka_ref, 1, c, rope(pk[:, c * LANES:(c + 1) * LANES]))
    col += DIL_WIDTH
    pv = _dot(h, w_ref[:, col:col + DIL_WIDTH])
    for c in range(DIL_WIDTH // LANES):
        store_dilated(va_ref, 2, c, pv[:, c * LANES:(c + 1) * LANES])
    col += DIL_WIDTH
    qb_ref[...] = (_dot(h, w_ref[:, col:col + SB_WIDTH]) * scale).astype(BF16)
    col += SB_WIDTH
    kb_ref[...] = _dot(h, w_ref[:, col:col + SB_WIDTH]).astype(BF16)
    col += SB_WIDTH
    vb_ref[...] = _dot(h, w_ref[:, col:col + SB_WIDTH]).astype(BF16)
    col += SB_WIDTH
    uc_ref[...] = _dot(h, w_ref[:, col:col + POOL_WIDTH])


def _inproj(x2d, g, w_in, layer, cos_t, sin_t):
    t, d = x2d.shape
    tm = INPROJ_TILE
    row = lambda width: pl.BlockSpec((tm, width), lambda i: (i, 0))
    widths = (DIL_WIDTH,) * 3 + (SB_WIDTH,) * 3 + (POOL_WIDTH,)
    dtypes = (F32,) * 3 + (BF16,) * 3 + (F32,)
    return pl.pallas_call(
        _inproj_kernel,
        grid=(t // tm,),
        in_specs=[row(d), _const_spec((1, d)), _layer_weight_spec(w_in, layer),
                  row(LANES), row(LANES)],
        out_specs=[row(w) for w in widths],
        out_shape=[jax.ShapeDtypeStruct((t, w), dt) for w, dt in zip(widths, dtypes)],
        scratch_shapes=[pltpu.VMEM((tm, d), BF16),
                        pltpu.VMEM((3, DIL_OUT // LANES, tm, LANES), F32),
                        pltpu.VMEM(w_in.shape[1:], BF16)],
        compiler_params=_cparams(1),
        name="in_proj",
    )(x2d, g, w_in, cos_t, sin_t)


def _dil_kernel(dil, q_ref, kp_ref, kc_ref, vp_ref, vc_ref, o_ref, lse_ref):
    has_prev = pl.program_id(1) > 0
    n = DIL_BLOCK
    n_blocks = q_ref.shape[0] // (n * dil)
    qi = lax.broadcasted_iota(jnp.int32, (n, n), 0)
    kj = lax.broadcasted_iota(jnp.int32, (n, n), 1)
    mask_prev = kj >= qi
    mask_prev_first = jnp.logical_and(mask_prev, has_prev)
    mask_cur = kj <= qi
    lo_head = kj < HEAD_DIM
    ones = jnp.ones((2 * n, LANES), BF16)

    def rows(ref, r, j):
        if dil in DIL_PERMUTED:
            per = INPROJ_TILE // dil
            pieces = [ref[ti * INPROJ_TILE + r * per:ti * INPROJ_TILE + (r + 1) * per, :]
                      for ti in range(j * (n // per), (j + 1) * (n // per))]
            return jnp.concatenate(pieces, axis=0).astype(BF16)
        start = j * n * dil + r
        return ref[pl.ds(start, n, stride=dil) if dil > 1 else pl.ds(start, n), :].astype(BF16)

    units = {}

    def unit(r, j):
        if (r, j) not in units:
            kc, vc = rows(kc_ref, r, j), rows(vc_ref, r, j)
            if j == 0:
                kp, vp = rows(kp_ref, r, 0), rows(vp_ref, r, 0)
            else:
                kp, vp = units[(r, j - 1)]["kc"], units[(r, j - 1)]["vc"]
            vo = jnp.concatenate([jnp.concatenate([vp, vc], axis=0), ones], axis=1)
            units[(r, j)] = dict(q=rows(q_ref, r, j), kc=kc, vc=vc, kp=kp, vo=vo,
                                 pm=mask_prev_first if j == 0 else mask_prev,
                                 start=j * n * dil + r)
        return units[(r, j)]

    def stage_scores(c):
        u = unit(c["r"], c["j"])
        sel = lo_head if c["hh"] == 0 else jnp.logical_not(lo_head)
        qm = jnp.where(sel, u["q"], jnp.zeros((n, LANES), BF16))
        c["sp"] = _dot_nt(qm, u["kp"])
        c["sc"] = _dot_nt(qm, u["kc"])

    def stage_max(c):
        u = unit(c["r"], c["j"])
        c["sp"] = jnp.where(u["pm"], c["sp"], NEG_BIG)
        c["sc"] = jnp.where(mask_cur, c["sc"], NEG_BIG)
        c["m"] = jnp.max(jnp.maximum(c["sp"], c["sc"]), axis=-1, keepdims=True)

    def stage_pv(c):
        u = unit(c["r"], c["j"])
        ep = jnp.exp(c.pop("sp") - c["m"]).astype(BF16)
        ec = jnp.exp(c.pop("sc") - c["m"]).astype(BF16)
        res = _dot(jnp.concatenate([ep, ec], axis=1), u["vo"])
        c["acc"] = res[:, :LANES]
        c["den"] = res[:, LANES:]

    def stage_out(c, pairs):
        c["o"] = c.pop("acc") / c["den"]
        c["lse"] = c["m"] + jnp.log(c.pop("den"))
        key = (c["r"], c["j"])
        pairs.setdefault(key, []).append(c)
        if len(pairs[key]) == 2:
            c0, c1 = pairs.pop(key)
            u = unit(c["r"], c["j"])
            idx = pl.ds(u["start"], n, stride=dil) if dil > 1 else pl.ds(u["start"], n)
            o_ref[idx, :] = jnp.where(lo_head, c0["o"], c1["o"])
            lse_ref[idx, :] = jnp.where(lo_head, c0["lse"], c1["lse"])

    chains = [dict(r=r, j=j, hh=hh) for r in range(dil) for j in range(n_blocks)
              for hh in range(2)]
    stages = (stage_scores, stage_max, stage_pv)
    pairs = {}
    total = len(chains)
    for step in range(total + len(stages) * DIL_SKEW):
        for s, fn in enumerate(stages):
            i = step - s * DIL_SKEW
            if 0 <= i < total:
                fn(chains[i])
        i = step - len(stages) * DIL_SKEW
        if 0 <= i < total:
            stage_out(chains[i], pairs)


def _dilated_group(qa, ka, va, g, batch, seq):
    window, dil = DIL_CONFIGS[g]
    assert window // dil == DIL_BLOCK and seq % DIL_CHUNK == 0
    nc = seq // DIL_CHUNK
    hist = DIL_BLOCK * dil
    t = batch * seq
    n_pairs = DIL_OUT // LANES
    cur = pl.BlockSpec((DIL_CHUNK, LANES), lambda b, c, pr: (b * nc + c, g * n_pairs + pr))
    prev = pl.BlockSpec(
        (hist, LANES),
        lambda b, c, pr: (jnp.maximum((b * nc + c) * (DIL_CHUNK // hist) - 1, 0),
                          g * n_pairs + pr))
    out = pl.BlockSpec((DIL_CHUNK, LANES), lambda b, c, pr: (b * nc + c, pr))
    return pl.pallas_call(
        functools.partial(_dil_kernel, dil),
        grid=(batch, nc, n_pairs),
        in_specs=[cur, prev, cur, prev, cur],
        out_specs=[out, out],
        out_shape=[jax.ShapeDtypeStruct((t, DIL_OUT), F32)] * 2,
        compiler_params=_cparams(3),
        name=f"dilated_d{dil}",
    )(qa, ka, ka, va, va)


def _sb_kernel(q_ref, k_ref, v_ref, o_ref, acc_scr, r_scr):
    blk = SB_BLOCK
    nq = q_ref.shape[0] // blk
    row = lax.broadcasted_iota(jnp.int32, (blk, blk), 0)
    col = lax.broadcasted_iota(jnp.int32, (blk, blk), 1)
    tri = col < row
    lo_head = col < HEAD_DIM
    cum_mat = jnp.where(row > col, 1.0, 0.0).astype(BF16)

    def head_q(q, hh):
        sel = lo_head if hh == 0 else jnp.logical_not(lo_head)
        return jnp.where(sel, q, jnp.zeros_like(q))

    def tile_scores(z, mask):
        sp = jnp.maximum(z, 0.0) + jnp.log(1.0 + jnp.exp(-jnp.abs(z)))
        lom = -sp
        if mask is not None:
            lom = jnp.where(mask, lom, 0.0)
        hi = lom.astype(BF16)
        lo = (lom - hi.astype(F32)).astype(BF16)
        after = _dot(hi, cum_mat) + _dot(lo, cum_mat)
        return z - sp + after, jnp.sum(lom, axis=-1, keepdims=True)

    def tile_weights(logit, mask, r_acc):
        a = jnp.exp(logit + r_acc)
        if mask is not None:
            a = jnp.where(mask, a, 0.0)
        return a.astype(BF16)

    def q_group(blocks):
        chains = []
        for qi, n_tiles in blocks:
            qoff = qi * blk
            koff = (qi - (n_tiles - 1)) * blk
            if not isinstance(qi, int):
                qoff, koff = pl.multiple_of(qoff, blk), pl.multiple_of(koff, blk)
            q = q_ref[pl.ds(qoff, blk), :]
            kw = k_ref[pl.ds(koff, n_tiles * blk), :]
            vw = v_ref[pl.ds(koff, n_tiles * blk), :]
            for hh in range(2):
                chains.append(dict(z=_dot_nt(head_q(q, hh), kw), vw=vw, n=n_tiles,
                                   r=jnp.zeros((blk, 1), F32), acc=jnp.zeros((blk, blk), F32)))
        max_tiles = max(c["n"] for c in chains)
        tiles = [(c, c["n"] - 1 - j) for j in range(max_tiles) for c in chains if j < c["n"]]
        pending = []
        for step in range(len(tiles) + SB_SKEW):
            if step < len(tiles):
                c, w = tiles[step]
                mask = tri if w == c["n"] - 1 else None
                logit, rowsum = tile_scores(c["z"][:, w * blk:(w + 1) * blk], mask)
                pending.append((c, w, mask, logit, rowsum))
            if step >= SB_SKEW:
                c, w, mask, logit, rowsum = pending[step - SB_SKEW]
                a = tile_weights(logit, mask, c["r"])
                c["acc"] = c["acc"] + _dot(a, c["vw"][w * blk:(w + 1) * blk, :])
                c["r"] = c["r"] + rowsum
        for b, (qi, _) in enumerate(blocks):
            qoff = qi * blk if isinstance(qi, int) else pl.multiple_of(qi * blk, blk)
            c0, c1 = chains[2 * b], chains[2 * b + 1]
            r_scr[0, pl.ds(qoff, blk), :] = jnp.broadcast_to(c0["r"], (blk, blk))
            r_scr[1, pl.ds(qoff, blk), :] = jnp.broadcast_to(c1["r"], (blk, blk))
            acc_pair = jnp.where(lo_head, c0["acc"], c1["acc"])
            acc_scr[pl.ds(qoff, blk), :] = acc_pair
            o_ref[pl.ds(qoff, blk), :] = acc_pair.astype(BF16)

    first_tail = min(SB_WINDOW, nq)
    n_static = first_tail + (nq - first_tail) % SB_GROUP
    q_group([(qi, min(qi + 1, SB_WINDOW)) for qi in range(n_static)])

    def q_loop(it, carry):
        q0 = n_static + it * SB_GROUP
        q_group([(q0 + j, SB_WINDOW) for j in range(SB_GROUP)])
        return carry

    lax.fori_loop(0, (nq - n_static) // SB_GROUP, q_loop, 0)


    @pl.when(jnp.max(r_scr[:, first_tail * blk:, :]) > SB_DEAD_LOG)
    def _tail():
        def q_tail(qi, carry):
            qoff = pl.multiple_of(qi * blk, blk)
            q = q_ref[pl.ds(qoff, blk), :]
            acc0 = acc_scr[pl.ds(qoff, blk), :]
            res = []
            for hh in range(2):
                qm = head_q(q, hh)
                r0 = r_scr[hh, pl.ds(qoff, blk), :]

                def cond(c):
                    return jnp.logical_and(c[0] >= 0, c[3] > SB_DEAD_LOG)

                def body(c, qm=qm):
                    off = pl.multiple_of(c[0] * blk, blk)
                    z = _dot_nt(qm, k_ref[pl.ds(off, blk), :])
                    logit, rowsum = tile_scores(z, None)
                    a = tile_weights(logit, None, c[1])
                    r_new = c[1] + rowsum
                    acc_new = c[2] + _dot(a, v_ref[pl.ds(off, blk), :])
                    return c[0] - 1, r_new, acc_new, jnp.max(r_new)

                c0 = (qi - SB_WINDOW, r0, acc0, jnp.max(r0))
                res.append(lax.while_loop(cond, body, c0)[2])
            o_ref[pl.ds(qoff, blk), :] = jnp.where(lo_head, res[0], res[1]).astype(BF16)
            return carry

        lax.fori_loop(first_tail, nq, q_tail, 0)


def _stick_breaking(qb, kb, vb, batch, seq):
    view = lambda a: a.reshape(batch, seq, SB_WIDTH)
    spec = pl.BlockSpec((None, seq, LANES), lambda b, hp: (b, 0, hp))
    o = pl.pallas_call(
        _sb_kernel,
        grid=(batch, SB_WIDTH // LANES),
        in_specs=[spec, spec, spec],
        out_specs=spec,
        out_shape=jax.ShapeDtypeStruct((batch, seq, SB_WIDTH), BF16),
        scratch_shapes=[pltpu.VMEM((seq, LANES), F32), pltpu.VMEM((2, seq, LANES), F32)],
        compiler_params=_cparams(2),
        name="stick_breaking",
    )(view(qb), view(kb), view(vb))
    return o.reshape(batch * seq, SB_WIDTH)


def _merge_kernel(seq, x_ref, o1_ref, o2_ref, o3_ref, l1_ref, l2_ref, l3_ref, sb_ref,
                  uc_ref, uh_ref, g_ref, wg32_ref, bg_ref, wp32_ref, ps_ref,
                  wa32_ref, wb32_ref, wc32_ref, wo32_ref, out_ref,
                  h_scr, ub_scr, yc_scr, m_scr, wg_ref, wp_ref, wa_ref, wb_ref, wc_ref, wo_ref):
    i = pl.program_id(0)
    _cast_weights_once(i == 0, [(wg32_ref, wg_ref), (wp32_ref, wp_ref), (wa32_ref, wa_ref),
                                (wb32_ref, wb_ref), (wc32_ref, wc_ref), (wo32_ref, wo_ref)])
    tm, d = x_ref.shape
    x = x_ref[...]
    h_scr[...] = _rms(x, g_ref[...]).astype(BF16)

    l1, l2, l3 = l1_ref[...], l2_ref[...], l3_ref[...]
    m = jnp.maximum(jnp.maximum(l1, l2), l3)
    e1, e2, e3 = jnp.exp(l1 - m), jnp.exp(l2 - m), jnp.exp(l3 - m)
    dil = ((e1 * o1_ref[...] + e2 * o2_ref[...] + e3 * o3_ref[...]) / (e1 + e2 + e3)).astype(BF16)

    row0 = (i * tm) % seq
    halo = jnp.where(row0 == 0, 0.0, uh_ref[...])
    ub_scr[0:POOL_HALO, :] = halo
    ub_scr[POOL_HALO:POOL_HALO + tm, :] = uc_ref[...]
    t_in_seq = row0 + lax.broadcasted_iota(jnp.int32, (tm, 1), 0)
    for g, w in enumerate(POOL_WINDOWS):
        sl = slice(g * POOL_GROUP, (g + 1) * POOL_GROUP)
        u = ub_scr[POOL_HALO:POOL_HALO + tm, sl]
        acc = u
        for j in range(1, w):
            acc = acc + ub_scr[POOL_HALO - j:POOL_HALO - j + tm, sl]
        cnt = jnp.minimum(t_in_seq + 1, w).astype(F32)
        pooled = acc / cnt - u
        y = _dot(pooled.astype(BF16), wp_ref[g]) * ps_ref[:, sl]
        yc_scr[:, sl] = y.astype(BF16)

    h = h_scr[...]
    sb = sb_ref[...]
    yc_in = yc_scr[...]
    cw = 256
    for c in range(d // cw):
        sl = slice(c * cw, (c + 1) * cw)
        gates = []
        for br in range(N_BRANCH):
            gsl = slice(br * d + c * cw, br * d + (c + 1) * cw)
            gates.append(jax.nn.sigmoid(_dot(h, wg_ref[:, gsl]) + bg_ref[:, gsl]))
        merged = (gates[0] * _dot(dil, wa_ref[:, sl]) + gates[1] * _dot(sb, wb_ref[:, sl])
                  + gates[2] * _dot(yc_in, wc_ref[:, sl]))
        m_scr[:, sl] = merged.astype(BF16)
    out_ref[...] = x + _dot(m_scr[...], wo_ref[...])


def _merge(x2d, dil_outs, sb_out, uc, seq, layer, g, w_gate, b_gate, w_pool, pool_scale,
           w_up_a, w_up_b, w_up_c, w_out):
    t, d = x2d.shape
    tm = MERGE_TILE
    row = lambda width: pl.BlockSpec((tm, width), lambda i: (i, 0))
    halo = pl.BlockSpec((POOL_HALO, POOL_WIDTH),
                        lambda i: (jnp.maximum(i * (tm // POOL_HALO) - 1, 0), 0))
    (o1, l1), (o2, l2), (o3, l3) = dil_outs
    big = (w_gate, w_pool, w_up_a, w_up_b, w_up_c, w_out)
    lw = lambda w: _layer_weight_spec(w, layer)
    return pl.pallas_call(
        functools.partial(_merge_kernel, seq),
        grid=(t // tm,),
        in_specs=[row(d)] + [row(DIL_OUT)] * 6 + [row(SB_WIDTH), row(POOL_WIDTH), halo]
                 + [_const_spec(g.shape), lw(w_gate), _const_spec(b_gate.shape), lw(w_pool),
                    _const_spec(pool_scale.shape), lw(w_up_a), lw(w_up_b), lw(w_up_c),
                    lw(w_out)],
        out_specs=row(d),
        out_shape=jax.ShapeDtypeStruct((t, d), F32),
        scratch_shapes=[pltpu.VMEM((tm, d), BF16),
                        pltpu.VMEM((POOL_HALO + tm, POOL_WIDTH), F32),
                        pltpu.VMEM((tm, POOL_WIDTH), BF16),
                        pltpu.VMEM((tm, d), BF16)]
                       + [pltpu.VMEM(w.shape[1:], BF16) for w in big],
        compiler_params=_cparams(1),
        name="merge",
    )(x2d, o1, o2, o3, l1, l2, l3, sb_out, uc, uc, g, w_gate, b_gate, w_pool, pool_scale,
      w_up_a, w_up_b, w_up_c, w_out)


ROUTER_EXP_LANE0 = MOE_GROUPS
MOE_TILE = 1024
MOE_SUB = 256
MOE_SLOTS = 80


def _moe_kernel(final, x_ref, g_ref, wr_ref, br_ref, wg_ref, wu_ref, wd_ref,
                p_ref, gp_ref, wpg32_ref, wpi32_ref, gf_ref, out_ref,
                xn_scr, gate_scr, ghi_scr, glo_scr, sel_scr, selt_scr, acc_scr, ovf_smem,
                wpg_ref, wpi_ref):
    grp = pl.program_id(1)
    _cast_weights_once(jnp.logical_and(pl.program_id(0) == 0, grp == 0),
                       [(wpg32_ref, wpg_ref), (wpi32_ref, wpi_ref)])
    tm, d = x_ref.shape
    n_sub = tm // MOE_SUB
    lane_f = lax.broadcasted_iota(jnp.int32, (tm, LANES), 1).astype(F32)

    @pl.when(grp == 0)
    def _route():
        xn = _rms(x_ref[...], g_ref[...])
        a_hi = xn.astype(BF16)
        xn_scr[...] = a_hi
        a_lo = (xn - a_hi.astype(F32)).astype(BF16)
        r_hi = _dot(a_hi, wr_ref[...])
        logits = (r_hi[:, :LANES] + r_hi[:, LANES:] + _dot(a_lo, wr_ref[:, :LANES])
                  + br_ref[...])
        is_grp = lane_f < MOE_GROUPS
        lg = jnp.where(is_grp, logits, NEG_BIG)
        mg = jnp.max(lg, axis=-1, keepdims=True)
        p1 = 1.0 / jnp.sum(jnp.exp(lg - mg), axis=-1, keepdims=True)
        gi = jnp.min(jnp.where(lg == mg, lane_f, float(LANES)), axis=-1, keepdims=True)
        lane_grp = jnp.floor((lane_f - ROUTER_EXP_LANE0) * (1.0 / EXP_PER_GROUP))
        in_grp = jnp.logical_and(
            jnp.logical_and(lane_f >= ROUTER_EXP_LANE0, lane_f < ROUTER_EXP_LANE0 + N_EXPERTS),
            lane_grp == gi)
        le = jnp.where(in_grp, logits, NEG_BIG)
        v1 = jnp.max(le, axis=-1, keepdims=True)
        i1 = jnp.min(jnp.where(le == v1, lane_f, float(LANES)), axis=-1, keepdims=True)
        le2 = jnp.where(lane_f == i1, NEG_BIG, le)
        v2 = jnp.max(le2, axis=-1, keepdims=True)
        i2 = jnp.min(jnp.where(le2 == v2, lane_f, float(LANES)), axis=-1, keepdims=True)
        t = jnp.exp(v2 - v1)
        w1 = p1 / (1.0 + t)
        w2 = p1 * t / (1.0 + t)
        gate = jnp.where(lane_f == i1, w1, 0.0) + jnp.where(lane_f == i2, w2, 0.0)
        gate_scr[...] = gate
        g_hi = gate.astype(BF16)
        ghi_scr[...] = g_hi
        glo_scr[...] = (gate - g_hi.astype(F32)).astype(BF16)
        acc_scr[...] = jnp.zeros_like(acc_scr)

        onehot = jnp.where(lane_f == gi, 1.0, 0.0).astype(BF16)
        r = lax.broadcasted_iota(jnp.int32, (MOE_SUB, MOE_SUB), 0)
        c = lax.broadcasted_iota(jnp.int32, (MOE_SUB, MOE_SUB), 1)
        earlier = jnp.where(c < r, 1.0, 0.0).astype(BF16)
        fullest = jnp.zeros((1, LANES), F32)
        gi_b = jnp.broadcast_to(gi, (tm, LANES))
        for s in range(n_sub):
            rs = slice(s * MOE_SUB, (s + 1) * MOE_SUB)
            oh = onehot[rs, :]
            ahead = _dot(earlier, oh)
            slot = jnp.sum(jnp.where(oh > 0, ahead, 0.0), axis=-1, keepdims=True)
            sel_b = jnp.where(slot < LANES, slot + float(LANES) * gi_b[rs, :], -1.0)
            sel_scr[rs, :] = sel_b
            selt_scr[:, rs] = jnp.transpose(sel_b)[:8, :]
            fullest = jnp.maximum(fullest, jnp.sum(oh.astype(F32), axis=0, keepdims=True))
        lane1 = lax.broadcasted_iota(jnp.int32, (1, LANES), 1)
        for gidx in range(MOE_GROUPS):
            most = jnp.max(jnp.where(lane1 == gidx, fullest, 0.0))
            ovf_smem[gidx] = (most > MOE_SLOTS).astype(jnp.int32)

    def experts(rows, gates):
        lane_r = lax.broadcasted_iota(jnp.int32, gates.shape, 1)
        y = None
        for j in range(EXP_PER_GROUP):
            hg = _dot(rows, wg_ref[j])
            hu = _dot(rows, wu_ref[j])
            here = lane_r == ROUTER_EXP_LANE0 + EXP_PER_GROUP * grp + j
            gcol = jnp.sum(jnp.where(here, gates, 0.0), axis=-1, keepdims=True)
            hid = ((hg * jax.nn.sigmoid(hg)) * hu * gcol).astype(BF16)
            yj = _dot(hid, wd_ref[j])
            y = yj if y is None else y + yj
        return y

    key = (grp * LANES).astype(F32)

    @pl.when(ovf_smem[grp] == 0)
    def _compact():
        slot_rows = lax.broadcasted_iota(jnp.int32, (MOE_SLOTS, MOE_SUB), 0).astype(F32) + key
        slot_cols = lax.broadcasted_iota(jnp.int32, (MOE_SUB, LANES), 1).astype(F32) + key
        xs, gs = [], []
        for s in range(n_sub):
            rs = slice(s * MOE_SUB, (s + 1) * MOE_SUB)
            pick = jnp.where(selt_scr[0:1, rs] == slot_rows, 1.0, 0.0).astype(BF16)
            xs.append(_dot(pick, xn_scr[rs, :]).astype(BF16))
            gs.append(_dot(pick, ghi_scr[rs, :]) + _dot(pick, glo_scr[rs, :]))
        y = experts(jnp.concatenate(xs, axis=0), jnp.concatenate(gs, axis=0))
        y = jnp.concatenate([y, jnp.zeros((LANES - MOE_SLOTS, d), F32)], axis=0)
        y_hi = y.astype(BF16)
        y_lo = (y - y_hi.astype(F32)).astype(BF16)
        for s in range(n_sub):
            rs = slice(s * MOE_SUB, (s + 1) * MOE_SUB)
            ys = slice(s * MOE_SLOTS, s * MOE_SLOTS + LANES)
            place = jnp.where(sel_scr[rs, :] == slot_cols, 1.0, 0.0).astype(BF16)
            acc_scr[rs, :] += _dot(jnp.concatenate([place, place], axis=1),
                                   jnp.concatenate([y_hi[ys, :], y_lo[ys, :]], axis=0))

    @pl.when(ovf_smem[grp] != 0)
    def _dense():
        for s in range(n_sub):
            rs = slice(s * MOE_SUB, (s + 1) * MOE_SUB)
            acc_scr[rs, :] += experts(xn_scr[rs, :], gate_scr[rs, :])

    @pl.when(grp == MOE_GROUPS - 1)
    def _finish():
        for s in range(n_sub):
            rs = slice(s * MOE_SUB, (s + 1) * MOE_SUB)
            x = x_ref[rs, :] + acc_scr[rs, :]
            gate = jax.nn.sigmoid(_dot(_rms(x, gp_ref[...]).astype(BF16), wpg_ref[...]))
            y = x + _dot(p_ref[rs, :].astype(BF16), wpi_ref[...]) * gate
            if final:
                y = _rms(y, gf_ref[...])
            out_ref[rs, :] = y


def _moe_ple(x2d, g, w_router, b_router, w_exp_gate, w_exp_up, w_exp_down,
             p2d, layer, g_ple, w_ple_gate, w_ple_in, g_final, final):
    t, d = x2d.shape
    f = w_exp_gate.shape[-1]
    tm = MOE_TILE
    row = pl.BlockSpec((tm, d), lambda i, e: (i, 0))
    group_w = lambda a, b: pl.BlockSpec((EXP_PER_GROUP, a, b), lambda i, e: (e, 0, 0))
    return pl.pallas_call(
        functools.partial(_moe_kernel, final),
        grid=(t // tm, MOE_GROUPS),
        in_specs=[row, _const_spec((1, d)), _const_spec(w_router.shape),
                  _const_spec(b_router.shape),
                  group_w(d, f), group_w(d, f), group_w(f, d),
                  pl.BlockSpec((tm, p2d.shape[1]), lambda i, e: (i, 0)),
                  _const_spec((1, d)), _layer_weight_spec(w_ple_gate, layer),
                  _layer_weight_spec(w_ple_in, layer), _const_spec((1, d))],
        out_specs=row,
        out_shape=jax.ShapeDtypeStruct((t, d), F32),
        scratch_shapes=[pltpu.VMEM((tm, d), BF16), pltpu.VMEM((tm, LANES), F32),
                        pltpu.VMEM((tm, LANES), BF16), pltpu.VMEM((tm, LANES), BF16),
                        pltpu.VMEM((tm, LANES), F32), pltpu.VMEM((8, tm), F32),
                        pltpu.VMEM((tm, d), F32), pltpu.SMEM((MOE_GROUPS,), jnp.int32),
                        pltpu.VMEM(w_ple_gate.shape[1:], BF16),
                        pltpu.VMEM(w_ple_in.shape[1:], BF16)],
        compiler_params=_cparams(2),
        name="moe_ple",
    )(x2d, g, w_router, b_router, w_exp_gate, w_exp_up, w_exp_down,
      p2d, g_ple, w_ple_gate, w_ple_in, g_final)


def _router_weights(w_grp, b_grp, w_exp, b_exp):
    d = w_grp.shape[0]
    w = jnp.concatenate([w_grp, w_exp.transpose(1, 0, 2).reshape(d, N_EXPERTS)], axis=1)
    b = jnp.concatenate([b_grp, b_exp.reshape(N_EXPERTS)])
    pad = LANES - w.shape[1]
    w = jnp.pad(w, ((0, 0), (0, pad)))
    w_hi = w.astype(BF16)
    w_lo = (w - w_hi.astype(F32)).astype(BF16)
    return jnp.concatenate([w_hi, w_lo], axis=1), jnp.pad(b, (0, pad))[None, :]


def kernel(x, p, positions, norm_mix, w_in, w_gate, b_gate, w_pool, pool_scale, w_up_a, w_up_b, w_up_c, w_out, norm_moe, w_router_grp, b_router_grp, w_router_exp, b_router_exp, w_exp_gate, w_exp_up, w_exp_down, norm_ple, w_ple_in, w_ple_gate, norm_final):
    batch, seq, d = x.shape
    depth = w_in.shape[0]
    t = batch * seq
    x2d = x.reshape(t, d)
    cos_t, sin_t = _rope_tables(positions)
    bf = lambda a: a.astype(BF16)
    for i in range(depth):
        qa, ka, va, qb, kb, vb, uc = _inproj(x2d, norm_mix[i][None, :], w_in, i, cos_t, sin_t)
        dil_outs = [_dilated_group(qa, ka, va, g, batch, seq) for g in range(len(DIL_CONFIGS))]
        sb_out = _stick_breaking(qb, kb, vb, batch, seq)
        x2d = _merge(x2d, dil_outs, sb_out, uc, seq, i, norm_mix[i][None, :], w_gate,
                     b_gate[i][None, :], w_pool, pool_scale[i][None, :],
                     w_up_a, w_up_b, w_up_c, w_out)
        w_r, b_r = _router_weights(w_router_grp[i], b_router_grp[i],
                                   w_router_exp[i], b_router_exp[i])
        x2d = _moe_ple(x2d, norm_moe[i][None, :], w_r, b_r, bf(w_exp_gate[i]),
                       bf(w_exp_up[i]), bf(w_exp_down[i]), p[i].reshape(t, -1), i,
                       norm_ple[i][None, :], w_ple_gate, w_ple_in,
                       norm_final[None, :], i == depth - 1)
    return x2d.reshape(batch, seq, d)
```

```python
import functools

import jax
import jax.numpy as jnp
from jax import lax
from jax.experimental import pallas as pl
from jax.experimental.pallas import tpu as pltpu

F32 = jnp.float32
BF16 = jnp.bfloat16

HEAD_DIM = 64
ROPE_THETA = 10000.0
DIL_CONFIGS = ((128, 1), (512, 4), (2048, 16))
DIL_HEADS = 4
DIL_OUT = DIL_HEADS * HEAD_DIM
DIL_WIDTH = len(DIL_CONFIGS) * DIL_OUT
DIL_BLOCK = 128
DIL_CHUNK = 2048
DIL_SKEW = 3
INPROJ_TILE = 512
MERGE_TILE = 512
DIL_PERMUTED = (16,)
SB_HEADS = 8
SB_WIDTH = SB_HEADS * HEAD_DIM
SB_BLOCK = 128
POOL_WINDOWS = (2, 4, 8, 16)
POOL_GROUP = 128
POOL_WIDTH = len(POOL_WINDOWS) * POOL_GROUP
POOL_HALO = 16
N_BRANCH = 3
MOE_GROUPS = 4
EXP_PER_GROUP = 4
N_EXPERTS = MOE_GROUPS * EXP_PER_GROUP
EPS = 1e-6

LANES = 128
NEG_BIG = -1e30
SB_DEAD_LOG = -105.0
SB_WINDOW = 3
SB_GROUP = 28
SB_SKEW = 8
SB_FAR_ROWS = 64
VMEM_LIMIT = 56 * 1024 * 1024


def _cparams(n_axes):
    return pltpu.CompilerParams(
        dimension_semantics=("arbitrary",) * n_axes, vmem_limit_bytes=VMEM_LIMIT)


def _rms(x, g):
    ms = jnp.mean(x * x, axis=-1, keepdims=True)
    return x * lax.rsqrt(ms + EPS) * g


def _dot(a, b):
    return jnp.dot(a, b, preferred_element_type=F32)


def _dot_nt(a, b):
    return lax.dot_general(a, b, (((1,), (1,)), ((), ())), preferred_element_type=F32)


def _const_spec(shape):
    zeros = (0,) * len(shape)
    return pl.BlockSpec(shape, lambda *_: zeros)


def _layer_weight_spec(w, layer):
    tail = (0,) * (w.ndim - 1)
    return pl.BlockSpec((None,) + w.shape[1:], lambda *_: (layer,) + tail,
                        pipeline_mode=pl.Buffered(1))


def _cast_weights_once(first_step, pairs):
    @pl.when(first_step)
    def _cast():
        for src, dst in pairs:
            if len(src.shape) == 3:
                for k in range(src.shape[0]):
                    dst[k] = src[k].astype(BF16)
            else:
                step = 2 * LANES
                for c in range(0, src.shape[1], step):
                    dst[:, c:c + step] = src[:, c:c + step].astype(BF16)


def _rope_table_kernel(pos_ref, inv_ref, sgn_ref, cos_ref, sin_ref):
    ang = pos_ref[...].astype(F32) * inv_ref[...]
    cos_ref[...] = jnp.cos(ang)
    sin_ref[...] = jnp.sin(ang) * sgn_ref[...]


def _rope_tables(positions):
    t = positions.size
    half = HEAD_DIM // 2
    inv = ROPE_THETA ** (-jnp.arange(half, dtype=F32) / half)
    inv_row = jnp.tile(inv, LANES // half)[None, :]
    sgn_row = jnp.tile(jnp.concatenate([-jnp.ones(half, F32), jnp.ones(half, F32)]),
                       LANES // HEAD_DIM)[None, :]
    tm = 2048
    return pl.pallas_call(
        _rope_table_kernel,
        grid=(t // tm,),
        in_specs=[pl.BlockSpec((tm, 1), lambda i: (i, 0)),
                  _const_spec((1, LANES)), _const_spec((1, LANES))],
        out_specs=[pl.BlockSpec((tm, LANES), lambda i: (i, 0))] * 2,
        out_shape=[jax.ShapeDtypeStruct((t, LANES), F32)] * 2,
        compiler_params=_cparams(1),
        name="rope_table",
    )(positions.reshape(t, 1), inv_row, sgn_row)


def _inproj_kernel(x_ref, g_ref, w32_ref, cos_ref, sin_ref,
                   qa_ref, ka_ref, va_ref, qb_ref, kb_ref, vb_ref, uc_ref,
                   h_scr, perm_scr, w_ref):
    _cast_weights_once(pl.program_id(0) == 0, [(w32_ref, w_ref)])
    h_scr[...] = _rms(x_ref[...], g_ref[...]).astype(BF16)
    cos = cos_ref[...]
    sin = sin_ref[...]
    tm = cos.shape[0]
    lane = lax.broadcasted_iota(jnp.int32, (tm, LANES), 1)
    first_half = (lane & (HEAD_DIM // 2)) == 0
    scale = HEAD_DIM ** -0.5

    def rope(t):
        partner = jnp.where(first_half, pltpu.roll(t, LANES - HEAD_DIM // 2, 1),
                            pltpu.roll(t, HEAD_DIM // 2, 1))
        return t * cos + partner * sin

    def store_dilated(ref, which, c, val):
        sl = slice(c * LANES, (c + 1) * LANES)
        dil = DIL_CONFIGS[c // (DIL_OUT // LANES)][1]
        if dil not in DIL_PERMUTED:
            ref[:, sl] = val
            return
        per = tm // dil
        perm_scr[which, c % (DIL_OUT // LANES)] = val
        for r in range(dil):
            ref[r * per:(r + 1) * per, sl] = perm_scr[which, c % (DIL_OUT // LANES),
                                                      pl.ds(r, per, stride=dil), :]

    h = h_scr[...]
    col = 0
    pq = _dot(h, w_ref[:, col:col + DIL_WIDTH])
    for c in range(DIL_WIDTH // LANES):
        store_dilated(qa_ref, 0, c, rope(pq[:, c * LANES:(c + 1) * LANES]) * scale)
    col += DIL_WIDTH
    pk = _dot(h, w_ref[:, col:col + DIL_WIDTH])
    for c in range(DIL_WIDTH // LANES):
        store_dilated(ka_ref, 1, c, rope(pk[:, c * LANES:(c + 1) * LANES]))
    col += DIL_WIDTH
    pv = _dot(h, w_ref[:, col:col + DIL_WIDTH])
    for c in range(DIL_WIDTH // LANES):
        store_dilated(va_ref, 2, c, pv[:, c * LANES:(c + 1) * LANES])
    col += DIL_WIDTH
    qb_ref[...] = (_dot(h, w_ref[:, col:col + SB_WIDTH]) * scale).astype(BF16)
    col += SB_WIDTH
    kb_ref[...] = _dot(h, w_ref[:, col:col + SB_WIDTH]).astype(BF16)
    col += SB_WIDTH
    vb_ref[...] = _dot(h, w_ref[:, col:col + SB_WIDTH]).astype(BF16)
    col += SB_WIDTH
    uc_ref[...] = _dot(h, w_ref[:, col:col + POOL_WIDTH])


def _inproj(x2d, g, w_in, layer, cos_t, sin_t):
    t, d = x2d.shape
    tm = INPROJ_TILE
    row = lambda width: pl.BlockSpec((tm, width), lambda i: (i, 0))
    widths = (DIL_WIDTH,) * 3 + (SB_WIDTH,) * 3 + (POOL_WIDTH,)
    dtypes = (F32,) * 3 + (BF16,) * 3 + (F32,)
    return pl.pallas_call(
        _inproj_kernel,
        grid=(t // tm,),
        in_specs=[row(d), _const_spec((1, d)), _layer_weight_spec(w_in, layer),
                  row(LANES), row(LANES)],
        out_specs=[row(w) for w in widths],
        out_shape=[jax.ShapeDtypeStruct((t, w), dt) for w, dt in zip(widths, dtypes)],
        scratch_shapes=[pltpu.VMEM((tm, d), BF16),
                        pltpu.VMEM((3, DIL_OUT // LANES, tm, LANES), F32),
                        pltpu.VMEM(w_in.shape[1:], BF16)],
        compiler_params=_cparams(1),
        name="in_proj",
    )(x2d, g, w_in, cos_t, sin_t)


def _dil_kernel(dil, q_ref, kp_ref, kc_ref, vp_ref, vc_ref, o_ref, lse_ref):
    has_prev = pl.program_id(1) > 0
    n = DIL_BLOCK
    n_blocks = q_ref.shape[0] // (n * dil)
    qi = lax.broadcasted_iota(jnp.int32, (n, n), 0)
    kj = lax.broadcasted_iota(jnp.int32, (n, n), 1)
    mask_prev = kj >= qi
    mask_prev_first = jnp.logical_and(mask_prev, has_prev)
    mask_cur = kj <= qi
    lo_head = kj < HEAD_DIM
    ones = jnp.ones((2 * n, LANES), BF16)

    def rows(ref, r, j):
        if dil in DIL_PERMUTED:
            per = INPROJ_TILE // dil
            pieces = [ref[ti * INPROJ_TILE + r * per:ti * INPROJ_TILE + (r + 1) * per, :]
                      for ti in range(j * (n // per), (j + 1) * (n // per))]
            return jnp.concatenate(pieces, axis=0).astype(BF16)
        start = j * n * dil + r
        return ref[pl.ds(start, n, stride=dil) if dil > 1 else pl.ds(start, n), :].astype(BF16)

    units = {}

    def unit(r, j):
        if (r, j) not in units:
            kc, vc = rows(kc_ref, r, j), rows(vc_ref, r, j)
            if j == 0:
                kp, vp = rows(kp_ref, r, 0), rows(vp_ref, r, 0)
            else:
                kp, vp = units[(r, j - 1)]["kc"], units[(r, j - 1)]["vc"]
            vo = jnp.concatenate([jnp.concatenate([vp, vc], axis=0), ones], axis=1)
            units[(r, j)] = dict(q=rows(q_ref, r, j), kc=kc, vc=vc, kp=kp, vo=vo,
                                 pm=mask_prev_first if j == 0 else mask_prev,
                                 start=j * n * dil + r)
        return units[(r, j)]

    def stage_scores(c):
        u = unit(c["r"], c["j"])
        sel = lo_head if c["hh"] == 0 else jnp.logical_not(lo_head)
        qm = jnp.where(sel, u["q"], jnp.zeros((n, LANES), BF16))
        c["sp"] = _dot_nt(qm, u["kp"])
        c["sc"] = _dot_nt(qm, u["kc"])

    def stage_max(c):
        u = unit(c["r"], c["j"])
        c["sp"] = jnp.where(u["pm"], c["sp"], NEG_BIG)
        c["sc"] = jnp.where(mask_cur, c["sc"], NEG_BIG)
        c["m"] = jnp.max(jnp.maximum(c["sp"], c["sc"]), axis=-1, keepdims=True)

    def stage_pv(c):
        u = unit(c["r"], c["j"])
        ep = jnp.exp(c.pop("sp") - c["m"]).astype(BF16)
        ec = jnp.exp(c.pop("sc") - c["m"]).astype(BF16)
        res = _dot(jnp.concatenate([ep, ec], axis=1), u["vo"])
        c["acc"] = res[:, :LANES]
        c["den"] = res[:, LANES:]

    def stage_out(c, pairs):
        c["o"] = c.pop("acc") / c["den"]
        c["lse"] = c["m"] + jnp.log(c.pop("den"))
        key = (c["r"], c["j"])
        pairs.setdefault(key, []).append(c)
        if len(pairs[key]) == 2:
            c0, c1 = pairs.pop(key)
            u = unit(c["r"], c["j"])
            idx = pl.ds(u["start"], n, stride=dil) if dil > 1 else pl.ds(u["start"], n)
            o_ref[idx, :] = jnp.where(lo_head, c0["o"], c1["o"])
            lse_ref[idx, :] = jnp.where(lo_head, c0["lse"], c1["lse"])

    chains = [dict(r=r, j=j, hh=hh) for r in range(dil) for j in range(n_blocks)
              for hh in range(2)]
    stages = (stage_scores, stage_max, stage_pv)
    pairs = {}
    total = len(chains)
    for step in range(total + len(stages) * DIL_SKEW):
        for s, fn in enumerate(stages):
            i = step - s * DIL_SKEW
            if 0 <= i < total:
                fn(chains[i])
        i = step - len(stages) * DIL_SKEW
        if 0 <= i < total:
            stage_out(chains[i], pairs)


def _dilated_group(qa, ka, va, g, batch, seq):
    window, dil = DIL_CONFIGS[g]
    assert window // dil == DIL_BLOCK and seq % DIL_CHUNK == 0
    nc = seq // DIL_CHUNK
    hist = DIL_BLOCK * dil
    t = batch * seq
    n_pairs = DIL_OUT // LANES
    cur = pl.BlockSpec((DIL_CHUNK, LANES), lambda b, c, pr: (b * nc + c, g * n_pairs + pr))
    prev = pl.BlockSpec(
        (hist, LANES),
        lambda b, c, pr: (jnp.maximum((b * nc + c) * (DIL_CHUNK // hist) - 1, 0),
                          g * n_pairs + pr))
    out = pl.BlockSpec((DIL_CHUNK, LANES), lambda b, c, pr: (b * nc + c, pr))
    return pl.pallas_call(
        functools.partial(_dil_kernel, dil),
        grid=(batch, nc, n_pairs),
        in_specs=[cur, prev, cur, prev, cur],
        out_specs=[out, out],
        out_shape=[jax.ShapeDtypeStruct((t, DIL_OUT), F32)] * 2,
        compiler_params=_cparams(3),
        name=f"dilated_d{dil}",
    )(qa, ka, ka, va, va)


def _sb_kernel(q_ref, k_ref, v_ref, o_ref, acc_scr, r_scr):
    blk = SB_BLOCK
    nq = q_ref.shape[0] // blk
    row = lax.broadcasted_iota(jnp.int32, (blk, blk), 0)
    col = lax.broadcasted_iota(jnp.int32, (blk, blk), 1)
    tri = col < row
    lo_head = col < HEAD_DIM
    cum_mat = jnp.where(row > col, 1.0, 0.0).astype(BF16)

    def head_q(q, hh):
        sel = lo_head if hh == 0 else jnp.logical_not(lo_head)
        return jnp.where(sel, q, jnp.zeros_like(q))

    def tile_scores(z, mask):
        sp = jnp.maximum(z, 0.0) + jnp.log(1.0 + jnp.exp(-jnp.abs(z)))
        lom = -sp
        if mask is not None:
            lom = jnp.where(mask, lom, 0.0)
        hi = lom.astype(BF16)
        lo = (lom - hi.astype(F32)).astype(BF16)
        after = _dot(hi, cum_mat) + _dot(lo, cum_mat)
        return z - sp + after, jnp.sum(lom, axis=-1, keepdims=True)

    def tile_weights(logit, mask, r_acc):
        a = jnp.exp(logit + r_acc)
        if mask is not None:
            a = jnp.where(mask, a, 0.0)
        return a.astype(BF16)

    def q_group(blocks):
        chains = []
        for qi, n_tiles in blocks:
            qoff = qi * blk
            koff = (qi - (n_tiles - 1)) * blk
            if not isinstance(qi, int):
                qoff, koff = pl.multiple_of(qoff, blk), pl.multiple_of(koff, blk)
            q = q_ref[pl.ds(qoff, blk), :]
            kw = k_ref[pl.ds(koff, n_tiles * blk), :]
            vw = v_ref[pl.ds(koff, n_tiles * blk), :]
            for hh in range(2):
                chains.append(dict(z=_dot_nt(head_q(q, hh), kw), vw=vw, n=n_tiles,
                                   r=jnp.zeros((blk, 1), F32), acc=jnp.zeros((blk, blk), F32)))
        max_tiles = max(c["n"] for c in chains)
        tiles = [(c, c["n"] - 1 - j) for j in range(max_tiles) for c in chains if j < c["n"]]
        pending = []
        for step in range(len(tiles) + SB_SKEW):
            if step < len(tiles):
                c, w = tiles[step]
                mask = tri if w == c["n"] - 1 else None
                rows = SB_FAR_ROWS if (w == 0 and c["n"] == SB_WINDOW) else blk
                logit, rowsum = tile_scores(c["z"][:rows, w * blk:(w + 1) * blk], mask)
                pending.append((c, w, mask, logit, rowsum, rows))
            if step >= SB_SKEW:
                c, w, mask, logit, rowsum, rows = pending[step - SB_SKEW]
                a = tile_weights(logit, mask, c["r"][:rows])
                pv = _dot(a, c["vw"][w * blk:(w + 1) * blk, :])
                if rows == blk:
                    c["acc"] = c["acc"] + pv
                    c["r"] = c["r"] + rowsum
                else:
                    c["acc"] = jnp.concatenate([c["acc"][:rows] + pv, c["acc"][rows:]], axis=0)
                    c["r"] = jnp.concatenate([c["r"][:rows] + rowsum, c["r"][rows:]], axis=0)
        for b, (qi, _) in enumerate(blocks):
            qoff = qi * blk if isinstance(qi, int) else pl.multiple_of(qi * blk, blk)
            c0, c1 = chains[2 * b], chains[2 * b + 1]
            r_scr[0, pl.ds(qoff, blk), :] = jnp.broadcast_to(c0["r"], (blk, blk))
            r_scr[1, pl.ds(qoff, blk), :] = jnp.broadcast_to(c1["r"], (blk, blk))
            acc_pair = jnp.where(lo_head, c0["acc"], c1["acc"])
            acc_scr[pl.ds(qoff, blk), :] = acc_pair
            o_ref[pl.ds(qoff, blk), :] = acc_pair.astype(BF16)

    first_tail = min(SB_WINDOW - 1, nq)
    n_static = SB_WINDOW + (nq - SB_WINDOW) % SB_GROUP
    q_group([(qi, min(qi + 1, SB_WINDOW)) for qi in range(n_static)])

    def q_loop(it, carry):
        q0 = n_static + it * SB_GROUP
        q_group([(q0 + j, SB_WINDOW) for j in range(SB_GROUP)])
        return carry

    lax.fori_loop(0, (nq - n_static) // SB_GROUP, q_loop, 0)


    @pl.when(jnp.max(r_scr[:, first_tail * blk:, :]) > SB_DEAD_LOG)
    def _tail():
        def q_tail(qi, carry):
            qoff = pl.multiple_of(qi * blk, blk)
            q = q_ref[pl.ds(qoff, blk), :]
            acc0 = acc_scr[pl.ds(qoff, blk), :]
            res = []
            for hh in range(2):
                qm = head_q(q, hh)
                r0 = r_scr[hh, pl.ds(qoff, blk), :]

                def cond(c):
                    return jnp.logical_and(c[0] >= 0, c[3] > SB_DEAD_LOG)

                def body(c, qm=qm):
                    off = pl.multiple_of(c[0] * blk, blk)
                    z = _dot_nt(qm, k_ref[pl.ds(off, blk), :])
                    todo = jnp.logical_or(row >= SB_FAR_ROWS, c[0] != qi - (SB_WINDOW - 1))
                    logit, rowsum = tile_scores(z, todo)
                    a = tile_weights(logit, todo, c[1])
                    r_new = c[1] + rowsum
                    acc_new = c[2] + _dot(a, v_ref[pl.ds(off, blk), :])
                    return c[0] - 1, r_new, acc_new, jnp.max(r_new)

                c0 = (qi - (SB_WINDOW - 1), r0, acc0, jnp.max(r0))
                res.append(lax.while_loop(cond, body, c0)[2])
            o_ref[pl.ds(qoff, blk), :] = jnp.where(lo_head, res[0], res[1]).astype(BF16)
            return carry

        lax.fori_loop(first_tail, nq, q_tail, 0)


def _stick_breaking(qb, kb, vb, batch, seq):
    view = lambda a: a.reshape(batch, seq, SB_WIDTH)
    spec = pl.BlockSpec((None, seq, LANES), lambda b, hp: (b, 0, hp))
    o = pl.pallas_call(
        _sb_kernel,
        grid=(batch, SB_WIDTH // LANES),
        in_specs=[spec, spec, spec],
        out_specs=spec,
        out_shape=jax.ShapeDtypeStruct((batch, seq, SB_WIDTH), BF16),
        scratch_shapes=[pltpu.VMEM((seq, LANES), F32), pltpu.VMEM((2, seq, LANES), F32)],
        compiler_params=_cparams(2),
        name="stick_breaking",
    )(view(qb), view(kb), view(vb))
    return o.reshape(batch * seq, SB_WIDTH)


def _merge_kernel(seq, x_ref, o1_ref, o2_ref, o3_ref, l1_ref, l2_ref, l3_ref, sb_ref,
                  uc_ref, uh_ref, g_ref, wg32_ref, bg_ref, wp32_ref, ps_ref,
                  wa32_ref, wb32_ref, wc32_ref, wo32_ref, out_ref,
                  h_scr, ub_scr, yc_scr, m_scr, wg_ref, wp_ref, wa_ref, wb_ref, wc_ref, wo_ref):
    i = pl.program_id(0)
    _cast_weights_once(i == 0, [(wg32_ref, wg_ref), (wp32_ref, wp_ref), (wa32_ref, wa_ref),
                                (wb32_ref, wb_ref), (wc32_ref, wc_ref), (wo32_ref, wo_ref)])
    tm, d = x_ref.shape
    x = x_ref[...]
    h_scr[...] = _rms(x, g_ref[...]).astype(BF16)

    l1, l2, l3 = l1_ref[...], l2_ref[...], l3_ref[...]
    m = jnp.maximum(jnp.maximum(l1, l2), l3)
    e1, e2, e3 = jnp.exp(l1 - m), jnp.exp(l2 - m), jnp.exp(l3 - m)
    dil = ((e1 * o1_ref[...] + e2 * o2_ref[...] + e3 * o3_ref[...]) / (e1 + e2 + e3)).astype(BF16)

    row0 = (i * tm) % seq
    halo = jnp.where(row0 == 0, 0.0, uh_ref[...])
    ub_scr[0:POOL_HALO, :] = halo
    ub_scr[POOL_HALO:POOL_HALO + tm, :] = uc_ref[...]
    t_in_seq = row0 + lax.broadcasted_iota(jnp.int32, (tm, 1), 0)
    for g, w in enumerate(POOL_WINDOWS):
        sl = slice(g * POOL_GROUP, (g + 1) * POOL_GROUP)
        u = ub_scr[POOL_HALO:POOL_HALO + tm, sl]
        acc = u
        for j in range(1, w):
            acc = acc + ub_scr[POOL_HALO - j:POOL_HALO - j + tm, sl]
        cnt = jnp.minimum(t_in_seq + 1, w).astype(F32)
        pooled = acc / cnt - u
        y = _dot(pooled.astype(BF16), wp_ref[g]) * ps_ref[:, sl]
        yc_scr[:, sl] = y.astype(BF16)

    h = h_scr[...]
    sb = sb_ref[...]
    yc_in = yc_scr[...]
    cw = 256
    for c in range(d // cw):
        sl = slice(c * cw, (c + 1) * cw)
        gates = []
        for br in range(N_BRANCH):
            gsl = slice(br * d + c * cw, br * d + (c + 1) * cw)
            gates.append(jax.nn.sigmoid(_dot(h, wg_ref[:, gsl]) + bg_ref[:, gsl]))
        merged = (gates[0] * _dot(dil, wa_ref[:, sl]) + gates[1] * _dot(sb, wb_ref[:, sl])
                  + gates[2] * _dot(yc_in, wc_ref[:, sl]))
        m_scr[:, sl] = merged.astype(BF16)
    out_ref[...] = x + _dot(m_scr[...], wo_ref[...])


def _merge(x2d, dil_outs, sb_out, uc, seq, layer, g, w_gate, b_gate, w_pool, pool_scale,
           w_up_a, w_up_b, w_up_c, w_out):
    t, d = x2d.shape
    tm = MERGE_TILE
    row = lambda width: pl.BlockSpec((tm, width), lambda i: (i, 0))
    halo = pl.BlockSpec((POOL_HALO, POOL_WIDTH),
                        lambda i: (jnp.maximum(i * (tm // POOL_HALO) - 1, 0), 0))
    (o1, l1), (o2, l2), (o3, l3) = dil_outs
    big = (w_gate, w_pool, w_up_a, w_up_b, w_up_c, w_out)
    lw = lambda w: _layer_weight_spec(w, layer)
    return pl.pallas_call(
        functools.partial(_merge_kernel, seq),
        grid=(t // tm,),
        in_specs=[row(d)] + [row(DIL_OUT)] * 6 + [row(SB_WIDTH), row(POOL_WIDTH), halo]
                 + [_const_spec(g.shape), lw(w_gate), _const_spec(b_gate.shape), lw(w_pool),
                    _const_spec(pool_scale.shape), lw(w_up_a), lw(w_up_b), lw(w_up_c),
                    lw(w_out)],
        out_specs=row(d),
        out_shape=jax.ShapeDtypeStruct((t, d), F32),
        scratch_shapes=[pltpu.VMEM((tm, d), BF16),
                        pltpu.VMEM((POOL_HALO + tm, POOL_WIDTH), F32),
                        pltpu.VMEM((tm, POOL_WIDTH), BF16),
                        pltpu.VMEM((tm, d), BF16)]
                       + [pltpu.VMEM(w.shape[1:], BF16) for w in big],
        compiler_params=_cparams(1),
        name="merge",
    )(x2d, o1, o2, o3, l1, l2, l3, sb_out, uc, uc, g, w_gate, b_gate, w_pool, pool_scale,
      w_up_a, w_up_b, w_up_c, w_out)


ROUTER_EXP_LANE0 = MOE_GROUPS
MOE_TILE = 1024
MOE_SUB = 256
MOE_SLOTS = 80


def _moe_kernel(final, x_ref, g_ref, wr_ref, br_ref, wg_ref, wu_ref, wd_ref,
                p_ref, gp_ref, wpg32_ref, wpi32_ref, gf_ref, out_ref,
                xn_scr, gate_scr, ghi_scr, glo_scr, sel_scr, selt_scr, acc_scr, ovf_smem,
                wpg_ref, wpi_ref):
    grp = pl.program_id(1)
    _cast_weights_once(jnp.logical_and(pl.program_id(0) == 0, grp == 0),
                       [(wpg32_ref, wpg_ref), (wpi32_ref, wpi_ref)])
    tm, d = x_ref.shape
    n_sub = tm // MOE_SUB
    lane_f = lax.broadcasted_iota(jnp.int32, (tm, LANES), 1).astype(F32)

    @pl.when(grp == 0)
    def _route():
        xn = _rms(x_ref[...], g_ref[...])
        a_hi = xn.astype(BF16)
        xn_scr[...] = a_hi
        a_lo = (xn - a_hi.astype(F32)).astype(BF16)
        r_hi = _dot(a_hi, wr_ref[...])
        logits = (r_hi[:, :LANES] + r_hi[:, LANES:] + _dot(a_lo, wr_ref[:, :LANES])
                  + br_ref[...])
        is_grp = lane_f < MOE_GROUPS
        lg = jnp.where(is_grp, logits, NEG_BIG)
        mg = jnp.max(lg, axis=-1, keepdims=True)
        p1 = 1.0 / jnp.sum(jnp.exp(lg - mg), axis=-1, keepdims=True)
        gi = jnp.min(jnp.where(lg == mg, lane_f, float(LANES)), axis=-1, keepdims=True)
        lane_grp = jnp.floor((lane_f - ROUTER_EXP_LANE0) * (1.0 / EXP_PER_GROUP))
        in_grp = jnp.logical_and(
            jnp.logical_and(lane_f >= ROUTER_EXP_LANE0, lane_f < ROUTER_EXP_LANE0 + N_EXPERTS),
            lane_grp == gi)
        le = jnp.where(in_grp, logits, NEG_BIG)
        v1 = jnp.max(le, axis=-1, keepdims=True)
        i1 = jnp.min(jnp.where(le == v1, lane_f, float(LANES)), axis=-1, keepdims=True)
        le2 = jnp.where(lane_f == i1, NEG_BIG, le)
        v2 = jnp.max(le2, axis=-1, keepdims=True)
        i2 = jnp.min(jnp.where(le2 == v2, lane_f, float(LANES)), axis=-1, keepdims=True)
        t = jnp.exp(v2 - v1)
        w1 = p1 / (1.0 + t)
        w2 = p1 * t / (1.0 + t)
        gate = jnp.where(lane_f == i1, w1, 0.0) + jnp.where(lane_f == i2, w2, 0.0)
        gate_scr[...] = gate
        g_hi = gate.astype(BF16)
        ghi_scr[...] = g_hi
        glo_scr[...] = (gate - g_hi.astype(F32)).astype(BF16)
        acc_scr[...] = jnp.zeros_like(acc_scr)

        onehot = jnp.where(lane_f == gi, 1.0, 0.0).astype(BF16)
        r = lax.broadcasted_iota(jnp.int32, (MOE_SUB, MOE_SUB), 0)
        c = lax.broadcasted_iota(jnp.int32, (MOE_SUB, MOE_SUB), 1)
        earlier = jnp.where(c < r, 1.0, 0.0).astype(BF16)
        fullest = jnp.zeros((1, LANES), F32)
        gi_b = jnp.broadcast_to(gi, (tm, LANES))
        for s in range(n_sub):
            rs = slice(s * MOE_SUB, (s + 1) * MOE_SUB)
            oh = onehot[rs, :]
            ahead = _dot(earlier, oh)
            slot = jnp.sum(jnp.where(oh > 0, ahead, 0.0), axis=-1, keepdims=True)
            sel_b = jnp.where(slot < LANES, slot + float(LANES) * gi_b[rs, :], -1.0)
            sel_scr[rs, :] = sel_b
            selt_scr[:, rs] = jnp.transpose(sel_b)[:8, :]
            fullest = jnp.maximum(fullest, jnp.sum(oh.astype(F32), axis=0, keepdims=True))
        lane1 = lax.broadcasted_iota(jnp.int32, (1, LANES), 1)
        for gidx in range(MOE_GROUPS):
            most = jnp.max(jnp.where(lane1 == gidx, fullest, 0.0))
            ovf_smem[gidx] = (most > MOE_SLOTS).astype(jnp.int32)

    def experts(rows, gates):
        lane_r = lax.broadcasted_iota(jnp.int32, gates.shape, 1)
        y = None
        for j in range(EXP_PER_GROUP):
            hg = _dot(rows, wg_ref[j])
            hu = _dot(rows, wu_ref[j])
            here = lane_r == ROUTER_EXP_LANE0 + EXP_PER_GROUP * grp + j
            gcol = jnp.sum(jnp.where(here, gates, 0.0), axis=-1, keepdims=True)
            hid = ((hg * jax.nn.sigmoid(hg)) * hu * gcol).astype(BF16)
            yj = _dot(hid, wd_ref[j])
            y = yj if y is None else y + yj
        return y

    key = (grp * LANES).astype(F32)

    @pl.when(ovf_smem[grp] == 0)
    def _compact():
        slot_rows = lax.broadcasted_iota(jnp.int32, (MOE_SLOTS, MOE_SUB), 0).astype(F32) + key
        slot_cols = lax.broadcasted_iota(jnp.int32, (MOE_SUB, LANES), 1).astype(F32) + key
        xs, gs = [], []
        for s in range(n_sub):
            rs = slice(s * MOE_SUB, (s + 1) * MOE_SUB)
            pick = jnp.where(selt_scr[0:1, rs] == slot_rows, 1.0, 0.0).astype(BF16)
            xs.append(_dot(pick, xn_scr[rs, :]).astype(BF16))
            gs.append(_dot(pick, ghi_scr[rs, :]) + _dot(pick, glo_scr[rs, :]))
        y = experts(jnp.concatenate(xs, axis=0), jnp.concatenate(gs, axis=0))
        y = jnp.concatenate([y, jnp.zeros((LANES - MOE_SLOTS, d), F32)], axis=0)
        y_hi = y.astype(BF16)
        y_lo = (y - y_hi.astype(F32)).astype(BF16)
        for s in range(n_sub):
            rs = slice(s * MOE_SUB, (s + 1) * MOE_SUB)
            ys = slice(s * MOE_SLOTS, s * MOE_SLOTS + LANES)
            place = jnp.where(sel_scr[rs, :] == slot_cols, 1.0, 0.0).astype(BF16)
            acc_scr[rs, :] += _dot(jnp.concatenate([place, place], axis=1),
                                   jnp.concatenate([y_hi[ys, :], y_lo[ys, :]], axis=0))

    @pl.when(ovf_smem[grp] != 0)
    def _dense():
        for s in range(n_sub):
            rs = slice(s * MOE_SUB, (s + 1) * MOE_SUB)
            acc_scr[rs, :] += experts(xn_scr[rs, :], gate_scr[rs, :])

    @pl.when(grp == MOE_GROUPS - 1)
    def _finish():
        for s in range(n_sub):
            rs = slice(s * MOE_SUB, (s + 1) * MOE_SUB)
            x = x_ref[rs, :] + acc_scr[rs, :]
            gate = jax.nn.sigmoid(_dot(_rms(x, gp_ref[...]).astype(BF16), wpg_ref[...]))
            y = x + _dot(p_ref[rs, :].astype(BF16), wpi_ref[...]) * gate
            if final:
                y = _rms(y, gf_ref[...])
            out_ref[rs, :] = y


def _moe_ple(x2d, g, w_router, b_router, w_exp_gate, w_exp_up, w_exp_down,
             p2d, layer, g_ple, w_ple_gate, w_ple_in, g_final, final):
    t, d = x2d.shape
    f = w_exp_gate.shape[-1]
    tm = MOE_TILE
    row = pl.BlockSpec((tm, d), lambda i, e: (i, 0))
    group_w = lambda a, b: pl.BlockSpec((EXP_PER_GROUP, a, b), lambda i, e: (e, 0, 0))
    return pl.pallas_call(
        functools.partial(_moe_kernel, final),
        grid=(t // tm, MOE_GROUPS),
        in_specs=[row, _const_spec((1, d)), _const_spec(w_router.shape),
                  _const_spec(b_router.shape),
                  group_w(d, f), group_w(d, f), group_w(f, d),
                  pl.BlockSpec((tm, p2d.shape[1]), lambda i, e: (i, 0)),
                  _const_spec((1, d)), _layer_weight_spec(w_ple_gate, layer),
                  _layer_weight_spec(w_ple_in, layer), _const_spec((1, d))],
        out_specs=row,
        out_shape=jax.ShapeDtypeStruct((t, d), F32),
        scratch_shapes=[pltpu.VMEM((tm, d), BF16), pltpu.VMEM((tm, LANES), F32),
                        pltpu.VMEM((tm, LANES), BF16), pltpu.VMEM((tm, LANES), BF16),
                        pltpu.VMEM((tm, LANES), F32), pltpu.VMEM((8, tm), F32),
                        pltpu.VMEM((tm, d), F32), pltpu.SMEM((MOE_GROUPS,), jnp.int32),
                        pltpu.VMEM(w_ple_gate.shape[1:], BF16),
                        pltpu.VMEM(w_ple_in.shape[1:], BF16)],
        compiler_params=_cparams(2),
        name="moe_ple",
    )(x2d, g, w_router, b_router, w_exp_gate, w_exp_up, w_exp_down,
      p2d, g_ple, w_ple_gate, w_ple_in, g_final)


def _router_weights(w_grp, b_grp, w_exp, b_exp):
    d = w_grp.shape[0]
    w = jnp.concatenate([w_grp, w_exp.transpose(1, 0, 2).reshape(d, N_EXPERTS)], axis=1)
    b = jnp.concatenate([b_grp, b_exp.reshape(N_EXPERTS)])
    pad = LANES - w.shape[1]
    w = jnp.pad(w, ((0, 0), (0, pad)))
    w_hi = w.astype(BF16)
    w_lo = (w - w_hi.astype(F32)).astype(BF16)
    return jnp.concatenate([w_hi, w_lo], axis=1), jnp.pad(b, (0, pad))[None, :]


def kernel(x, p, positions, norm_mix, w_in, w_gate, b_gate, w_pool, pool_scale, w_up_a, w_up_b, w_up_c, w_out, norm_moe, w_router_grp, b_router_grp, w_router_exp, b_router_exp, w_exp_gate, w_exp_up, w_exp_down, norm_ple, w_ple_in, w_ple_gate, norm_final):
    batch, seq, d = x.shape
    depth = w_in.shape[0]
    t = batch * seq
    x2d = x.reshape(t, d)
    cos_t, sin_t = _rope_tables(positions)
    bf = lambda a: a.astype(BF16)
    for i in range(depth):
        qa, ka, va, qb, kb, vb, uc = _inproj(x2d, norm_mix[i][None, :], w_in, i, cos_t, sin_t)
        dil_outs = [_dilated_group(qa, ka, va, g, batch, seq) for g in range(len(DIL_CONFIGS))]
        sb_out = _stick_breaking(qb, kb, vb, batch, seq)
        x2d = _merge(x2d, dil_outs, sb_out, uc, seq, i, norm_mix[i][None, :], w_gate,
                     b_gate[i][None, :], w_pool, pool_scale[i][None, :],
                     w_up_a, w_up_b, w_up_c, w_out)
        w_r, b_r = _router_weights(w_router_grp[i], b_router_grp[i],
                                   w_router_exp[i], b_router_exp[i])
        x2d = _moe_ple(x2d, norm_moe[i][None, :], w_r, b_r, bf(w_exp_gate[i]),
                       bf(w_exp_up[i]), bf(w_exp_down[i]), p[i].reshape(t, -1), i,
                       norm_ple[i][None, :], w_ple_gate, w_ple_in,
                       norm_final[None, :], i == depth - 1)
    return x2d.reshape(batch, seq, d)
```

```python
import functools

import jax
import jax.numpy as jnp
from jax import lax
from jax.experimental import pallas as pl
from jax.experimental.pallas import tpu as pltpu

F32 = jnp.float32
BF16 = jnp.bfloat16

HEAD_DIM = 64
ROPE_THETA = 10000.0
DIL_CONFIGS = ((128, 1), (512, 4), (2048, 16))
DIL_HEADS = 4
DIL_OUT = DIL_HEADS * HEAD_DIM
DIL_WIDTH = len(DIL_CONFIGS) * DIL_OUT
DIL_BLOCK = 128
DIL_CHUNK = 2048
DIL_SKEW = 3
INPROJ_TILE = 512
MERGE_TILE = 512
DIL_PERMUTED = (16,)
SB_HEADS = 8
SB_WIDTH = SB_HEADS * HEAD_DIM
SB_BLOCK = 128
POOL_WINDOWS = (2, 4, 8, 16)
POOL_GROUP = 128
POOL_WIDTH = len(POOL_WINDOWS) * POOL_GROUP
POOL_HALO = 16
N_BRANCH = 3
MOE_GROUPS = 4
EXP_PER_GROUP = 4
N_EXPERTS = MOE_GROUPS * EXP_PER_GROUP
EPS = 1e-6

LANES = 128
NEG_BIG = -1e30
SB_DEAD_LOG = -105.0
SB_WINDOW = 3
SB_GROUP = 28
SB_SKEW = 8
SB_FAR_ROWS = 64
VMEM_LIMIT = 56 * 1024 * 1024


def _cparams(n_axes):
    return pltpu.CompilerParams(
        dimension_semantics=("arbitrary",) * n_axes, vmem_limit_bytes=VMEM_LIMIT)


def _rms(x, g):
    ms = jnp.mean(x * x, axis=-1, keepdims=True)
    return x * lax.rsqrt(ms + EPS) * g


def _dot(a, b):
    return jnp.dot(a, b, preferred_element_type=F32)


def _dot_nt(a, b):
    return lax.dot_general(a, b, (((1,), (1,)), ((), ())), preferred_element_type=F32)


def _const_spec(shape):
    zeros = (0,) * len(shape)
    return pl.BlockSpec(shape, lambda *_: zeros)


def _layer_weight_spec(w, layer):
    tail = (0,) * (w.ndim - 1)
    return pl.BlockSpec((None,) + w.shape[1:], lambda *_: (layer,) + tail,
                        pipeline_mode=pl.Buffered(1))


def _cast_weights_once(first_step, pairs):
    @pl.when(first_step)
    def _cast():
        for src, dst in pairs:
            if len(src.shape) == 3:
                for k in range(src.shape[0]):
                    dst[k] = src[k].astype(BF16)
            else:
                step = 2 * LANES
                for c in range(0, src.shape[1], step):
                    dst[:, c:c + step] = src[:, c:c + step].astype(BF16)


def _rope_table_kernel(pos_ref, inv_ref, sgn_ref, cos_ref, sin_ref):
    ang = inv_ref[...] * pos_ref[...].astype(F32)
    reps = LANES // ang.shape[0]
    cos_ref[...] = jnp.transpose(jnp.concatenate([jnp.cos(ang)] * reps, axis=0))
    sin_ref[...] = jnp.transpose(jnp.concatenate([jnp.sin(ang)] * reps, axis=0) * sgn_ref[...])


def _rope_tables(positions):
    t = positions.size
    half = HEAD_DIM // 2
    inv_col = (ROPE_THETA ** (-jnp.arange(half, dtype=F32) / half))[:, None]
    sgn_col = jnp.tile(jnp.concatenate([-jnp.ones(half, F32), jnp.ones(half, F32)]),
                       LANES // HEAD_DIM)[:, None]
    tm = 2048
    return pl.pallas_call(
        _rope_table_kernel,
        grid=(t // tm,),
        in_specs=[pl.BlockSpec((None, 1, tm), lambda i: (i, 0, 0)),
                  _const_spec((half, 1)), _const_spec((LANES, 1))],
        out_specs=[pl.BlockSpec((tm, LANES), lambda i: (i, 0))] * 2,
        out_shape=[jax.ShapeDtypeStruct((t, LANES), F32)] * 2,
        compiler_params=_cparams(1),
        name="rope_table",
    )(positions.reshape(t // tm, 1, tm), inv_col, sgn_col)


def _inproj_kernel(x_ref, g_ref, w32_ref, cos_ref, sin_ref,
                   qa_ref, ka_ref, va_ref, qb_ref, kb_ref, vb_ref, uc_ref,
                   h_scr, perm_scr, w_ref):
    _cast_weights_once(pl.program_id(0) == 0, [(w32_ref, w_ref)])
    h_scr[...] = _rms(x_ref[...], g_ref[...]).astype(BF16)
    cos = cos_ref[...]
    sin = sin_ref[...]
    tm = cos.shape[0]
    lane = lax.broadcasted_iota(jnp.int32, (tm, LANES), 1)
    first_half = (lane & (HEAD_DIM // 2)) == 0
    scale = HEAD_DIM ** -0.5

    def rope(t):
        partner = jnp.where(first_half, pltpu.roll(t, LANES - HEAD_DIM // 2, 1),
                            pltpu.roll(t, HEAD_DIM // 2, 1))
        return t * cos + partner * sin

    def store_dilated(ref, which, c, val):
        sl = slice(c * LANES, (c + 1) * LANES)
        dil = DIL_CONFIGS[c // (DIL_OUT // LANES)][1]
        if dil not in DIL_PERMUTED:
            ref[:, sl] = val
            return
        per = tm // dil
        perm_scr[which, c % (DIL_OUT // LANES)] = val
        for r in range(dil):
            ref[r * per:(r + 1) * per, sl] = perm_scr[which, c % (DIL_OUT // LANES),
                                                      pl.ds(r, per, stride=dil), :]

    h = h_scr[...]
    col = 0
    pq = _dot(h, w_ref[:, col:col + DIL_WIDTH])
    for c in range(DIL_WIDTH // LANES):
        store_dilated(qa_ref, 0, c, rope(pq[:, c * LANES:(c + 1) * LANES]) * scale)
    col += DIL_WIDTH
    pk = _dot(h, w_ref[:, col:col + DIL_WIDTH])
    for c in range(DIL_WIDTH // LANES):
        store_dilated(ka_ref, 1, c, rope(pk[:, c * LANES:(c + 1) * LANES]))
    col += DIL_WIDTH
    pv = _dot(h, w_ref[:, col:col + DIL_WIDTH])
    for c in range(DIL_WIDTH // LANES):
        store_dilated(va_ref, 2, c, pv[:, c * LANES:(c + 1) * LANES])
    col += DIL_WIDTH
    qb_ref[...] = (_dot(h, w_ref[:, col:col + SB_WIDTH]) * scale).astype(BF16)
    col += SB_WIDTH
    kb_ref[...] = _dot(h, w_ref[:, col:col + SB_WIDTH]).astype(BF16)
    col += SB_WIDTH
    vb_ref[...] = _dot(h, w_ref[:, col:col + SB_WIDTH]).astype(BF16)
    col += SB_WIDTH
    uc_ref[...] = _dot(h, w_ref[:, col:col + POOL_WIDTH])


def _inproj(x2d, g, w_in, layer, cos_t, sin_t):
    t, d = x2d.shape
    tm = INPROJ_TILE
    row = lambda width: pl.BlockSpec((tm, width), lambda i: (i, 0))
    widths = (DIL_WIDTH,) * 3 + (SB_WIDTH,) * 3 + (POOL_WIDTH,)
    dtypes = (F32,) * 3 + (BF16,) * 3 + (F32,)
    return pl.pallas_call(
        _inproj_kernel,
        grid=(t // tm,),
        in_specs=[row(d), _const_spec((1, d)), _layer_weight_spec(w_in, layer),
                  row(LANES), row(LANES)],
        out_specs=[row(w) for w in widths],
        out_shape=[jax.ShapeDtypeStruct((t, w), dt) for w, dt in zip(widths, dtypes)],
        scratch_shapes=[pltpu.VMEM((tm, d), BF16),
                        pltpu.VMEM((3, DIL_OUT // LANES, tm, LANES), F32),
                        pltpu.VMEM(w_in.shape[1:], BF16)],
        compiler_params=_cparams(1),
        name="in_proj",
    )(x2d, g, w_in, cos_t, sin_t)


def _dil_kernel(dil, q_ref, kp_ref, kc_ref, vp_ref, vc_ref, o_ref, lse_ref):
    has_prev = pl.program_id(1) > 0
    n = DIL_BLOCK
    n_blocks = q_ref.shape[0] // (n * dil)
    qi = lax.broadcasted_iota(jnp.int32, (n, n), 0)
    kj = lax.broadcasted_iota(jnp.int32, (n, n), 1)
    mask_prev = kj >= qi
    mask_prev_first = jnp.logical_and(mask_prev, has_prev)
    mask_cur = kj <= qi
    lo_head = kj < HEAD_DIM
    ones = jnp.ones((2 * n, LANES), BF16)

    def rows(ref, r, j):
        if dil in DIL_PERMUTED:
            per = INPROJ_TILE // dil
            pieces = [ref[ti * INPROJ_TILE + r * per:ti * INPROJ_TILE + (r + 1) * per, :]
                      for ti in range(j * (n // per), (j + 1) * (n // per))]
            return jnp.concatenate(pieces, axis=0).astype(BF16)
        start = j * n * dil + r
        return ref[pl.ds(start, n, stride=dil) if dil > 1 else pl.ds(start, n), :].astype(BF16)

    units = {}

    def unit(r, j):
        if (r, j) not in units:
            kc, vc = rows(kc_ref, r, j), rows(vc_ref, r, j)
            if j == 0:
                kp, vp = rows(kp_ref, r, 0), rows(vp_ref, r, 0)
            else:
                kp, vp = units[(r, j - 1)]["kc"], units[(r, j - 1)]["vc"]
            vo = jnp.concatenate([jnp.concatenate([vp, vc], axis=0), ones], axis=1)
            units[(r, j)] = dict(q=rows(q_ref, r, j), kc=kc, vc=vc, kp=kp, vo=vo,
                                 pm=mask_prev_first if j == 0 else mask_prev,
                                 start=j * n * dil + r)
        return units[(r, j)]

    def stage_scores(c):
        u = unit(c["r"], c["j"])
        sel = lo_head if c["hh"] == 0 else jnp.logical_not(lo_head)
        qm = jnp.where(sel, u["q"], jnp.zeros((n, LANES), BF16))
        c["sp"] = _dot_nt(qm, u["kp"])
        c["sc"] = _dot_nt(qm, u["kc"])

    def stage_max(c):
        u = unit(c["r"], c["j"])
        c["sp"] = jnp.where(u["pm"], c["sp"], NEG_BIG)
        c["sc"] = jnp.where(mask_cur, c["sc"], NEG_BIG)
        c["m"] = jnp.max(jnp.maximum(c["sp"], c["sc"]), axis=-1, keepdims=True)

    def stage_pv(c):
        u = unit(c["r"], c["j"])
        ep = jnp.exp(c.pop("sp") - c["m"]).astype(BF16)
        ec = jnp.exp(c.pop("sc") - c["m"]).astype(BF16)
        res = _dot(jnp.concatenate([ep, ec], axis=1), u["vo"])
        c["acc"] = res[:, :LANES]
        c["den"] = res[:, LANES:]

    def stage_out(c, pairs):
        c["o"] = c.pop("acc") / c["den"]
        c["lse"] = c["m"] + jnp.log(c.pop("den"))
        key = (c["r"], c["j"])
        pairs.setdefault(key, []).append(c)
        if len(pairs[key]) == 2:
            c0, c1 = pairs.pop(key)
            u = unit(c["r"], c["j"])
            idx = pl.ds(u["start"], n, stride=dil) if dil > 1 else pl.ds(u["start"], n)
            o_ref[idx, :] = jnp.where(lo_head, c0["o"], c1["o"])
            lse_ref[idx, :] = jnp.where(lo_head, c0["lse"], c1["lse"])

    chains = [dict(r=r, j=j, hh=hh) for r in range(dil) for j in range(n_blocks)
              for hh in range(2)]
    stages = (stage_scores, stage_max, stage_pv)
    pairs = {}
    total = len(chains)
    for step in range(total + len(stages) * DIL_SKEW):
        for s, fn in enumerate(stages):
            i = step - s * DIL_SKEW
            if 0 <= i < total:
                fn(chains[i])
        i = step - len(stages) * DIL_SKEW
        if 0 <= i < total:
            stage_out(chains[i], pairs)


def _dilated_group(qa, ka, va, g, batch, seq):
    window, dil = DIL_CONFIGS[g]
    assert window // dil == DIL_BLOCK and seq % DIL_CHUNK == 0
    nc = seq // DIL_CHUNK
    hist = DIL_BLOCK * dil
    t = batch * seq
    n_pairs = DIL_OUT // LANES
    cur = pl.BlockSpec((DIL_CHUNK, LANES), lambda b, c, pr: (b * nc + c, g * n_pairs + pr))
    prev = pl.BlockSpec(
        (hist, LANES),
        lambda b, c, pr: (jnp.maximum((b * nc + c) * (DIL_CHUNK // hist) - 1, 0),
                          g * n_pairs + pr))
    out = pl.BlockSpec((DIL_CHUNK, LANES), lambda b, c, pr: (b * nc + c, pr))
    return pl.pallas_call(
        functools.partial(_dil_kernel, dil),
        grid=(batch, nc, n_pairs),
        in_specs=[cur, prev, cur, prev, cur],
        out_specs=[out, out],
        out_shape=[jax.ShapeDtypeStruct((t, DIL_OUT), F32)] * 2,
        compiler_params=_cparams(3),
        name=f"dilated_d{dil}",
    )(qa, ka, ka, va, va)


def _sb_kernel(q_ref, k_ref, v_ref, o_ref, acc_scr, r_scr):
    blk = SB_BLOCK
    nq = q_ref.shape[0] // blk
    row = lax.broadcasted_iota(jnp.int32, (blk, blk), 0)
    col = lax.broadcasted_iota(jnp.int32, (blk, blk), 1)
    tri = col < row
    lo_head = col < HEAD_DIM
    cum_mat = jnp.where(row > col, 1.0, 0.0).astype(BF16)

    def head_q(q, hh):
        sel = lo_head if hh == 0 else jnp.logical_not(lo_head)
        return jnp.where(sel, q, jnp.zeros_like(q))

    def tile_scores(z, mask):
        sp = jnp.maximum(z, 0.0) + jnp.log(1.0 + jnp.exp(-jnp.abs(z)))
        lom = -sp
        if mask is not None:
            lom = jnp.where(mask, lom, 0.0)
        hi = lom.astype(BF16)
        lo = (lom - hi.astype(F32)).astype(BF16)
        after = _dot(hi, cum_mat) + _dot(lo, cum_mat)
        return z - sp + after, jnp.sum(lom, axis=-1, keepdims=True)

    def tile_weights(logit, mask, r_acc):
        a = jnp.exp(logit + r_acc)
        if mask is not None:
            a = jnp.where(mask, a, 0.0)
        return a.astype(BF16)

    def q_group(blocks):
        chains = []
        for qi, n_tiles in blocks:
            qoff = qi * blk
            koff = (qi - (n_tiles - 1)) * blk
            if not isinstance(qi, int):
                qoff, koff = pl.multiple_of(qoff, blk), pl.multiple_of(koff, blk)
            q = q_ref[pl.ds(qoff, blk), :]
            kw = k_ref[pl.ds(koff, n_tiles * blk), :]
            vw = v_ref[pl.ds(koff, n_tiles * blk), :]
            for hh in range(2):
                chains.append(dict(z=_dot_nt(head_q(q, hh), kw), vw=vw, n=n_tiles,
                                   r=jnp.zeros((blk, 1), F32), acc=jnp.zeros((blk, blk), F32)))
        max_tiles = max(c["n"] for c in chains)
        tiles = [(c, c["n"] - 1 - j) for j in range(max_tiles) for c in chains if j < c["n"]]
        pending = []
        for step in range(len(tiles) + SB_SKEW):
            if step < len(tiles):
                c, w = tiles[step]
                mask = tri if w == c["n"] - 1 else None
                rows = SB_FAR_ROWS if (w == 0 and c["n"] == SB_WINDOW) else blk
                logit, rowsum = tile_scores(c["z"][:rows, w * blk:(w + 1) * blk], mask)
                pending.append((c, w, mask, logit, rowsum, rows))
            if step >= SB_SKEW:
                c, w, mask, logit, rowsum, rows = pending[step - SB_SKEW]
                a = tile_weights(logit, mask, c["r"][:rows])
                pv = _dot(a, c["vw"][w * blk:(w + 1) * blk, :])
                if rows == blk:
                    c["acc"] = c["acc"] + pv
                    c["r"] = c["r"] + rowsum
                else:
                    c["acc"] = jnp.concatenate([c["acc"][:rows] + pv, c["acc"][rows:]], axis=0)
                    c["r"] = jnp.concatenate([c["r"][:rows] + rowsum, c["r"][rows:]], axis=0)
        for b, (qi, _) in enumerate(blocks):
            qoff = qi * blk if isinstance(qi, int) else pl.multiple_of(qi * blk, blk)
            c0, c1 = chains[2 * b], chains[2 * b + 1]
            r_scr[0, pl.ds(qoff, blk), :] = jnp.broadcast_to(c0["r"], (blk, blk))
            r_scr[1, pl.ds(qoff, blk), :] = jnp.broadcast_to(c1["r"], (blk, blk))
            acc_pair = jnp.where(lo_head, c0["acc"], c1["acc"])
            acc_scr[pl.ds(qoff, blk), :] = acc_pair
            o_ref[pl.ds(qoff, blk), :] = acc_pair.astype(BF16)

    first_tail = min(SB_WINDOW - 1, nq)
    n_static = SB_WINDOW + (nq - SB_WINDOW) % SB_GROUP
    q_group([(qi, min(qi + 1, SB_WINDOW)) for qi in range(n_static)])

    def q_loop(it, carry):
        q0 = n_static + it * SB_GROUP
        q_group([(q0 + j, SB_WINDOW) for j in range(SB_GROUP)])
        return carry

    lax.fori_loop(0, (nq - n_static) // SB_GROUP, q_loop, 0)


    @pl.when(jnp.max(r_scr[:, first_tail * blk:, :]) > SB_DEAD_LOG)
    def _tail():
        def q_tail(qi, carry):
            qoff = pl.multiple_of(qi * blk, blk)
            q = q_ref[pl.ds(qoff, blk), :]
            acc0 = acc_scr[pl.ds(qoff, blk), :]
            res = []
            for hh in range(2):
                qm = head_q(q, hh)
                r0 = r_scr[hh, pl.ds(qoff, blk), :]

                def cond(c):
                    return jnp.logical_and(c[0] >= 0, c[3] > SB_DEAD_LOG)

                def body(c, qm=qm):
                    off = pl.multiple_of(c[0] * blk, blk)
                    z = _dot_nt(qm, k_ref[pl.ds(off, blk), :])
                    todo = jnp.logical_or(row >= SB_FAR_ROWS, c[0] != qi - (SB_WINDOW - 1))
                    logit, rowsum = tile_scores(z, todo)
                    a = tile_weights(logit, todo, c[1])
                    r_new = c[1] + rowsum
                    acc_new = c[2] + _dot(a, v_ref[pl.ds(off, blk), :])
                    return c[0] - 1, r_new, acc_new, jnp.max(r_new)

                c0 = (qi - (SB_WINDOW - 1), r0, acc0, jnp.max(r0))
                res.append(lax.while_loop(cond, body, c0)[2])
            o_ref[pl.ds(qoff, blk), :] = jnp.where(lo_head, res[0], res[1]).astype(BF16)
            return carry

        lax.fori_loop(first_tail, nq, q_tail, 0)


def _stick_breaking(qb, kb, vb, batch, seq):
    view = lambda a: a.reshape(batch, seq, SB_WIDTH)
    spec = pl.BlockSpec((None, seq, LANES), lambda b, hp: (b, 0, hp))
    o = pl.pallas_call(
        _sb_kernel,
        grid=(batch, SB_WIDTH // LANES),
        in_specs=[spec, spec, spec],
        out_specs=spec,
        out_shape=jax.ShapeDtypeStruct((batch, seq, SB_WIDTH), BF16),
        scratch_shapes=[pltpu.VMEM((seq, LANES), F32), pltpu.VMEM((2, seq, LANES), F32)],
        compiler_params=_cparams(2),
        name="stick_breaking",
    )(view(qb), view(kb), view(vb))
    return o.reshape(batch * seq, SB_WIDTH)


def _merge_kernel(seq, x_ref, o1_ref, o2_ref, o3_ref, l1_ref, l2_ref, l3_ref, sb_ref,
                  uc_ref, uh_ref, g_ref, wg32_ref, bg_ref, wp32_ref, ps_ref,
                  wa32_ref, wb32_ref, wc32_ref, wo32_ref, out_ref,
                  h_scr, ub_scr, yc_scr, m_scr, wg_ref, wp_ref, wa_ref, wb_ref, wc_ref, wo_ref):
    i = pl.program_id(0)
    _cast_weights_once(i == 0, [(wg32_ref, wg_ref), (wp32_ref, wp_ref), (wa32_ref, wa_ref),
                                (wb32_ref, wb_ref), (wc32_ref, wc_ref), (wo32_ref, wo_ref)])
    tm, d = x_ref.shape
    x = x_ref[...]
    h_scr[...] = _rms(x, g_ref[...]).astype(BF16)
    cw = 256

    def gate_chunk(c):
        gates = []
        for br in range(N_BRANCH):
            gsl = slice(br * d + c * cw, br * d + (c + 1) * cw)
            gates.append(jax.nn.sigmoid(_dot(h_scr[...], wg_ref[:, gsl]) + bg_ref[:, gsl]))
        return gates

    l1, l2, l3 = l1_ref[...], l2_ref[...], l3_ref[...]
    m = jnp.maximum(jnp.maximum(l1, l2), l3)
    e1, e2, e3 = jnp.exp(l1 - m), jnp.exp(l2 - m), jnp.exp(l3 - m)
    dil = ((e1 * o1_ref[...] + e2 * o2_ref[...] + e3 * o3_ref[...]) / (e1 + e2 + e3)).astype(BF16)

    row0 = (i * tm) % seq
    halo = jnp.where(row0 == 0, 0.0, uh_ref[...])
    ub_scr[0:POOL_HALO, :] = halo
    ub_scr[POOL_HALO:POOL_HALO + tm, :] = uc_ref[...]
    t_in_seq = row0 + lax.broadcasted_iota(jnp.int32, (tm, 1), 0)
    for g, w in enumerate(POOL_WINDOWS):
        sl = slice(g * POOL_GROUP, (g + 1) * POOL_GROUP)
        u = ub_scr[POOL_HALO:POOL_HALO + tm, sl]
        acc = u
        for j in range(1, w):
            acc = acc + ub_scr[POOL_HALO - j:POOL_HALO - j + tm, sl]
        cnt = jnp.minimum(t_in_seq + 1, w).astype(F32)
        pooled = acc / cnt - u
        y = _dot(pooled.astype(BF16), wp_ref[g]) * ps_ref[:, sl]
        yc_scr[:, sl] = y.astype(BF16)

    sb = sb_ref[...]
    yc_in = yc_scr[...]
    for c in range(d // cw):
        sl = slice(c * cw, (c + 1) * cw)
        gates = gate_chunk(c)
        merged = (gates[0] * _dot(dil, wa_ref[:, sl]) + gates[1] * _dot(sb, wb_ref[:, sl])
                  + gates[2] * _dot(yc_in, wc_ref[:, sl]))
        m_scr[:, sl] = merged.astype(BF16)
    out_ref[...] = x + _dot(m_scr[...], wo_ref[...])


def _merge(x2d, dil_outs, sb_out, uc, seq, layer, g, w_gate, b_gate, w_pool, pool_scale,
           w_up_a, w_up_b, w_up_c, w_out):
    t, d = x2d.shape
    tm = MERGE_TILE
    row = lambda width: pl.BlockSpec((tm, width), lambda i: (i, 0))
    halo = pl.BlockSpec((POOL_HALO, POOL_WIDTH),
                        lambda i: (jnp.maximum(i * (tm // POOL_HALO) - 1, 0), 0))
    (o1, l1), (o2, l2), (o3, l3) = dil_outs
    big = (w_gate, w_pool, w_up_a, w_up_b, w_up_c, w_out)
    lw = lambda w: _layer_weight_spec(w, layer)
    return pl.pallas_call(
        functools.partial(_merge_kernel, seq),
        grid=(t // tm,),
        in_specs=[row(d)] + [row(DIL_OUT)] * 6 + [row(SB_WIDTH), row(POOL_WIDTH), halo]
                 + [_const_spec(g.shape), lw(w_gate), _const_spec(b_gate.shape), lw(w_pool),
                    _const_spec(pool_scale.shape), lw(w_up_a), lw(w_up_b), lw(w_up_c),
                    lw(w_out)],
        out_specs=row(d),
        out_shape=jax.ShapeDtypeStruct((t, d), F32),
        scratch_shapes=[pltpu.VMEM((tm, d), BF16),
                        pltpu.VMEM((POOL_HALO + tm, POOL_WIDTH), F32),
                        pltpu.VMEM((tm, POOL_WIDTH), BF16),
                        pltpu.VMEM((tm, d), BF16)]
                       + [pltpu.VMEM(w.shape[1:], BF16) for w in big],
        compiler_params=_cparams(1),
        name="merge",
    )(x2d, o1, o2, o3, l1, l2, l3, sb_out, uc, uc, g, w_gate, b_gate, w_pool, pool_scale,
      w_up_a, w_up_b, w_up_c, w_out)


ROUTER_EXP_LANE0 = MOE_GROUPS
MOE_TILE = 1024
MOE_SUB = 256
MOE_SLOTS = 80


def _moe_kernel(final, x_ref, g_ref, wr_ref, br_ref, wg_ref, wu_ref, wd_ref,
                p_ref, gp_ref, wpg32_ref, wpi32_ref, gf_ref, out_ref,
                xn_scr, gate_scr, ghi_scr, glo_scr, sel_scr, selt_scr, acc_scr, ovf_smem,
                wpg_ref, wpi_ref):
    grp = pl.program_id(1)
    _cast_weights_once(jnp.logical_and(pl.program_id(0) == 0, grp == 0),
                       [(wpg32_ref, wpg_ref), (wpi32_ref, wpi_ref)])
    tm, d = x_ref.shape
    n_sub = tm // MOE_SUB
    lane_f = lax.broadcasted_iota(jnp.int32, (tm, LANES), 1).astype(F32)

    def route():
        xn = _rms(x_ref[...], g_ref[...])
        a_hi = xn.astype(BF16)
        xn_scr[...] = a_hi
        a_lo = (xn - a_hi.astype(F32)).astype(BF16)
        yield
        r_hi = _dot(a_hi, wr_ref[...])
        logits = (r_hi[:, :LANES] + r_hi[:, LANES:] + _dot(a_lo, wr_ref[:, :LANES])
                  + br_ref[...])
        yield
        is_grp = lane_f < MOE_GROUPS
        lg = jnp.where(is_grp, logits, NEG_BIG)
        mg = jnp.max(lg, axis=-1, keepdims=True)
        p1 = 1.0 / jnp.sum(jnp.exp(lg - mg), axis=-1, keepdims=True)
        gi = jnp.min(jnp.where(lg == mg, lane_f, float(LANES)), axis=-1, keepdims=True)
        lane_grp = jnp.floor((lane_f - ROUTER_EXP_LANE0) * (1.0 / EXP_PER_GROUP))
        in_grp = jnp.logical_and(
            jnp.logical_and(lane_f >= ROUTER_EXP_LANE0, lane_f < ROUTER_EXP_LANE0 + N_EXPERTS),
            lane_grp == gi)
        le = jnp.where(in_grp, logits, NEG_BIG)
        v1 = jnp.max(le, axis=-1, keepdims=True)
        i1 = jnp.min(jnp.where(le == v1, lane_f, float(LANES)), axis=-1, keepdims=True)
        le2 = jnp.where(lane_f == i1, NEG_BIG, le)
        v2 = jnp.max(le2, axis=-1, keepdims=True)
        i2 = jnp.min(jnp.where(le2 == v2, lane_f, float(LANES)), axis=-1, keepdims=True)
        t = jnp.exp(v2 - v1)
        w1 = p1 / (1.0 + t)
        w2 = p1 * t / (1.0 + t)
        gate = jnp.where(lane_f == i1, w1, 0.0) + jnp.where(lane_f == i2, w2, 0.0)
        gate_scr[...] = gate
        g_hi = gate.astype(BF16)
        ghi_scr[...] = g_hi
        glo_scr[...] = (gate - g_hi.astype(F32)).astype(BF16)
        yield

        onehot = jnp.where(lane_f == gi, 1.0, 0.0).astype(BF16)
        r = lax.broadcasted_iota(jnp.int32, (MOE_SUB, MOE_SUB), 0)
        c = lax.broadcasted_iota(jnp.int32, (MOE_SUB, MOE_SUB), 1)
        earlier = jnp.where(c < r, 1.0, 0.0).astype(BF16)
        fullest = jnp.zeros((1, LANES), F32)
        gi_b = jnp.broadcast_to(gi, (tm, LANES))
        for s in range(n_sub):
            rs = slice(s * MOE_SUB, (s + 1) * MOE_SUB)
            oh = onehot[rs, :]
            ahead = _dot(earlier, oh)
            slot = jnp.sum(jnp.where(oh > 0, ahead, 0.0), axis=-1, keepdims=True)
            sel_b = jnp.where(slot < LANES, slot + float(LANES) * gi_b[rs, :], -1.0)
            sel_scr[rs, :] = sel_b
            selt_scr[:, rs] = jnp.transpose(sel_b)[:8, :]
            fullest = jnp.maximum(fullest, jnp.sum(oh.astype(F32), axis=0, keepdims=True))
        lane1 = lax.broadcasted_iota(jnp.int32, (1, LANES), 1)
        for gidx in range(MOE_GROUPS):
            most = jnp.max(jnp.where(lane1 == gidx, fullest, 0.0))
            ovf_smem[gidx] = (most > MOE_SLOTS).astype(jnp.int32)
        yield
        acc_scr[...] = x_ref[...]

    @pl.when(jnp.logical_and(pl.program_id(0) == 0, grp == 0))
    def _first_route():
        for _ in route():
            pass

    def experts(rows, gates):
        lane_r = lax.broadcasted_iota(jnp.int32, gates.shape, 1)
        y = None
        for j in range(EXP_PER_GROUP):
            hg = _dot(rows, wg_ref[j])
            hu = _dot(rows, wu_ref[j])
            here = lane_r == ROUTER_EXP_LANE0 + EXP_PER_GROUP * grp + j
            gcol = jnp.sum(jnp.where(here, gates, 0.0), axis=-1, keepdims=True)
            hid = ((hg * jax.nn.sigmoid(hg)) * hu * gcol).astype(BF16)
            yj = _dot(hid, wd_ref[j])
            y = yj if y is None else y + yj
        return y

    key = (grp * LANES).astype(F32)

    @pl.when(ovf_smem[grp] == 0)
    def _compact():
        slot_rows = lax.broadcasted_iota(jnp.int32, (MOE_SLOTS, MOE_SUB), 0).astype(F32) + key
        slot_cols = lax.broadcasted_iota(jnp.int32, (MOE_SUB, LANES), 1).astype(F32) + key
        xs, gs = [], []
        for s in range(n_sub):
            rs = slice(s * MOE_SUB, (s + 1) * MOE_SUB)
            pick = jnp.where(selt_scr[0:1, rs] == slot_rows, 1.0, 0.0).astype(BF16)
            xs.append(_dot(pick, xn_scr[rs, :]).astype(BF16))
            gs.append(_dot(pick, ghi_scr[rs, :]) + _dot(pick, glo_scr[rs, :]))
        y = experts(jnp.concatenate(xs, axis=0), jnp.concatenate(gs, axis=0))
        y = jnp.concatenate([y, jnp.zeros((LANES - MOE_SLOTS, d), F32)], axis=0)
        y_hi = y.astype(BF16)
        y_lo = (y - y_hi.astype(F32)).astype(BF16)
        for s in range(n_sub):
            rs = slice(s * MOE_SUB, (s + 1) * MOE_SUB)
            ys = slice(s * MOE_SLOTS, s * MOE_SLOTS + LANES)
            place = jnp.where(sel_scr[rs, :] == slot_cols, 1.0, 0.0).astype(BF16)
            acc_scr[rs, :] += _dot(jnp.concatenate([place, place], axis=1),
                                   jnp.concatenate([y_hi[ys, :], y_lo[ys, :]], axis=0))

    @pl.when(ovf_smem[grp] != 0)
    def _dense():
        for s in range(n_sub):
            rs = slice(s * MOE_SUB, (s + 1) * MOE_SUB)
            acc_scr[rs, :] += experts(xn_scr[rs, :], gate_scr[rs, :])

    @pl.when(grp == MOE_GROUPS - 1)
    def _finish():
        next_route = route()
        for s in range(n_sub):
            next(next_route, None)
            rs = slice(s * MOE_SUB, (s + 1) * MOE_SUB)
            x = acc_scr[rs, :]
            gate = jax.nn.sigmoid(_dot(_rms(x, gp_ref[...]).astype(BF16), wpg_ref[...]))
            y = x + _dot(p_ref[rs, :].astype(BF16), wpi_ref[...]) * gate
            if final:
                y = _rms(y, gf_ref[...])
            out_ref[rs, :] = y
        for _ in next_route:
            pass


def _moe_ple(x2d, g, w_router, b_router, w_exp_gate, w_exp_up, w_exp_down,
             p2d, layer, g_ple, w_ple_gate, w_ple_in, g_final, final):
    t, d = x2d.shape
    f = w_exp_gate.shape[-1]
    tm = MOE_TILE
    row = pl.BlockSpec((tm, d), lambda i, e: (i, 0))
    n_tiles = t // tm
    x_ahead = pl.BlockSpec(
        (tm, d), lambda i, e: (jnp.minimum(i + (e == MOE_GROUPS - 1).astype(jnp.int32),
                                           n_tiles - 1), 0))
    group_w = lambda a, b: pl.BlockSpec((EXP_PER_GROUP, a, b), lambda i, e: (e, 0, 0))
    return pl.pallas_call(
        functools.partial(_moe_kernel, final),
        grid=(n_tiles, MOE_GROUPS),
        in_specs=[x_ahead, _const_spec((1, d)), _const_spec(w_router.shape),
                  _const_spec(b_router.shape),
                  group_w(d, f), group_w(d, f), group_w(f, d),
                  pl.BlockSpec((tm, p2d.shape[1]), lambda i, e: (i, 0)),
                  _const_spec((1, d)), _layer_weight_spec(w_ple_gate, layer),
                  _layer_weight_spec(w_ple_in, layer), _const_spec((1, d))],
        out_specs=row,
        out_shape=jax.ShapeDtypeStruct((t, d), F32),
        scratch_shapes=[pltpu.VMEM((tm, d), BF16), pltpu.VMEM((tm, LANES), F32),
                        pltpu.VMEM((tm, LANES), BF16), pltpu.VMEM((tm, LANES), BF16),
                        pltpu.VMEM((tm, LANES), F32), pltpu.VMEM((8, tm), F32),
                        pltpu.VMEM((tm, d), F32), pltpu.SMEM((MOE_GROUPS,), jnp.int32),
                        pltpu.VMEM(w_ple_gate.shape[1:], BF16),
                        pltpu.VMEM(w_ple_in.shape[1:], BF16)],
        compiler_params=_cparams(2),
        name="moe_ple",
    )(x2d, g, w_router, b_router, w_exp_gate, w_exp_up, w_exp_down,
      p2d, g_ple, w_ple_gate, w_ple_in, g_final)


def _router_weights(w_grp, b_grp, w_exp, b_exp):
    d = w_grp.shape[0]
    w = jnp.concatenate([w_grp, w_exp.transpose(1, 0, 2).reshape(d, N_EXPERTS)], axis=1)
    b = jnp.concatenate([b_grp, b_exp.reshape(N_EXPERTS)])
    pad = LANES - w.shape[1]
    w = jnp.pad(w, ((0, 0), (0, pad)))
    w_hi = w.astype(BF16)
    w_lo = (w - w_hi.astype(F32)).astype(BF16)
    return jnp.concatenate([w_hi, w_lo], axis=1), jnp.pad(b, (0, pad))[None, :]


def kernel(x, p, positions, norm_mix, w_in, w_gate, b_gate, w_pool, pool_scale, w_up_a, w_up_b, w_up_c, w_out, norm_moe, w_router_grp, b_router_grp, w_router_exp, b_router_exp, w_exp_gate, w_exp_up, w_exp_down, norm_ple, w_ple_in, w_ple_gate, norm_final):
    batch, seq, d = x.shape
    depth = w_in.shape[0]
    t = batch * seq
    x2d = x.reshape(t, d)
    cos_t, sin_t = _rope_tables(positions)
    bf = lambda a: a.astype(BF16)
    for i in range(depth):
        qa, ka, va, qb, kb, vb, uc = _inproj(x2d, norm_mix[i][None, :], w_in, i, cos_t, sin_t)
        dil_outs = [_dilated_group(qa, ka, va, g, batch, seq) for g in range(len(DIL_CONFIGS))]
        sb_out = _stick_breaking(qb, kb, vb, batch, seq)
        x2d = _merge(x2d, dil_outs, sb_out, uc, seq, i, norm_mix[i][None, :], w_gate,
                     b_gate[i][None, :], w_pool, pool_scale[i][None, :],
                     w_up_a, w_up_b, w_up_c, w_out)
        w_r, b_r = _router_weights(w_router_grp[i], b_router_grp[i],
                                   w_router_exp[i], b_router_exp[i])
        x2d = _moe_ple(x2d, norm_moe[i][None, :], w_r, b_r, bf(w_exp_gate[i]),
                       bf(w_exp_up[i]), bf(w_exp_down[i]), p[i].reshape(t, -1), i,
                       norm_ple[i][None, :], w_ple_gate, w_ple_in,
                       norm_final[None, :], i == depth - 1)
    return x2d.reshape(batch, seq, d)
```

```python
import functools

import jax
import jax.numpy as jnp
from jax import lax
from jax.experimental import pallas as pl
from jax.experimental.pallas import tpu as pltpu

F32 = jnp.float32
BF16 = jnp.bfloat16

HEAD_DIM = 64
ROPE_THETA = 10000.0
DIL_CONFIGS = ((128, 1), (512, 4), (2048, 16))
DIL_HEADS = 4
DIL_OUT = DIL_HEADS * HEAD_DIM
DIL_WIDTH = len(DIL_CONFIGS) * DIL_OUT
DIL_BLOCK = 128
DIL_CHUNK = 2048
DIL_SKEW = 3
INPROJ_TILE = 512
MERGE_TILE = 512
DIL_PERMUTED = (16,)
SB_HEADS = 8
SB_WIDTH = SB_HEADS * HEAD_DIM
SB_BLOCK = 128
POOL_WINDOWS = (2, 4, 8, 16)
POOL_GROUP = 128
POOL_WIDTH = len(POOL_WINDOWS) * POOL_GROUP
POOL_HALO = 16
N_BRANCH = 3
MOE_GROUPS = 4
EXP_PER_GROUP = 4
N_EXPERTS = MOE_GROUPS * EXP_PER_GROUP
EPS = 1e-6

LANES = 128
NEG_BIG = -1e30
SB_DEAD_LOG = -105.0
SB_WINDOW = 3
SB_GROUP = 28
SB_SKEW = 8
SB_FAR_ROWS = 64
VMEM_LIMIT = 56 * 1024 * 1024


def _cparams(n_axes):
    return pltpu.CompilerParams(
        dimension_semantics=("arbitrary",) * n_axes, vmem_limit_bytes=VMEM_LIMIT)


def _rms(x, g):
    ms = jnp.mean(x * x, axis=-1, keepdims=True)
    return x * lax.rsqrt(ms + EPS) * g


def _dot(a, b):
    return jnp.dot(a, b, preferred_element_type=F32)


def _dot_nt(a, b):
    return lax.dot_general(a, b, (((1,), (1,)), ((), ())), preferred_element_type=F32)


def _const_spec(shape):
    zeros = (0,) * len(shape)
    return pl.BlockSpec(shape, lambda *_: zeros)


def _layer_weight_spec(w, layer):
    tail = (0,) * (w.ndim - 1)
    return pl.BlockSpec((None,) + w.shape[1:], lambda *_: (layer,) + tail,
                        pipeline_mode=pl.Buffered(1))


def _cast_weights_once(first_step, pairs):
    @pl.when(first_step)
    def _cast():
        for src, dst in pairs:
            if len(src.shape) == 3:
                for k in range(src.shape[0]):
                    dst[k] = src[k].astype(BF16)
            else:
                step = 2 * LANES
                for c in range(0, src.shape[1], step):
                    dst[:, c:c + step] = src[:, c:c + step].astype(BF16)


def _rope_table_kernel(pos_ref, inv_ref, sgn_ref, cos_ref, sin_ref):
    ang = inv_ref[...] * pos_ref[...].astype(F32)
    reps = LANES // ang.shape[0]
    cos_ref[...] = jnp.transpose(jnp.concatenate([jnp.cos(ang)] * reps, axis=0))
    sin_ref[...] = jnp.transpose(jnp.concatenate([jnp.sin(ang)] * reps, axis=0) * sgn_ref[...])


def _rope_tables(positions):
    t = positions.size
    half = HEAD_DIM // 2
    inv_col = (ROPE_THETA ** (-jnp.arange(half, dtype=F32) / half))[:, None]
    sgn_col = jnp.tile(jnp.concatenate([-jnp.ones(half, F32), jnp.ones(half, F32)]),
                       LANES // HEAD_DIM)[:, None]
    tm = 2048
    return pl.pallas_call(
        _rope_table_kernel,
        grid=(t // tm,),
        in_specs=[pl.BlockSpec((None, 1, tm), lambda i: (i, 0, 0)),
                  _const_spec((half, 1)), _const_spec((LANES, 1))],
        out_specs=[pl.BlockSpec((tm, LANES), lambda i: (i, 0))] * 2,
        out_shape=[jax.ShapeDtypeStruct((t, LANES), F32)] * 2,
        compiler_params=_cparams(1),
        name="rope_table",
    )(positions.reshape(t // tm, 1, tm), inv_col, sgn_col)


def _inproj_kernel(x_ref, g_ref, w32_ref, cos_ref, sin_ref,
                   qa_ref, ka_ref, va_ref, qb_ref, kb_ref, vb_ref, uc_ref,
                   h_scr, perm_scr, w_ref):
    _cast_weights_once(pl.program_id(0) == 0, [(w32_ref, w_ref)])
    h_scr[...] = _rms(x_ref[...], g_ref[...]).astype(BF16)
    cos = cos_ref[...]
    sin = sin_ref[...]
    tm = cos.shape[0]
    lane = lax.broadcasted_iota(jnp.int32, (tm, LANES), 1)
    first_half = (lane & (HEAD_DIM // 2)) == 0
    scale = HEAD_DIM ** -0.5

    def rope(t):
        partner = jnp.where(first_half, pltpu.roll(t, LANES - HEAD_DIM // 2, 1),
                            pltpu.roll(t, HEAD_DIM // 2, 1))
        return t * cos + partner * sin

    def store_dilated(ref, which, c, val):
        sl = slice(c * LANES, (c + 1) * LANES)
        dil = DIL_CONFIGS[c // (DIL_OUT // LANES)][1]
        if dil not in DIL_PERMUTED:
            ref[:, sl] = val
            return
        per = tm // dil
        perm_scr[which, c % (DIL_OUT // LANES)] = val
        for r in range(dil):
            ref[r * per:(r + 1) * per, sl] = perm_scr[which, c % (DIL_OUT // LANES),
                                                      pl.ds(r, per, stride=dil), :]

    h = h_scr[...]
    col = 0
    pq = _dot(h, w_ref[:, col:col + DIL_WIDTH])
    for c in range(DIL_WIDTH // LANES):
        store_dilated(qa_ref, 0, c, rope(pq[:, c * LANES:(c + 1) * LANES]) * scale)
    col += DIL_WIDTH
    pk = _dot(h, w_ref[:, col:col + DIL_WIDTH])
    for c in range(DIL_WIDTH // LANES):
        store_dilated(ka_ref, 1, c, rope(pk[:, c * LANES:(c + 1) * LANES]))
    col += DIL_WIDTH
    pv = _dot(h, w_ref[:, col:col + DIL_WIDTH])
    for c in range(DIL_WIDTH // LANES):
        store_dilated(va_ref, 2, c, pv[:, c * LANES:(c + 1) * LANES])
    col += DIL_WIDTH
    qb_ref[...] = (_dot(h, w_ref[:, col:col + SB_WIDTH]) * scale).astype(BF16)
    col += SB_WIDTH
    kb_ref[...] = _dot(h, w_ref[:, col:col + SB_WIDTH]).astype(BF16)
    col += SB_WIDTH
    vb_ref[...] = _dot(h, w_ref[:, col:col + SB_WIDTH]).astype(BF16)
    col += SB_WIDTH
    uc_ref[...] = _dot(h, w_ref[:, col:col + POOL_WIDTH])


def _inproj(x2d, g, w_in, layer, cos_t, sin_t):
    t, d = x2d.shape
    tm = INPROJ_TILE
    row = lambda width: pl.BlockSpec((tm, width), lambda i: (i, 0))
    widths = (DIL_WIDTH,) * 3 + (SB_WIDTH,) * 3 + (POOL_WIDTH,)
    dtypes = (F32,) * 3 + (BF16,) * 3 + (F32,)
    return pl.pallas_call(
        _inproj_kernel,
        grid=(t // tm,),
        in_specs=[row(d), _const_spec((1, d)), _layer_weight_spec(w_in, layer),
                  row(LANES), row(LANES)],
        out_specs=[row(w) for w in widths],
        out_shape=[jax.ShapeDtypeStruct((t, w), dt) for w, dt in zip(widths, dtypes)],
        scratch_shapes=[pltpu.VMEM((tm, d), BF16),
                        pltpu.VMEM((3, DIL_OUT // LANES, tm, LANES), F32),
                        pltpu.VMEM(w_in.shape[1:], BF16)],
        compiler_params=_cparams(1),
        name="in_proj",
    )(x2d, g, w_in, cos_t, sin_t)


def _dil_kernel(*refs):
    n_grp = len(DIL_CONFIGS)
    out_ref, o_scr, l_scr = refs[5 * n_grp:]
    for g, (_, dil) in enumerate(DIL_CONFIGS):
        _dil_group(dil, *refs[5 * g:5 * g + 5], o_scr.at[g], l_scr.at[g])
    lses = [l_scr[g] for g in range(n_grp)]
    m = functools.reduce(jnp.maximum, lses)
    es = [jnp.exp(l - m) for l in lses]
    num = sum(e * o_scr[g] for g, e in enumerate(es))
    out_ref[...] = (num / sum(es)).astype(BF16)


def _dil_group(dil, q_ref, kp_ref, kc_ref, vp_ref, vc_ref, o_ref, lse_ref):
    has_prev = pl.program_id(1) > 0
    n = DIL_BLOCK
    n_blocks = q_ref.shape[0] // (n * dil)
    qi = lax.broadcasted_iota(jnp.int32, (n, n), 0)
    kj = lax.broadcasted_iota(jnp.int32, (n, n), 1)
    mask_prev = kj >= qi
    mask_prev_first = jnp.logical_and(mask_prev, has_prev)
    mask_cur = kj <= qi
    lo_head = kj < HEAD_DIM
    ones = jnp.ones((2 * n, LANES), BF16)

    def rows(ref, r, j):
        if dil in DIL_PERMUTED:
            per = INPROJ_TILE // dil
            pieces = [ref[ti * INPROJ_TILE + r * per:ti * INPROJ_TILE + (r + 1) * per, :]
                      for ti in range(j * (n // per), (j + 1) * (n // per))]
            return jnp.concatenate(pieces, axis=0).astype(BF16)
        start = j * n * dil + r
        return ref[pl.ds(start, n, stride=dil) if dil > 1 else pl.ds(start, n), :].astype(BF16)

    units = {}

    def unit(r, j):
        if (r, j) not in units:
            kc, vc = rows(kc_ref, r, j), rows(vc_ref, r, j)
            if j == 0:
                kp, vp = rows(kp_ref, r, 0), rows(vp_ref, r, 0)
            else:
                kp, vp = units[(r, j - 1)]["kc"], units[(r, j - 1)]["vc"]
            vo = jnp.concatenate([jnp.concatenate([vp, vc], axis=0), ones], axis=1)
            units[(r, j)] = dict(q=rows(q_ref, r, j), kc=kc, vc=vc, kp=kp, vo=vo,
                                 pm=mask_prev_first if j == 0 else mask_prev,
                                 start=j * n * dil + r)
        return units[(r, j)]

    def stage_scores(c):
        u = unit(c["r"], c["j"])
        sel = lo_head if c["hh"] == 0 else jnp.logical_not(lo_head)
        qm = jnp.where(sel, u["q"], jnp.zeros((n, LANES), BF16))
        c["sp"] = _dot_nt(qm, u["kp"])
        c["sc"] = _dot_nt(qm, u["kc"])

    def stage_max(c):
        u = unit(c["r"], c["j"])
        c["sp"] = jnp.where(u["pm"], c["sp"], NEG_BIG)
        c["sc"] = jnp.where(mask_cur, c["sc"], NEG_BIG)
        c["m"] = jnp.max(jnp.maximum(c["sp"], c["sc"]), axis=-1, keepdims=True)

    def stage_pv(c):
        u = unit(c["r"], c["j"])
        ep = jnp.exp(c.pop("sp") - c["m"]).astype(BF16)
        ec = jnp.exp(c.pop("sc") - c["m"]).astype(BF16)
        res = _dot(jnp.concatenate([ep, ec], axis=1), u["vo"])
        c["acc"] = res[:, :LANES]
        c["den"] = res[:, LANES:]

    def stage_out(c, pairs):
        c["o"] = c.pop("acc") / c["den"]
        c["lse"] = c["m"] + jnp.log(c.pop("den"))
        key = (c["r"], c["j"])
        pairs.setdefault(key, []).append(c)
        if len(pairs[key]) == 2:
            c0, c1 = pairs.pop(key)
            u = unit(c["r"], c["j"])
            idx = pl.ds(u["start"], n, stride=dil) if dil > 1 else pl.ds(u["start"], n)
            o_ref[idx, :] = jnp.where(lo_head, c0["o"], c1["o"])
            lse_ref[idx, :] = jnp.where(lo_head, c0["lse"], c1["lse"])

    chains = [dict(r=r, j=j, hh=hh) for r in range(dil) for j in range(n_blocks)
              for hh in range(2)]
    stages = (stage_scores, stage_max, stage_pv)
    pairs = {}
    total = len(chains)
    for step in range(total + len(stages) * DIL_SKEW):
        for s, fn in enumerate(stages):
            i = step - s * DIL_SKEW
            if 0 <= i < total:
                fn(chains[i])
        i = step - len(stages) * DIL_SKEW
        if 0 <= i < total:
            stage_out(chains[i], pairs)


def _dilated(qa, ka, va, batch, seq):
    assert seq % DIL_CHUNK == 0
    nc = seq // DIL_CHUNK
    t = batch * seq
    n_pairs = DIL_OUT // LANES
    in_specs = []
    for g, (window, dil) in enumerate(DIL_CONFIGS):
        assert window // dil == DIL_BLOCK
        hist = DIL_BLOCK * dil
        cur = pl.BlockSpec((DIL_CHUNK, LANES),
                           lambda b, c, pr, g=g: (b * nc + c, g * n_pairs + pr))
        prev = pl.BlockSpec(
            (hist, LANES),
            lambda b, c, pr, g=g, hist=hist: (
                jnp.maximum((b * nc + c) * (DIL_CHUNK // hist) - 1, 0), g * n_pairs + pr))
        in_specs += [cur, prev, cur, prev, cur]
    n_grp = len(DIL_CONFIGS)
    return pl.pallas_call(
        _dil_kernel,
        grid=(batch, nc, n_pairs),
        in_specs=in_specs,
        out_specs=pl.BlockSpec((DIL_CHUNK, LANES), lambda b, c, pr: (b * nc + c, pr)),
        out_shape=jax.ShapeDtypeStruct((t, DIL_OUT), BF16),
        scratch_shapes=[pltpu.VMEM((n_grp, DIL_CHUNK, LANES), F32)] * 2,
        compiler_params=_cparams(3),
        name="dilated",
    )(*([qa, ka, ka, va, va] * n_grp))


def _sb_kernel(q_ref, k_ref, v_ref, o_ref, acc_scr, r_scr):
    blk = SB_BLOCK
    nq = q_ref.shape[0] // blk
    row = lax.broadcasted_iota(jnp.int32, (blk, blk), 0)
    col = lax.broadcasted_iota(jnp.int32, (blk, blk), 1)
    tri = col < row
    lo_head = col < HEAD_DIM
    cum_mat = jnp.where(row > col, 1.0, 0.0).astype(BF16)

    def head_q(q, hh):
        sel = lo_head if hh == 0 else jnp.logical_not(lo_head)
        return jnp.where(sel, q, jnp.zeros_like(q))

    def tile_scores(z, mask):
        sp = jnp.maximum(z, 0.0) + jnp.log(1.0 + jnp.exp(-jnp.abs(z)))
        lom = -sp
        if mask is not None:
            lom = jnp.where(mask, lom, 0.0)
        hi = lom.astype(BF16)
        lo = (lom - hi.astype(F32)).astype(BF16)
        after = _dot(hi, cum_mat) + _dot(lo, cum_mat)
        return z - sp + after, jnp.sum(lom, axis=-1, keepdims=True)

    def tile_weights(logit, mask, r_acc):
        a = jnp.exp(logit + r_acc)
        if mask is not None:
            a = jnp.where(mask, a, 0.0)
        return a.astype(BF16)

    def q_group(blocks):
        chains = []
        for qi, n_tiles in blocks:
            qoff = qi * blk
            koff = (qi - (n_tiles - 1)) * blk
            if not isinstance(qi, int):
                qoff, koff = pl.multiple_of(qoff, blk), pl.multiple_of(koff, blk)
            q = q_ref[pl.ds(qoff, blk), :]
            kw = k_ref[pl.ds(koff, n_tiles * blk), :]
            vw = v_ref[pl.ds(koff, n_tiles * blk), :]
            for hh in range(2):
                chains.append(dict(z=_dot_nt(head_q(q, hh), kw), vw=vw, n=n_tiles,
                                   r=jnp.zeros((blk, 1), F32), acc=jnp.zeros((blk, blk), F32)))
        max_tiles = max(c["n"] for c in chains)
        tiles = [(c, c["n"] - 1 - j) for j in range(max_tiles) for c in chains if j < c["n"]]
        pending = []
        for step in range(len(tiles) + SB_SKEW):
            if step < len(tiles):
                c, w = tiles[step]
                mask = tri if w == c["n"] - 1 else None
                rows = SB_FAR_ROWS if (w == 0 and c["n"] == SB_WINDOW) else blk
                logit, rowsum = tile_scores(c["z"][:rows, w * blk:(w + 1) * blk], mask)
                pending.append((c, w, mask, logit, rowsum, rows))
            if step >= SB_SKEW:
                c, w, mask, logit, rowsum, rows = pending[step - SB_SKEW]
                a = tile_weights(logit, mask, c["r"][:rows])
                pv = _dot(a, c["vw"][w * blk:(w + 1) * blk, :])
                if rows == blk:
                    c["acc"] = c["acc"] + pv
                    c["r"] = c["r"] + rowsum
                else:
                    c["acc"] = jnp.concatenate([c["acc"][:rows] + pv, c["acc"][rows:]], axis=0)
                    c["r"] = jnp.concatenate([c["r"][:rows] + rowsum, c["r"][rows:]], axis=0)
        for b, (qi, _) in enumerate(blocks):
            qoff = qi * blk if isinstance(qi, int) else pl.multiple_of(qi * blk, blk)
            c0, c1 = chains[2 * b], chains[2 * b + 1]
            r_scr[0, pl.ds(qoff, blk), :] = jnp.broadcast_to(c0["r"], (blk, blk))
            r_scr[1, pl.ds(qoff, blk), :] = jnp.broadcast_to(c1["r"], (blk, blk))
            acc_pair = jnp.where(lo_head, c0["acc"], c1["acc"])
            acc_scr[pl.ds(qoff, blk), :] = acc_pair
            o_ref[pl.ds(qoff, blk), :] = acc_pair.astype(BF16)

    first_tail = min(SB_WINDOW - 1, nq)
    n_static = SB_WINDOW + (nq - SB_WINDOW) % SB_GROUP
    q_group([(qi, min(qi + 1, SB_WINDOW)) for qi in range(n_static)])

    def q_loop(it, carry):
        q0 = n_static + it * SB_GROUP
        q_group([(q0 + j, SB_WINDOW) for j in range(SB_GROUP)])
        return carry

    lax.fori_loop(0, (nq - n_static) // SB_GROUP, q_loop, 0)


    @pl.when(jnp.max(r_scr[:, first_tail * blk:, :]) > SB_DEAD_LOG)
    def _tail():
        def q_tail(qi, carry):
            qoff = pl.multiple_of(qi * blk, blk)
            q = q_ref[pl.ds(qoff, blk), :]
            acc0 = acc_scr[pl.ds(qoff, blk), :]
            res = []
            for hh in range(2):
                qm = head_q(q, hh)
                r0 = r_scr[hh, pl.ds(qoff, blk), :]

                def cond(c):
                    return jnp.logical_and(c[0] >= 0, c[3] > SB_DEAD_LOG)

                def body(c, qm=qm):
                    off = pl.multiple_of(c[0] * blk, blk)
                    z = _dot_nt(qm, k_ref[pl.ds(off, blk), :])
                    todo = jnp.logical_or(row >= SB_FAR_ROWS, c[0] != qi - (SB_WINDOW - 1))
                    logit, rowsum = tile_scores(z, todo)
                    a = tile_weights(logit, todo, c[1])
                    r_new = c[1] + rowsum
                    acc_new = c[2] + _dot(a, v_ref[pl.ds(off, blk), :])
                    return c[0] - 1, r_new, acc_new, jnp.max(r_new)

                c0 = (qi - (SB_WINDOW - 1), r0, acc0, jnp.max(r0))
                res.append(lax.while_loop(cond, body, c0)[2])
            o_ref[pl.ds(qoff, blk), :] = jnp.where(lo_head, res[0], res[1]).astype(BF16)
            return carry

        lax.fori_loop(first_tail, nq, q_tail, 0)


def _stick_breaking(qb, kb, vb, batch, seq):
    view = lambda a: a.reshape(batch, seq, SB_WIDTH)
    spec = pl.BlockSpec((None, seq, LANES), lambda b, hp: (b, 0, hp))
    o = pl.pallas_call(
        _sb_kernel,
        grid=(batch, SB_WIDTH // LANES),
        in_specs=[spec, spec, spec],
        out_specs=spec,
        out_shape=jax.ShapeDtypeStruct((batch, seq, SB_WIDTH), BF16),
        scratch_shapes=[pltpu.VMEM((seq, LANES), F32), pltpu.VMEM((2, seq, LANES), F32)],
        compiler_params=_cparams(2),
        name="stick_breaking",
    )(view(qb), view(kb), view(vb))
    return o.reshape(batch * seq, SB_WIDTH)


def _merge_kernel(seq, x_ref, dil_ref, sb_ref,
                  uc_ref, uh_ref, g_ref, wg32_ref, bg_ref, wp32_ref, ps_ref,
                  wa32_ref, wb32_ref, wc32_ref, wo32_ref, out_ref,
                  h_scr, ub_scr, yc_scr, m_scr, wg_ref, wp_ref, wa_ref, wb_ref, wc_ref, wo_ref):
    i = pl.program_id(0)
    _cast_weights_once(i == 0, [(wg32_ref, wg_ref), (wp32_ref, wp_ref), (wa32_ref, wa_ref),
                                (wb32_ref, wb_ref), (wc32_ref, wc_ref), (wo32_ref, wo_ref)])
    tm, d = x_ref.shape
    x = x_ref[...]
    h_scr[...] = _rms(x, g_ref[...]).astype(BF16)
    cw = 256

    def gate_chunk(c):
        gates = []
        for br in range(N_BRANCH):
            gsl = slice(br * d + c * cw, br * d + (c + 1) * cw)
            gates.append(jax.nn.sigmoid(_dot(h_scr[...], wg_ref[:, gsl]) + bg_ref[:, gsl]))
        return gates

    dil = dil_ref[...]

    row0 = (i * tm) % seq
    halo = jnp.where(row0 == 0, 0.0, uh_ref[...])
    ub_scr[0:POOL_HALO, :] = halo
    ub_scr[POOL_HALO:POOL_HALO + tm, :] = uc_ref[...]
    t_in_seq = row0 + lax.broadcasted_iota(jnp.int32, (tm, 1), 0)
    for g, w in enumerate(POOL_WINDOWS):
        sl = slice(g * POOL_GROUP, (g + 1) * POOL_GROUP)
        u = ub_scr[POOL_HALO:POOL_HALO + tm, sl]
        acc = u
        for j in range(1, w):
            acc = acc + ub_scr[POOL_HALO - j:POOL_HALO - j + tm, sl]
        cnt = jnp.minimum(t_in_seq + 1, w).astype(F32)
        pooled = acc / cnt - u
        y = _dot(pooled.astype(BF16), wp_ref[g]) * ps_ref[:, sl]
        yc_scr[:, sl] = y.astype(BF16)

    sb = sb_ref[...]
    yc_in = yc_scr[...]
    for c in range(d // cw):
        sl = slice(c * cw, (c + 1) * cw)
        gates = gate_chunk(c)
        merged = (gates[0] * _dot(dil, wa_ref[:, sl]) + gates[1] * _dot(sb, wb_ref[:, sl])
                  + gates[2] * _dot(yc_in, wc_ref[:, sl]))
        m_scr[:, sl] = merged.astype(BF16)
    out_ref[...] = x + _dot(m_scr[...], wo_ref[...])


def _merge(x2d, dil_mix, sb_out, uc, seq, layer, g, w_gate, b_gate, w_pool, pool_scale,
           w_up_a, w_up_b, w_up_c, w_out):
    t, d = x2d.shape
    tm = MERGE_TILE
    row = lambda width: pl.BlockSpec((tm, width), lambda i: (i, 0))
    halo = pl.BlockSpec((POOL_HALO, POOL_WIDTH),
                        lambda i: (jnp.maximum(i * (tm // POOL_HALO) - 1, 0), 0))
    big = (w_gate, w_pool, w_up_a, w_up_b, w_up_c, w_out)
    lw = lambda w: _layer_weight_spec(w, layer)
    return pl.pallas_call(
        functools.partial(_merge_kernel, seq),
        grid=(t // tm,),
        in_specs=[row(d), row(DIL_OUT), row(SB_WIDTH), row(POOL_WIDTH), halo]
                 + [_const_spec(g.shape), lw(w_gate), _const_spec(b_gate.shape), lw(w_pool),
                    _const_spec(pool_scale.shape), lw(w_up_a), lw(w_up_b), lw(w_up_c),
                    lw(w_out)],
        out_specs=row(d),
        out_shape=jax.ShapeDtypeStruct((t, d), F32),
        scratch_shapes=[pltpu.VMEM((tm, d), BF16),
                        pltpu.VMEM((POOL_HALO + tm, POOL_WIDTH), F32),
                        pltpu.VMEM((tm, POOL_WIDTH), BF16),
                        pltpu.VMEM((tm, d), BF16)]
                       + [pltpu.VMEM(w.shape[1:], BF16) for w in big],
        compiler_params=_cparams(1),
        name="merge",
    )(x2d, dil_mix, sb_out, uc, uc, g, w_gate, b_gate, w_pool, pool_scale,
      w_up_a, w_up_b, w_up_c, w_out)


ROUTER_EXP_LANE0 = MOE_GROUPS
MOE_TILE = 1024
MOE_SUB = 256
MOE_SLOTS = 80


def _moe_kernel(final, x_ref, g_ref, wr_ref, br_ref, wg_ref, wu_ref, wd_ref,
                p_ref, gp_ref, wpg32_ref, wpi32_ref, gf_ref, out_ref,
                xn_scr, gate_scr, ghi_scr, glo_scr, sel_scr, selt_scr, acc_scr, ovf_smem,
                wpg_ref, wpi_ref):
    grp = pl.program_id(1)
    _cast_weights_once(jnp.logical_and(pl.program_id(0) == 0, grp == 0),
                       [(wpg32_ref, wpg_ref), (wpi32_ref, wpi_ref)])
    tm, d = x_ref.shape
    n_sub = tm // MOE_SUB
    lane_f = lax.broadcasted_iota(jnp.int32, (tm, LANES), 1).astype(F32)

    @pl.when(grp == 0)
    def _route():
        xn = _rms(x_ref[...], g_ref[...])
        a_hi = xn.astype(BF16)
        xn_scr[...] = a_hi
        a_lo = (xn - a_hi.astype(F32)).astype(BF16)
        r_hi = _dot(a_hi, wr_ref[...])
        logits = (r_hi[:, :LANES] + r_hi[:, LANES:] + _dot(a_lo, wr_ref[:, :LANES])
                  + br_ref[...])
        is_grp = lane_f < MOE_GROUPS
        lg = jnp.where(is_grp, logits, NEG_BIG)
        mg = jnp.max(lg, axis=-1, keepdims=True)
        p1 = 1.0 / jnp.sum(jnp.exp(lg - mg), axis=-1, keepdims=True)
        gi = jnp.min(jnp.where(lg == mg, lane_f, float(LANES)), axis=-1, keepdims=True)
        lane_grp = jnp.floor((lane_f - ROUTER_EXP_LANE0) * (1.0 / EXP_PER_GROUP))
        in_grp = jnp.logical_and(
            jnp.logical_and(lane_f >= ROUTER_EXP_LANE0, lane_f < ROUTER_EXP_LANE0 + N_EXPERTS),
            lane_grp == gi)
        le = jnp.where(in_grp, logits, NEG_BIG)
        v1 = jnp.max(le, axis=-1, keepdims=True)
        i1 = jnp.min(jnp.where(le == v1, lane_f, float(LANES)), axis=-1, keepdims=True)
        le2 = jnp.where(lane_f == i1, NEG_BIG, le)
        v2 = jnp.max(le2, axis=-1, keepdims=True)
        i2 = jnp.min(jnp.where(le2 == v2, lane_f, float(LANES)), axis=-1, keepdims=True)
        t = jnp.exp(v2 - v1)
        w1 = p1 / (1.0 + t)
        w2 = p1 * t / (1.0 + t)
        gate = jnp.where(lane_f == i1, w1, 0.0) + jnp.where(lane_f == i2, w2, 0.0)
        gate_scr[...] = gate
        g_hi = gate.astype(BF16)
        ghi_scr[...] = g_hi
        glo_scr[...] = (gate - g_hi.astype(F32)).astype(BF16)
        acc_scr[...] = jnp.zeros_like(acc_scr)

        onehot = jnp.where(lane_f == gi, 1.0, 0.0).astype(BF16)
        r = lax.broadcasted_iota(jnp.int32, (MOE_SUB, MOE_SUB), 0)
        c = lax.broadcasted_iota(jnp.int32, (MOE_SUB, MOE_SUB), 1)
        earlier = jnp.where(c < r, 1.0, 0.0).astype(BF16)
        fullest = jnp.zeros((1, LANES), F32)
        gi_b = jnp.broadcast_to(gi, (tm, LANES))
        for s in range(n_sub):
            rs = slice(s * MOE_SUB, (s + 1) * MOE_SUB)
            oh = onehot[rs, :]
            ahead = _dot(earlier, oh)
            slot = jnp.sum(jnp.where(oh > 0, ahead, 0.0), axis=-1, keepdims=True)
            sel_b = jnp.where(slot < LANES, slot + float(LANES) * gi_b[rs, :], -1.0)
            sel_scr[rs, :] = sel_b
            selt_scr[:, rs] = jnp.transpose(sel_b)[:8, :]
            fullest = jnp.maximum(fullest, jnp.sum(oh.astype(F32), axis=0, keepdims=True))
        lane1 = lax.broadcasted_iota(jnp.int32, (1, LANES), 1)
        for gidx in range(MOE_GROUPS):
            most = jnp.max(jnp.where(lane1 == gidx, fullest, 0.0))
            ovf_smem[gidx] = (most > MOE_SLOTS).astype(jnp.int32)

    def experts(rows, gates):
        lane_r = lax.broadcasted_iota(jnp.int32, gates.shape, 1)
        y = None
        for j in range(EXP_PER_GROUP):
            hg = _dot(rows, wg_ref[j])
            hu = _dot(rows, wu_ref[j])
            here = lane_r == ROUTER_EXP_LANE0 + EXP_PER_GROUP * grp + j
            gcol = jnp.sum(jnp.where(here, gates, 0.0), axis=-1, keepdims=True)
            hid = ((hg * jax.nn.sigmoid(hg)) * hu * gcol).astype(BF16)
            yj = _dot(hid, wd_ref[j])
            y = yj if y is None else y + yj
        return y

    key = (grp * LANES).astype(F32)

    @pl.when(ovf_smem[grp] == 0)
    def _compact():
        slot_rows = lax.broadcasted_iota(jnp.int32, (MOE_SLOTS, MOE_SUB), 0).astype(F32) + key
        slot_cols = lax.broadcasted_iota(jnp.int32, (MOE_SUB, LANES), 1).astype(F32) + key
        xs, gs = [], []
        for s in range(n_sub):
            rs = slice(s * MOE_SUB, (s + 1) * MOE_SUB)
            pick = jnp.where(selt_scr[0:1, rs] == slot_rows, 1.0, 0.0).astype(BF16)
            xs.append(_dot(pick, xn_scr[rs, :]).astype(BF16))
            gs.append(_dot(pick, ghi_scr[rs, :]) + _dot(pick, glo_scr[rs, :]))
        y = experts(jnp.concatenate(xs, axis=0), jnp.concatenate(gs, axis=0))
        y = jnp.concatenate([y, jnp.zeros((LANES - MOE_SLOTS, d), F32)], axis=0)
        y_hi = y.astype(BF16)
        y_lo = (y - y_hi.astype(F32)).astype(BF16)
        for s in range(n_sub):
            rs = slice(s * MOE_SUB, (s + 1) * MOE_SUB)
            ys = slice(s * MOE_SLOTS, s * MOE_SLOTS + LANES)
            place = jnp.where(sel_scr[rs, :] == slot_cols, 1.0, 0.0).astype(BF16)
            acc_scr[rs, :] += _dot(jnp.concatenate([place, place], axis=1),
                                   jnp.concatenate([y_hi[ys, :], y_lo[ys, :]], axis=0))

    @pl.when(ovf_smem[grp] != 0)
    def _dense():
        for s in range(n_sub):
            rs = slice(s * MOE_SUB, (s + 1) * MOE_SUB)
            acc_scr[rs, :] += experts(xn_scr[rs, :], gate_scr[rs, :])

    @pl.when(grp == MOE_GROUPS - 1)
    def _finish():
        for s in range(n_sub):
            rs = slice(s * MOE_SUB, (s + 1) * MOE_SUB)
            x = x_ref[rs, :] + acc_scr[rs, :]
            gate = jax.nn.sigmoid(_dot(_rms(x, gp_ref[...]).astype(BF16), wpg_ref[...]))
            y = x + _dot(p_ref[rs, :].astype(BF16), wpi_ref[...]) * gate
            if final:
                y = _rms(y, gf_ref[...])
            out_ref[rs, :] = y


def _moe_ple(x2d, g, w_router, b_router, w_exp_gate, w_exp_up, w_exp_down,
             p2d, layer, g_ple, w_ple_gate, w_ple_in, g_final, final):
    t, d = x2d.shape
    f = w_exp_gate.shape[-1]
    tm = MOE_TILE
    row = pl.BlockSpec((tm, d), lambda i, e: (i, 0))
    group_w = lambda a, b: pl.BlockSpec((EXP_PER_GROUP, a, b), lambda i, e: (e, 0, 0))
    return pl.pallas_call(
        functools.partial(_moe_kernel, final),
        grid=(t // tm, MOE_GROUPS),
        in_specs=[row, _const_spec((1, d)), _const_spec(w_router.shape),
                  _const_spec(b_router.shape),
                  group_w(d, f), group_w(d, f), group_w(f, d),
                  pl.BlockSpec((tm, p2d.shape[1]), lambda i, e: (i, 0)),
                  _const_spec((1, d)), _layer_weight_spec(w_ple_gate, layer),
                  _layer_weight_spec(w_ple_in, layer), _const_spec((1, d))],
        out_specs=row,
        out_shape=jax.ShapeDtypeStruct((t, d), F32),
        scratch_shapes=[pltpu.VMEM((tm, d), BF16), pltpu.VMEM((tm, LANES), F32),
                        pltpu.VMEM((tm, LANES), BF16), pltpu.VMEM((tm, LANES), BF16),
                        pltpu.VMEM((tm, LANES), F32), pltpu.VMEM((8, tm), F32),
                        pltpu.VMEM((tm, d), F32), pltpu.SMEM((MOE_GROUPS,), jnp.int32),
                        pltpu.VMEM(w_ple_gate.shape[1:], BF16),
                        pltpu.VMEM(w_ple_in.shape[1:], BF16)],
        compiler_params=_cparams(2),
        name="moe_ple",
    )(x2d, g, w_router, b_router, w_exp_gate, w_exp_up, w_exp_down,
      p2d, g_ple, w_ple_gate, w_ple_in, g_final)


def _router_weights(w_grp, b_grp, w_exp, b_exp):
    d = w_grp.shape[0]
    w = jnp.concatenate([w_grp, w_exp.transpose(1, 0, 2).reshape(d, N_EXPERTS)], axis=1)
    b = jnp.concatenate([b_grp, b_exp.reshape(N_EXPERTS)])
    pad = LANES - w.shape[1]
    w = jnp.pad(w, ((0, 0), (0, pad)))
    w_hi = w.astype(BF16)
    w_lo = (w - w_hi.astype(F32)).astype(BF16)
    return jnp.concatenate([w_hi, w_lo], axis=1), jnp.pad(b, (0, pad))[None, :]


def kernel(x, p, positions, norm_mix, w_in, w_gate, b_gate, w_pool, pool_scale, w_up_a, w_up_b, w_up_c, w_out, norm_moe, w_router_grp, b_router_grp, w_router_exp, b_router_exp, w_exp_gate, w_exp_up, w_exp_down, norm_ple, w_ple_in, w_ple_gate, norm_final):
    batch, seq, d = x.shape
    depth = w_in.shape[0]
    t = batch * seq
    x2d = x.reshape(t, d)
    cos_t, sin_t = _rope_tables(positions)
    bf = lambda a: a.astype(BF16)
    for i in range(depth):
        qa, ka, va, qb, kb, vb, uc = _inproj(x2d, norm_mix[i][None, :], w_in, i, cos_t, sin_t)
        dil_mix = _dilated(qa, ka, va, batch, seq)
        sb_out = _stick_breaking(qb, kb, vb, batch, seq)
        x2d = _merge(x2d, dil_mix, sb_out, uc, seq, i, norm_mix[i][None, :], w_gate,
                     b_gate[i][None, :], w_pool, pool_scale[i][None, :],
                     w_up_a, w_up_b, w_up_c, w_out)
        w_r, b_r = _router_weights(w_router_grp[i], b_router_grp[i],
                                   w_router_exp[i], b_router_exp[i])
        x2d = _moe_ple(x2d, norm_moe[i][None, :], w_r, b_r, bf(w_exp_gate[i]),
                       bf(w_exp_up[i]), bf(w_exp_down[i]), p[i].reshape(t, -1), i,
                       norm_ple[i][None, :], w_ple_gate, w_ple_in,
                       norm_final[None, :], i == depth - 1)
    return x2d.reshape(batch, seq, d)
```

```python
import functools

import jax
import jax.numpy as jnp
from jax import lax
from jax.experimental import pallas as pl
from jax.experimental.pallas import tpu as pltpu

F32 = jnp.float32
BF16 = jnp.bfloat16

HEAD_DIM = 64
ROPE_THETA = 10000.0
DIL_CONFIGS = ((128, 1), (512, 4), (2048, 16))
DIL_HEADS = 4
DIL_OUT = DIL_HEADS * HEAD_DIM
DIL_WIDTH = len(DIL_CONFIGS) * DIL_OUT
DIL_BLOCK = 128
DIL_CHUNK = 2048
DIL_SKEW = 3
INPROJ_TILE = 512
MERGE_TILE = 512
DIL_PERMUTED = (16,)
SB_HEADS = 8
SB_WIDTH = SB_HEADS * HEAD_DIM
SB_BLOCK = 128
POOL_WINDOWS = (2, 4, 8, 16)
POOL_GROUP = 128
POOL_WIDTH = len(POOL_WINDOWS) * POOL_GROUP
POOL_HALO = 16
N_BRANCH = 3
MOE_GROUPS = 4
EXP_PER_GROUP = 4
N_EXPERTS = MOE_GROUPS * EXP_PER_GROUP
EPS = 1e-6

LANES = 128
NEG_BIG = -1e30
SB_DEAD_LOG = -105.0
SB_WINDOW = 3
SB_GROUP = 28
SB_SKEW = 8
SB_FAR_ROWS = 64
VMEM_LIMIT = 56 * 1024 * 1024


def _cparams(n_axes):
    return pltpu.CompilerParams(
        dimension_semantics=("arbitrary",) * n_axes, vmem_limit_bytes=VMEM_LIMIT)


def _rms(x, g):
    ms = jnp.mean(x * x, axis=-1, keepdims=True)
    return x * lax.rsqrt(ms + EPS) * g


def _dot(a, b):
    return jnp.dot(a, b, preferred_element_type=F32)


def _dot_nt(a, b):
    return lax.dot_general(a, b, (((1,), (1,)), ((), ())), preferred_element_type=F32)


def _const_spec(shape):
    zeros = (0,) * len(shape)
    return pl.BlockSpec(shape, lambda *_: zeros)


def _layer_weight_spec(w, layer):
    tail = (0,) * (w.ndim - 1)
    return pl.BlockSpec((None,) + w.shape[1:], lambda *_: (layer,) + tail,
                        pipeline_mode=pl.Buffered(1))


def _cast_weights_once(first_step, pairs):
    @pl.when(first_step)
    def _cast():
        for src, dst in pairs:
            if len(src.shape) == 3:
                for k in range(src.shape[0]):
                    dst[k] = src[k].astype(BF16)
            else:
                step = 2 * LANES
                for c in range(0, src.shape[1], step):
                    dst[:, c:c + step] = src[:, c:c + step].astype(BF16)


def _rope_table_kernel(pos_ref, inv_ref, sgn_ref, cos_ref, sin_ref):
    ang = inv_ref[...] * pos_ref[...].astype(F32)
    reps = LANES // ang.shape[0]
    cos_ref[...] = jnp.transpose(jnp.concatenate([jnp.cos(ang)] * reps, axis=0))
    sin_ref[...] = jnp.transpose(jnp.concatenate([jnp.sin(ang)] * reps, axis=0) * sgn_ref[...])


def _rope_tables(positions):
    t = positions.size
    half = HEAD_DIM // 2
    inv_col = (ROPE_THETA ** (-jnp.arange(half, dtype=F32) / half))[:, None]
    sgn_col = jnp.tile(jnp.concatenate([-jnp.ones(half, F32), jnp.ones(half, F32)]),
                       LANES // HEAD_DIM)[:, None]
    tm = 2048
    return pl.pallas_call(
        _rope_table_kernel,
        grid=(t // tm,),
        in_specs=[pl.BlockSpec((None, 1, tm), lambda i: (i, 0, 0)),
                  _const_spec((half, 1)), _const_spec((LANES, 1))],
        out_specs=[pl.BlockSpec((tm, LANES), lambda i: (i, 0))] * 2,
        out_shape=[jax.ShapeDtypeStruct((t, LANES), F32)] * 2,
        compiler_params=_cparams(1),
        name="rope_table",
    )(positions.reshape(t // tm, 1, tm), inv_col, sgn_col)


def _inproj_kernel(x_ref, g_ref, w32_ref, cos_ref, sin_ref, eg32_ref, eu32_ref, ed32_ref,
                   qa_ref, ka_ref, va_ref, qb_ref, kb_ref, vb_ref, uc_ref,
                   eg_ref, eu_ref, ed_ref, h_scr, perm_scr, w_ref):
    _cast_weights_once(pl.program_id(0) == 0, [(w32_ref, w_ref)])
    eg_ref[...] = eg32_ref[...].astype(BF16)
    eu_ref[...] = eu32_ref[...].astype(BF16)
    ed_ref[...] = ed32_ref[...].astype(BF16)
    h_scr[...] = _rms(x_ref[...], g_ref[...]).astype(BF16)
    cos = cos_ref[...]
    sin = sin_ref[...]
    tm = cos.shape[0]
    lane = lax.broadcasted_iota(jnp.int32, (tm, LANES), 1)
    first_half = (lane & (HEAD_DIM // 2)) == 0
    scale = HEAD_DIM ** -0.5

    def rope(t):
        partner = jnp.where(first_half, pltpu.roll(t, LANES - HEAD_DIM // 2, 1),
                            pltpu.roll(t, HEAD_DIM // 2, 1))
        return t * cos + partner * sin

    def store_dilated(ref, which, c, val):
        sl = slice(c * LANES, (c + 1) * LANES)
        dil = DIL_CONFIGS[c // (DIL_OUT // LANES)][1]
        if dil not in DIL_PERMUTED:
            ref[:, sl] = val
            return
        per = tm // dil
        perm_scr[which, c % (DIL_OUT // LANES)] = val
        for r in range(dil):
            ref[r * per:(r + 1) * per, sl] = perm_scr[which, c % (DIL_OUT // LANES),
                                                      pl.ds(r, per, stride=dil), :]

    h = h_scr[...]
    col = 0
    pq = _dot(h, w_ref[:, col:col + DIL_WIDTH])
    for c in range(DIL_WIDTH // LANES):
        store_dilated(qa_ref, 0, c, rope(pq[:, c * LANES:(c + 1) * LANES]) * scale)
    col += DIL_WIDTH
    pk = _dot(h, w_ref[:, col:col + DIL_WIDTH])
    for c in range(DIL_WIDTH // LANES):
        store_dilated(ka_ref, 1, c, rope(pk[:, c * LANES:(c + 1) * LANES]))
    col += DIL_WIDTH
    pv = _dot(h, w_ref[:, col:col + DIL_WIDTH])
    for c in range(DIL_WIDTH // LANES):
        store_dilated(va_ref, 2, c, pv[:, c * LANES:(c + 1) * LANES])
    col += DIL_WIDTH
    qb_ref[...] = (_dot(h, w_ref[:, col:col + SB_WIDTH]) * scale).astype(BF16)
    col += SB_WIDTH
    kb_ref[...] = _dot(h, w_ref[:, col:col + SB_WIDTH]).astype(BF16)
    col += SB_WIDTH
    vb_ref[...] = _dot(h, w_ref[:, col:col + SB_WIDTH]).astype(BF16)
    col += SB_WIDTH
    uc_ref[...] = _dot(h, w_ref[:, col:col + POOL_WIDTH])


def _inproj(x2d, g, w_in, layer, cos_t, sin_t, expert_ws):
    t, d = x2d.shape
    tm = INPROJ_TILE
    steps = t // tm
    row = lambda width: pl.BlockSpec((tm, width), lambda i: (i, 0))
    widths = (DIL_WIDTH,) * 3 + (SB_WIDTH,) * 3 + (POOL_WIDTH,)
    dtypes = (F32,) * 3 + (BF16,) * 3 + (F32,)
    flat = [w.reshape(w.shape[0], -1, w.shape[-1]) for w in expert_ws]
    slabs = [(w.shape[1] // steps, w.shape[2]) for w in flat]
    outs = pl.pallas_call(
        _inproj_kernel,
        grid=(steps,),
        in_specs=[row(d), _const_spec((1, d)), _layer_weight_spec(w_in, layer),
                  row(LANES), row(LANES)]
                 + [pl.BlockSpec((None,) + s, lambda i: (layer, i, 0)) for s in slabs],
        out_specs=[row(w) for w in widths]
                  + [pl.BlockSpec(s, lambda i: (i, 0)) for s in slabs],
        out_shape=[jax.ShapeDtypeStruct((t, w), dt) for w, dt in zip(widths, dtypes)]
                  + [jax.ShapeDtypeStruct(w.shape[1:], BF16) for w in flat],
        scratch_shapes=[pltpu.VMEM((tm, d), BF16),
                        pltpu.VMEM((3, DIL_OUT // LANES, tm, LANES), F32),
                        pltpu.VMEM(w_in.shape[1:], BF16)],
        compiler_params=_cparams(1),
        name="in_proj",
    )(x2d, g, w_in, cos_t, sin_t, *flat)
    experts_bf16 = [o.reshape(w.shape[1:]) for o, w in zip(outs[len(widths):], expert_ws)]
    return outs[:len(widths)], experts_bf16


def _dil_kernel(*refs):
    n_grp = len(DIL_CONFIGS)
    out_ref, o_scr, l_scr = refs[5 * n_grp:]
    for g, (_, dil) in enumerate(DIL_CONFIGS):
        _dil_group(dil, *refs[5 * g:5 * g + 5], o_scr.at[g], l_scr.at[g])
    lses = [l_scr[g] for g in range(n_grp)]
    m = functools.reduce(jnp.maximum, lses)
    es = [jnp.exp(l - m) for l in lses]
    num = sum(e * o_scr[g] for g, e in enumerate(es))
    out_ref[...] = (num / sum(es)).astype(BF16)


def _dil_group(dil, q_ref, kp_ref, kc_ref, vp_ref, vc_ref, o_ref, lse_ref):
    has_prev = pl.program_id(1) > 0
    n = DIL_BLOCK
    n_blocks = q_ref.shape[0] // (n * dil)
    qi = lax.broadcasted_iota(jnp.int32, (n, n), 0)
    kj = lax.broadcasted_iota(jnp.int32, (n, n), 1)
    mask_prev = kj >= qi
    mask_prev_first = jnp.logical_and(mask_prev, has_prev)
    mask_cur = kj <= qi
    lo_head = kj < HEAD_DIM
    ones = jnp.ones((2 * n, LANES), BF16)

    def rows(ref, r, j):
        if dil in DIL_PERMUTED:
            per = INPROJ_TILE // dil
            pieces = [ref[ti * INPROJ_TILE + r * per:ti * INPROJ_TILE + (r + 1) * per, :]
                      for ti in range(j * (n // per), (j + 1) * (n // per))]
            return jnp.concatenate(pieces, axis=0).astype(BF16)
        start = j * n * dil + r
        return ref[pl.ds(start, n, stride=dil) if dil > 1 else pl.ds(start, n), :].astype(BF16)

    units = {}

    def unit(r, j):
        if (r, j) not in units:
            kc, vc = rows(kc_ref, r, j), rows(vc_ref, r, j)
            if j == 0:
                kp, vp = rows(kp_ref, r, 0), rows(vp_ref, r, 0)
            else:
                kp, vp = units[(r, j - 1)]["kc"], units[(r, j - 1)]["vc"]
            vo = jnp.concatenate([jnp.concatenate([vp, vc], axis=0), ones], axis=1)
            units[(r, j)] = dict(q=rows(q_ref, r, j), kc=kc, vc=vc, kp=kp, vo=vo,
                                 pm=mask_prev_first if j == 0 else mask_prev,
                                 start=j * n * dil + r)
        return units[(r, j)]

    def stage_scores(c):
        u = unit(c["r"], c["j"])
        sel = lo_head if c["hh"] == 0 else jnp.logical_not(lo_head)
        qm = jnp.where(sel, u["q"], jnp.zeros((n, LANES), BF16))
        c["sp"] = _dot_nt(qm, u["kp"])
        c["sc"] = _dot_nt(qm, u["kc"])

    def stage_max(c):
        u = unit(c["r"], c["j"])
        c["sp"] = jnp.where(u["pm"], c["sp"], NEG_BIG)
        c["sc"] = jnp.where(mask_cur, c["sc"], NEG_BIG)
        c["m"] = jnp.max(jnp.maximum(c["sp"], c["sc"]), axis=-1, keepdims=True)

    def stage_pv(c):
        u = unit(c["r"], c["j"])
        ep = jnp.exp(c.pop("sp") - c["m"]).astype(BF16)
        ec = jnp.exp(c.pop("sc") - c["m"]).astype(BF16)
        res = _dot(jnp.concatenate([ep, ec], axis=1), u["vo"])
        c["acc"] = res[:, :LANES]
        c["den"] = res[:, LANES:]

    def stage_out(c, pairs):
        c["o"] = c.pop("acc") / c["den"]
        c["lse"] = c["m"] + jnp.log(c.pop("den"))
        key = (c["r"], c["j"])
        pairs.setdefault(key, []).append(c)
        if len(pairs[key]) == 2:
            c0, c1 = pairs.pop(key)
            u = unit(c["r"], c["j"])
            idx = pl.ds(u["start"], n, stride=dil) if dil > 1 else pl.ds(u["start"], n)
            o_ref[idx, :] = jnp.where(lo_head, c0["o"], c1["o"])
            lse_ref[idx, :] = jnp.where(lo_head, c0["lse"], c1["lse"])

    chains = [dict(r=r, j=j, hh=hh) for r in range(dil) for j in range(n_blocks)
              for hh in range(2)]
    stages = (stage_scores, stage_max, stage_pv)
    pairs = {}
    total = len(chains)
    for step in range(total + len(stages) * DIL_SKEW):
        for s, fn in enumerate(stages):
            i = step - s * DIL_SKEW
            if 0 <= i < total:
                fn(chains[i])
        i = step - len(stages) * DIL_SKEW
        if 0 <= i < total:
            stage_out(chains[i], pairs)


def _dilated(qa, ka, va, batch, seq):
    assert seq % DIL_CHUNK == 0
    nc = seq // DIL_CHUNK
    t = batch * seq
    n_pairs = DIL_OUT // LANES
    in_specs = []
    for g, (window, dil) in enumerate(DIL_CONFIGS):
        assert window // dil == DIL_BLOCK
        hist = DIL_BLOCK * dil
        cur = pl.BlockSpec((DIL_CHUNK, LANES),
                           lambda b, c, pr, g=g: (b * nc + c, g * n_pairs + pr))
        prev = pl.BlockSpec(
            (hist, LANES),
            lambda b, c, pr, g=g, hist=hist: (
                jnp.maximum((b * nc + c) * (DIL_CHUNK // hist) - 1, 0), g * n_pairs + pr))
        in_specs += [cur, prev, cur, prev, cur]
    n_grp = len(DIL_CONFIGS)
    return pl.pallas_call(
        _dil_kernel,
        grid=(batch, nc, n_pairs),
        in_specs=in_specs,
        out_specs=pl.BlockSpec((DIL_CHUNK, LANES), lambda b, c, pr: (b * nc + c, pr)),
        out_shape=jax.ShapeDtypeStruct((t, DIL_OUT), BF16),
        scratch_shapes=[pltpu.VMEM((n_grp, DIL_CHUNK, LANES), F32)] * 2,
        compiler_params=_cparams(3),
        name="dilated",
    )(*([qa, ka, ka, va, va] * n_grp))


def _sb_kernel(q_ref, k_ref, v_ref, o_ref, acc_scr, r_scr):
    blk = SB_BLOCK
    nq = q_ref.shape[0] // blk
    row = lax.broadcasted_iota(jnp.int32, (blk, blk), 0)
    col = lax.broadcasted_iota(jnp.int32, (blk, blk), 1)
    tri = col < row
    lo_head = col < HEAD_DIM
    cum_mat = jnp.where(row > col, 1.0, 0.0).astype(BF16)

    def head_q(q, hh):
        sel = lo_head if hh == 0 else jnp.logical_not(lo_head)
        return jnp.where(sel, q, jnp.zeros_like(q))

    def tile_scores(z, mask):
        sp = jnp.maximum(z, 0.0) + jnp.log(1.0 + jnp.exp(-jnp.abs(z)))
        lom = -sp
        if mask is not None:
            lom = jnp.where(mask, lom, 0.0)
        hi = lom.astype(BF16)
        lo = (lom - hi.astype(F32)).astype(BF16)
        after = _dot(hi, cum_mat) + _dot(lo, cum_mat)
        return z - sp + after, jnp.sum(lom, axis=-1, keepdims=True)

    def tile_weights(logit, mask, r_acc):
        a = jnp.exp(logit + r_acc)
        if mask is not None:
            a = jnp.where(mask, a, 0.0)
        return a.astype(BF16)

    def q_group(blocks):
        chains = []
        for qi, n_tiles in blocks:
            qoff = qi * blk
            koff = (qi - (n_tiles - 1)) * blk
            if not isinstance(qi, int):
                qoff, koff = pl.multiple_of(qoff, blk), pl.multiple_of(koff, blk)
            q = q_ref[pl.ds(qoff, blk), :]
            kw = k_ref[pl.ds(koff, n_tiles * blk), :]
            vw = v_ref[pl.ds(koff, n_tiles * blk), :]
            for hh in range(2):
                chains.append(dict(z=_dot_nt(head_q(q, hh), kw), vw=vw, n=n_tiles,
                                   r=jnp.zeros((blk, 1), F32), acc=jnp.zeros((blk, blk), F32)))
        max_tiles = max(c["n"] for c in chains)
        tiles = [(c, c["n"] - 1 - j) for j in range(max_tiles) for c in chains if j < c["n"]]
        pending = []
        for step in range(len(tiles) + SB_SKEW):
            if step < len(tiles):
                c, w = tiles[step]
                mask = tri if w == c["n"] - 1 else None
                rows = SB_FAR_ROWS if (w == 0 and c["n"] == SB_WINDOW) else blk
                logit, rowsum = tile_scores(c["z"][:rows, w * blk:(w + 1) * blk], mask)
                pending.append((c, w, mask, logit, rowsum, rows))
            if step >= SB_SKEW:
                c, w, mask, logit, rowsum, rows = pending[step - SB_SKEW]
                a = tile_weights(logit, mask, c["r"][:rows])
                pv = _dot(a, c["vw"][w * blk:(w + 1) * blk, :])
                if rows == blk:
                    c["acc"] = c["acc"] + pv
                    c["r"] = c["r"] + rowsum
                else:
                    c["acc"] = jnp.concatenate([c["acc"][:rows] + pv, c["acc"][rows:]], axis=0)
                    c["r"] = jnp.concatenate([c["r"][:rows] + rowsum, c["r"][rows:]], axis=0)
        for b, (qi, _) in enumerate(blocks):
            qoff = qi * blk if isinstance(qi, int) else pl.multiple_of(qi * blk, blk)
            c0, c1 = chains[2 * b], chains[2 * b + 1]
            r_scr[0, pl.ds(qoff, blk), :] = jnp.broadcast_to(c0["r"], (blk, blk))
            r_scr[1, pl.ds(qoff, blk), :] = jnp.broadcast_to(c1["r"], (blk, blk))
            acc_pair = jnp.where(lo_head, c0["acc"], c1["acc"])
            acc_scr[pl.ds(qoff, blk), :] = acc_pair
            o_ref[pl.ds(qoff, blk), :] = acc_pair.astype(BF16)

    first_tail = min(SB_WINDOW - 1, nq)
    n_static = SB_WINDOW + (nq - SB_WINDOW) % SB_GROUP
    q_group([(qi, min(qi + 1, SB_WINDOW)) for qi in range(n_static)])

    def q_loop(it, carry):
        q0 = n_static + it * SB_GROUP
        q_group([(q0 + j, SB_WINDOW) for j in range(SB_GROUP)])
        return carry

    lax.fori_loop(0, (nq - n_static) // SB_GROUP, q_loop, 0)


    @pl.when(jnp.max(r_scr[:, first_tail * blk:, :]) > SB_DEAD_LOG)
    def _tail():
        def q_tail(qi, carry):
            qoff = pl.multiple_of(qi * blk, blk)
            q = q_ref[pl.ds(qoff, blk), :]
            acc0 = acc_scr[pl.ds(qoff, blk), :]
            res = []
            for hh in range(2):
                qm = head_q(q, hh)
                r0 = r_scr[hh, pl.ds(qoff, blk), :]

                def cond(c):
                    return jnp.logical_and(c[0] >= 0, c[3] > SB_DEAD_LOG)

                def body(c, qm=qm):
                    off = pl.multiple_of(c[0] * blk, blk)
                    z = _dot_nt(qm, k_ref[pl.ds(off, blk), :])
                    todo = jnp.logical_or(row >= SB_FAR_ROWS, c[0] != qi - (SB_WINDOW - 1))
                    logit, rowsum = tile_scores(z, todo)
                    a = tile_weights(logit, todo, c[1])
                    r_new = c[1] + rowsum
                    acc_new = c[2] + _dot(a, v_ref[pl.ds(off, blk), :])
                    return c[0] - 1, r_new, acc_new, jnp.max(r_new)

                c0 = (qi - (SB_WINDOW - 1), r0, acc0, jnp.max(r0))
                res.append(lax.while_loop(cond, body, c0)[2])
            o_ref[pl.ds(qoff, blk), :] = jnp.where(lo_head, res[0], res[1]).astype(BF16)
            return carry

        lax.fori_loop(first_tail, nq, q_tail, 0)


def _stick_breaking(qb, kb, vb, batch, seq):
    view = lambda a: a.reshape(batch, seq, SB_WIDTH)
    spec = pl.BlockSpec((None, seq, LANES), lambda b, hp: (b, 0, hp))
    o = pl.pallas_call(
        _sb_kernel,
        grid=(batch, SB_WIDTH // LANES),
        in_specs=[spec, spec, spec],
        out_specs=spec,
        out_shape=jax.ShapeDtypeStruct((batch, seq, SB_WIDTH), BF16),
        scratch_shapes=[pltpu.VMEM((seq, LANES), F32), pltpu.VMEM((2, seq, LANES), F32)],
        compiler_params=_cparams(2),
        name="stick_breaking",
    )(view(qb), view(kb), view(vb))
    return o.reshape(batch * seq, SB_WIDTH)


def _merge_kernel(seq, x_ref, dil_ref, sb_ref,
                  uc_ref, uh_ref, g_ref, wg32_ref, bg_ref, wp32_ref, ps_ref,
                  wa32_ref, wb32_ref, wc32_ref, wo32_ref, out_ref,
                  h_scr, ub_scr, yc_scr, m_scr, wg_ref, wp_ref, wa_ref, wb_ref, wc_ref, wo_ref):
    i = pl.program_id(0)
    _cast_weights_once(i == 0, [(wg32_ref, wg_ref), (wp32_ref, wp_ref), (wa32_ref, wa_ref),
                                (wb32_ref, wb_ref), (wc32_ref, wc_ref), (wo32_ref, wo_ref)])
    tm, d = x_ref.shape
    x = x_ref[...]
    h_scr[...] = _rms(x, g_ref[...]).astype(BF16)
    cw = 256

    def gate_chunk(c):
        gates = []
        for br in range(N_BRANCH):
            gsl = slice(br * d + c * cw, br * d + (c + 1) * cw)
            gates.append(jax.nn.sigmoid(_dot(h_scr[...], wg_ref[:, gsl]) + bg_ref[:, gsl]))
        return gates

    dil = dil_ref[...]

    row0 = (i * tm) % seq
    halo = jnp.where(row0 == 0, 0.0, uh_ref[...])
    ub_scr[0:POOL_HALO, :] = halo
    ub_scr[POOL_HALO:POOL_HALO + tm, :] = uc_ref[...]
    t_in_seq = row0 + lax.broadcasted_iota(jnp.int32, (tm, 1), 0)
    for g, w in enumerate(POOL_WINDOWS):
        sl = slice(g * POOL_GROUP, (g + 1) * POOL_GROUP)
        u = ub_scr[POOL_HALO:POOL_HALO + tm, sl]
        acc = u
        for j in range(1, w):
            acc = acc + ub_scr[POOL_HALO - j:POOL_HALO - j + tm, sl]
        cnt = jnp.minimum(t_in_seq + 1, w).astype(F32)
        pooled = acc / cnt - u
        y = _dot(pooled.astype(BF16), wp_ref[g]) * ps_ref[:, sl]
        yc_scr[:, sl] = y.astype(BF16)

    sb = sb_ref[...]
    yc_in = yc_scr[...]
    for c in range(d // cw):
        sl = slice(c * cw, (c + 1) * cw)
        gates = gate_chunk(c)
        merged = (gates[0] * _dot(dil, wa_ref[:, sl]) + gates[1] * _dot(sb, wb_ref[:, sl])
                  + gates[2] * _dot(yc_in, wc_ref[:, sl]))
        m_scr[:, sl] = merged.astype(BF16)
    out_ref[...] = x + _dot(m_scr[...], wo_ref[...])


def _merge(x2d, dil_mix, sb_out, uc, seq, layer, g, w_gate, b_gate, w_pool, pool_scale,
           w_up_a, w_up_b, w_up_c, w_out):
    t, d = x2d.shape
    tm = MERGE_TILE
    row = lambda width: pl.BlockSpec((tm, width), lambda i: (i, 0))
    halo = pl.BlockSpec((POOL_HALO, POOL_WIDTH),
                        lambda i: (jnp.maximum(i * (tm // POOL_HALO) - 1, 0), 0))
    big = (w_gate, w_pool, w_up_a, w_up_b, w_up_c, w_out)
    lw = lambda w: _layer_weight_spec(w, layer)
    return pl.pallas_call(
        functools.partial(_merge_kernel, seq),
        grid=(t // tm,),
        in_specs=[row(d), row(DIL_OUT), row(SB_WIDTH), row(POOL_WIDTH), halo]
                 + [_const_spec(g.shape), lw(w_gate), _const_spec(b_gate.shape), lw(w_pool),
                    _const_spec(pool_scale.shape), lw(w_up_a), lw(w_up_b), lw(w_up_c),
                    lw(w_out)],
        out_specs=row(d),
        out_shape=jax.ShapeDtypeStruct((t, d), F32),
        scratch_shapes=[pltpu.VMEM((tm, d), BF16),
                        pltpu.VMEM((POOL_HALO + tm, POOL_WIDTH), F32),
                        pltpu.VMEM((tm, POOL_WIDTH), BF16),
                        pltpu.VMEM((tm, d), BF16)]
                       + [pltpu.VMEM(w.shape[1:], BF16) for w in big],
        compiler_params=_cparams(1),
        name="merge",
    )(x2d, dil_mix, sb_out, uc, uc, g, w_gate, b_gate, w_pool, pool_scale,
      w_up_a, w_up_b, w_up_c, w_out)


ROUTER_EXP_LANE0 = MOE_GROUPS
MOE_TILE = 1024
MOE_SUB = 256
MOE_SLOTS = 80


def _moe_kernel(final, x_ref, g_ref, wr_ref, br_ref, wg_ref, wu_ref, wd_ref,
                p_ref, gp_ref, wpg32_ref, wpi32_ref, gf_ref, out_ref,
                xn_scr, gate_scr, ghi_scr, glo_scr, sel_scr, selt_scr, acc_scr, ovf_smem,
                wpg_ref, wpi_ref):
    grp = pl.program_id(1)
    _cast_weights_once(jnp.logical_and(pl.program_id(0) == 0, grp == 0),
                       [(wpg32_ref, wpg_ref), (wpi32_ref, wpi_ref)])
    tm, d = x_ref.shape
    n_sub = tm // MOE_SUB
    lane_f = lax.broadcasted_iota(jnp.int32, (tm, LANES), 1).astype(F32)

    @pl.when(grp == 0)
    def _route():
        xn = _rms(x_ref[...], g_ref[...])
        a_hi = xn.astype(BF16)
        xn_scr[...] = a_hi
        a_lo = (xn - a_hi.astype(F32)).astype(BF16)
        r_hi = _dot(a_hi, wr_ref[...])
        logits = (r_hi[:, :LANES] + r_hi[:, LANES:] + _dot(a_lo, wr_ref[:, :LANES])
                  + br_ref[...])
        is_grp = lane_f < MOE_GROUPS
        lg = jnp.where(is_grp, logits, NEG_BIG)
        mg = jnp.max(lg, axis=-1, keepdims=True)
        p1 = 1.0 / jnp.sum(jnp.exp(lg - mg), axis=-1, keepdims=True)
        gi = jnp.min(jnp.where(lg == mg, lane_f, float(LANES)), axis=-1, keepdims=True)
        lane_grp = jnp.floor((lane_f - ROUTER_EXP_LANE0) * (1.0 / EXP_PER_GROUP))
        in_grp = jnp.logical_and(
            jnp.logical_and(lane_f >= ROUTER_EXP_LANE0, lane_f < ROUTER_EXP_LANE0 + N_EXPERTS),
            lane_grp == gi)
        le = jnp.where(in_grp, logits, NEG_BIG)
        v1 = jnp.max(le, axis=-1, keepdims=True)
        i1 = jnp.min(jnp.where(le == v1, lane_f, float(LANES)), axis=-1, keepdims=True)
        le2 = jnp.where(lane_f == i1, NEG_BIG, le)
        v2 = jnp.max(le2, axis=-1, keepdims=True)
        i2 = jnp.min(jnp.where(le2 == v2, lane_f, float(LANES)), axis=-1, keepdims=True)
        t = jnp.exp(v2 - v1)
        w1 = p1 / (1.0 + t)
        w2 = p1 * t / (1.0 + t)
        gate = jnp.where(lane_f == i1, w1, 0.0) + jnp.where(lane_f == i2, w2, 0.0)
        gate_scr[...] = gate
        g_hi = gate.astype(BF16)
        ghi_scr[...] = g_hi
        glo_scr[...] = (gate - g_hi.astype(F32)).astype(BF16)
        acc_scr[...] = jnp.zeros_like(acc_scr)

        onehot = jnp.where(lane_f == gi, 1.0, 0.0).astype(BF16)
        r = lax.broadcasted_iota(jnp.int32, (MOE_SUB, MOE_SUB), 0)
        c = lax.broadcasted_iota(jnp.int32, (MOE_SUB, MOE_SUB), 1)
        earlier = jnp.where(c < r, 1.0, 0.0).astype(BF16)
        fullest = jnp.zeros((1, LANES), F32)
        gi_b = jnp.broadcast_to(gi, (tm, LANES))
        for s in range(n_sub):
            rs = slice(s * MOE_SUB, (s + 1) * MOE_SUB)
            oh = onehot[rs, :]
            ahead = _dot(earlier, oh)
            slot = jnp.sum(jnp.where(oh > 0, ahead, 0.0), axis=-1, keepdims=True)
            sel_b = jnp.where(slot < LANES, slot + float(LANES) * gi_b[rs, :], -1.0)
            sel_scr[rs, :] = sel_b
            selt_scr[:, rs] = jnp.transpose(sel_b)[:8, :]
            fullest = jnp.maximum(fullest, jnp.sum(oh.astype(F32), axis=0, keepdims=True))
        lane1 = lax.broadcasted_iota(jnp.int32, (1, LANES), 1)
        for gidx in range(MOE_GROUPS):
            most = jnp.max(jnp.where(lane1 == gidx, fullest, 0.0))
            ovf_smem[gidx] = (most > MOE_SLOTS).astype(jnp.int32)

    def experts(rows, gates):
        lane_r = lax.broadcasted_iota(jnp.int32, gates.shape, 1)
        y = None
        for j in range(EXP_PER_GROUP):
            hg = _dot(rows, wg_ref[j])
            hu = _dot(rows, wu_ref[j])
            here = lane_r == ROUTER_EXP_LANE0 + EXP_PER_GROUP * grp + j
            gcol = jnp.sum(jnp.where(here, gates, 0.0), axis=-1, keepdims=True)
            hid = ((hg * jax.nn.sigmoid(hg)) * hu * gcol).astype(BF16)
            yj = _dot(hid, wd_ref[j])
            y = yj if y is None else y + yj
        return y

    key = (grp * LANES).astype(F32)

    @pl.when(ovf_smem[grp] == 0)
    def _compact():
        slot_rows = lax.broadcasted_iota(jnp.int32, (MOE_SLOTS, MOE_SUB), 0).astype(F32) + key
        slot_cols = lax.broadcasted_iota(jnp.int32, (MOE_SUB, LANES), 1).astype(F32) + key
        xs, gs = [], []
        for s in range(n_sub):
            rs = slice(s * MOE_SUB, (s + 1) * MOE_SUB)
            pick = jnp.where(selt_scr[0:1, rs] == slot_rows, 1.0, 0.0).astype(BF16)
            xs.append(_dot(pick, xn_scr[rs, :]).astype(BF16))
            gs.append(_dot(pick, ghi_scr[rs, :]) + _dot(pick, glo_scr[rs, :]))
        y = experts(jnp.concatenate(xs, axis=0), jnp.concatenate(gs, axis=0))
        y = jnp.concatenate([y, jnp.zeros((LANES - MOE_SLOTS, d), F32)], axis=0)
        y_hi = y.astype(BF16)
        y_lo = (y - y_hi.astype(F32)).astype(BF16)
        for s in range(n_sub):
            rs = slice(s * MOE_SUB, (s + 1) * MOE_SUB)
            ys = slice(s * MOE_SLOTS, s * MOE_SLOTS + LANES)
            place = jnp.where(sel_scr[rs, :] == slot_cols, 1.0, 0.0).astype(BF16)
            acc_scr[rs, :] += _dot(jnp.concatenate([place, place], axis=1),
                                   jnp.concatenate([y_hi[ys, :], y_lo[ys, :]], axis=0))

    @pl.when(ovf_smem[grp] != 0)
    def _dense():
        for s in range(n_sub):
            rs = slice(s * MOE_SUB, (s + 1) * MOE_SUB)
            acc_scr[rs, :] += experts(xn_scr[rs, :], gate_scr[rs, :])

    @pl.when(grp == MOE_GROUPS - 1)
    def _finish():
        for s in range(n_sub):
            rs = slice(s * MOE_SUB, (s + 1) * MOE_SUB)
            x = x_ref[rs, :] + acc_scr[rs, :]
            gate = jax.nn.sigmoid(_dot(_rms(x, gp_ref[...]).astype(BF16), wpg_ref[...]))
            y = x + _dot(p_ref[rs, :].astype(BF16), wpi_ref[...]) * gate
            if final:
                y = _rms(y, gf_ref[...])
            out_ref[rs, :] = y


def _moe_ple(x2d, g, w_router, b_router, w_exp_gate, w_exp_up, w_exp_down,
             p2d, layer, g_ple, w_ple_gate, w_ple_in, g_final, final):
    t, d = x2d.shape
    f = w_exp_gate.shape[-1]
    tm = MOE_TILE
    row = pl.BlockSpec((tm, d), lambda i, e: (i, 0))
    group_w = lambda a, b: pl.BlockSpec((EXP_PER_GROUP, a, b), lambda i, e: (e, 0, 0))
    return pl.pallas_call(
        functools.partial(_moe_kernel, final),
        grid=(t // tm, MOE_GROUPS),
        in_specs=[row, _const_spec((1, d)), _const_spec(w_router.shape),
                  _const_spec(b_router.shape),
                  group_w(d, f), group_w(d, f), group_w(f, d),
                  pl.BlockSpec((tm, p2d.shape[1]), lambda i, e: (i, 0)),
                  _const_spec((1, d)), _layer_weight_spec(w_ple_gate, layer),
                  _layer_weight_spec(w_ple_in, layer), _const_spec((1, d))],
        out_specs=row,
        out_shape=jax.ShapeDtypeStruct((t, d), F32),
        scratch_shapes=[pltpu.VMEM((tm, d), BF16), pltpu.VMEM((tm, LANES), F32),
                        pltpu.VMEM((tm, LANES), BF16), pltpu.VMEM((tm, LANES), BF16),
                        pltpu.VMEM((tm, LANES), F32), pltpu.VMEM((8, tm), F32),
                        pltpu.VMEM((tm, d), F32), pltpu.SMEM((MOE_GROUPS,), jnp.int32),
                        pltpu.VMEM(w_ple_gate.shape[1:], BF16),
                        pltpu.VMEM(w_ple_in.shape[1:], BF16)],
        compiler_params=_cparams(2),
        name="moe_ple",
    )(x2d, g, w_router, b_router, w_exp_gate, w_exp_up, w_exp_down,
      p2d, g_ple, w_ple_gate, w_ple_in, g_final)


def _router_weights(w_grp, b_grp, w_exp, b_exp):
    d = w_grp.shape[0]
    w = jnp.concatenate([w_grp, w_exp.transpose(1, 0, 2).reshape(d, N_EXPERTS)], axis=1)
    b = jnp.concatenate([b_grp, b_exp.reshape(N_EXPERTS)])
    pad = LANES - w.shape[1]
    w = jnp.pad(w, ((0, 0), (0, pad)))
    w_hi = w.astype(BF16)
    w_lo = (w - w_hi.astype(F32)).astype(BF16)
    return jnp.concatenate([w_hi, w_lo], axis=1), jnp.pad(b, (0, pad))[None, :]


def kernel(x, p, positions, norm_mix, w_in, w_gate, b_gate, w_pool, pool_scale, w_up_a, w_up_b, w_up_c, w_out, norm_moe, w_router_grp, b_router_grp, w_router_exp, b_router_exp, w_exp_gate, w_exp_up, w_exp_down, norm_ple, w_ple_in, w_ple_gate, norm_final):
    batch, seq, d = x.shape
    depth = w_in.shape[0]
    t = batch * seq
    x2d = x.reshape(t, d)
    cos_t, sin_t = _rope_tables(positions)
    for i in range(depth):
        (qa, ka, va, qb, kb, vb, uc), (e_gate, e_up, e_down) = _inproj(
            x2d, norm_mix[i][None, :], w_in, i, cos_t, sin_t,
            (w_exp_gate, w_exp_up, w_exp_down))
        dil_mix = _dilated(qa, ka, va, batch, seq)
        sb_out = _stick_breaking(qb, kb, vb, batch, seq)
        x2d = _merge(x2d, dil_mix, sb_out, uc, seq, i, norm_mix[i][None, :], w_gate,
                     b_gate[i][None, :], w_pool, pool_scale[i][None, :],
                     w_up_a, w_up_b, w_up_c, w_out)
        w_r, b_r = _router_weights(w_router_grp[i], b_router_grp[i],
                                   w_router_exp[i], b_router_exp[i])
        x2d = _moe_ple(x2d, norm_moe[i][None, :], w_r, b_r, e_gate, e_up, e_down,
                       p[i].reshape(t, -1), i,
                       norm_ple[i][None, :], w_ple_gate, w_ple_in,
                       norm_final[None, :], i == depth - 1)
    return x2d.reshape(batch, seq, d)
```

```python
import functools

import jax
import jax.numpy as jnp
from jax import lax
from jax.experimental import pallas as pl
from jax.experimental.pallas import tpu as pltpu

F32 = jnp.float32
BF16 = jnp.bfloat16

HEAD_DIM = 64
ROPE_THETA = 10000.0
DIL_CONFIGS = ((128, 1), (512, 4), (2048, 16))
DIL_HEADS = 4
DIL_OUT = DIL_HEADS * HEAD_DIM
DIL_WIDTH = len(DIL_CONFIGS) * DIL_OUT
DIL_BLOCK = 128
DIL_CHUNK = 2048
DIL_SKEW = 3
INPROJ_TILE = 512
MERGE_TILE = 512
DIL_PERMUTED = (16,)
SB_HEADS = 8
SB_WIDTH = SB_HEADS * HEAD_DIM
SB_BLOCK = 128
POOL_WINDOWS = (2, 4, 8, 16)
POOL_GROUP = 128
POOL_WIDTH = len(POOL_WINDOWS) * POOL_GROUP
POOL_HALO = 16
N_BRANCH = 3
MOE_GROUPS = 4
EXP_PER_GROUP = 4
N_EXPERTS = MOE_GROUPS * EXP_PER_GROUP
EPS = 1e-6

LANES = 128
NEG_BIG = -1e30
SB_DEAD_LOG = -105.0
SB_WINDOW = 3
SB_GROUP = 28
SB_SKEW = 8
SB_FAR_ROWS = 64
VMEM_LIMIT = 56 * 1024 * 1024


def _cparams(n_axes):
    return pltpu.CompilerParams(
        dimension_semantics=("arbitrary",) * n_axes, vmem_limit_bytes=VMEM_LIMIT)


def _rms(x, g):
    ms = jnp.mean(x * x, axis=-1, keepdims=True)
    return x * lax.rsqrt(ms + EPS) * g


def _dot(a, b):
    return jnp.dot(a, b, preferred_element_type=F32)


def _dot_nt(a, b):
    return lax.dot_general(a, b, (((1,), (1,)), ((), ())), preferred_element_type=F32)


def _const_spec(shape):
    zeros = (0,) * len(shape)
    return pl.BlockSpec(shape, lambda *_: zeros)


def _layer_weight_spec(w, layer):
    tail = (0,) * (w.ndim - 1)
    return pl.BlockSpec((None,) + w.shape[1:], lambda *_: (layer,) + tail,
                        pipeline_mode=pl.Buffered(1))


def _cast_weights_once(first_step, pairs):
    @pl.when(first_step)
    def _cast():
        for src, dst in pairs:
            if len(src.shape) == 3:
                for k in range(src.shape[0]):
                    dst[k] = src[k].astype(BF16)
            else:
                step = 2 * LANES
                for c in range(0, src.shape[1], step):
                    dst[:, c:c + step] = src[:, c:c + step].astype(BF16)


def _rope_table_kernel(pos_ref, inv_ref, sgn_ref, cos_ref, sin_ref):
    ang = inv_ref[...] * pos_ref[...].astype(F32)
    reps = LANES // ang.shape[0]
    cos_ref[...] = jnp.transpose(jnp.concatenate([jnp.cos(ang)] * reps, axis=0))
    sin_ref[...] = jnp.transpose(jnp.concatenate([jnp.sin(ang)] * reps, axis=0) * sgn_ref[...])


def _rope_tables(positions):
    t = positions.size
    half = HEAD_DIM // 2
    inv_col = (ROPE_THETA ** (-jnp.arange(half, dtype=F32) / half))[:, None]
    sgn_col = jnp.tile(jnp.concatenate([-jnp.ones(half, F32), jnp.ones(half, F32)]),
                       LANES // HEAD_DIM)[:, None]
    tm = 2048
    return pl.pallas_call(
        _rope_table_kernel,
        grid=(t // tm,),
        in_specs=[pl.BlockSpec((None, 1, tm), lambda i: (i, 0, 0)),
                  _const_spec((half, 1)), _const_spec((LANES, 1))],
        out_specs=[pl.BlockSpec((tm, LANES), lambda i: (i, 0))] * 2,
        out_shape=[jax.ShapeDtypeStruct((t, LANES), F32)] * 2,
        compiler_params=_cparams(1),
        name="rope_table",
    )(positions.reshape(t // tm, 1, tm), inv_col, sgn_col)


def _inproj_kernel(x_ref, g_ref, w32_ref, cos_ref, sin_ref, eg32_ref, eu32_ref, ed32_ref,
                   qa_ref, ka_ref, va_ref, qb_ref, kb_ref, vb_ref, uc_ref,
                   eg_ref, eu_ref, ed_ref, h_scr, perm_scr, w_ref):
    _cast_weights_once(pl.program_id(0) == 0, [(w32_ref, w_ref)])
    eg_ref[...] = eg32_ref[...].astype(BF16)
    eu_ref[...] = eu32_ref[...].astype(BF16)
    ed_ref[...] = ed32_ref[...].astype(BF16)
    h_scr[...] = _rms(x_ref[...], g_ref[...]).astype(BF16)
    cos = cos_ref[...]
    sin = sin_ref[...]
    tm = cos.shape[0]
    lane = lax.broadcasted_iota(jnp.int32, (tm, LANES), 1)
    first_half = (lane & (HEAD_DIM // 2)) == 0
    scale = HEAD_DIM ** -0.5

    def rope(t):
        partner = jnp.where(first_half, pltpu.roll(t, LANES - HEAD_DIM // 2, 1),
                            pltpu.roll(t, HEAD_DIM // 2, 1))
        return t * cos + partner * sin

    def store_dilated(ref, which, c, val):
        sl = slice(c * LANES, (c + 1) * LANES)
        dil = DIL_CONFIGS[c // (DIL_OUT // LANES)][1]
        if dil not in DIL_PERMUTED:
            ref[:, sl] = val
            return
        per = tm // dil
        perm_scr[which, c % (DIL_OUT // LANES)] = val
        for r in range(dil):
            ref[r * per:(r + 1) * per, sl] = perm_scr[which, c % (DIL_OUT // LANES),
                                                      pl.ds(r, per, stride=dil), :]

    h = h_scr[...]
    col = 0
    pq = _dot(h, w_ref[:, col:col + DIL_WIDTH])
    for c in range(DIL_WIDTH // LANES):
        store_dilated(qa_ref, 0, c, rope(pq[:, c * LANES:(c + 1) * LANES]) * scale)
    col += DIL_WIDTH
    pk = _dot(h, w_ref[:, col:col + DIL_WIDTH])
    for c in range(DIL_WIDTH // LANES):
        store_dilated(ka_ref, 1, c, rope(pk[:, c * LANES:(c + 1) * LANES]))
    col += DIL_WIDTH
    pv = _dot(h, w_ref[:, col:col + DIL_WIDTH])
    for c in range(DIL_WIDTH // LANES):
        store_dilated(va_ref, 2, c, pv[:, c * LANES:(c + 1) * LANES])
    col += DIL_WIDTH
    qb_ref[...] = (_dot(h, w_ref[:, col:col + SB_WIDTH]) * scale).astype(BF16)
    col += SB_WIDTH
    kb_ref[...] = _dot(h, w_ref[:, col:col + SB_WIDTH]).astype(BF16)
    col += SB_WIDTH
    vb_ref[...] = _dot(h, w_ref[:, col:col + SB_WIDTH]).astype(BF16)
    col += SB_WIDTH
    uc_ref[...] = _dot(h, w_ref[:, col:col + POOL_WIDTH])


def _inproj(x2d, g, w_in, layer, cos_t, sin_t, expert_ws):
    t, d = x2d.shape
    tm = INPROJ_TILE
    steps = t // tm
    row = lambda width: pl.BlockSpec((tm, width), lambda i: (i, 0))
    widths = (DIL_WIDTH,) * 3 + (SB_WIDTH,) * 3 + (POOL_WIDTH,)
    dtypes = (F32,) * 3 + (BF16,) * 3 + (F32,)
    flat = [w.reshape(w.shape[0], -1, w.shape[-1]) for w in expert_ws]
    slabs = [(w.shape[1] // steps, w.shape[2]) for w in flat]
    outs = pl.pallas_call(
        _inproj_kernel,
        grid=(steps,),
        in_specs=[row(d), _const_spec((1, d)), _layer_weight_spec(w_in, layer),
                  row(LANES), row(LANES)]
                 + [pl.BlockSpec((None,) + s, lambda i: (layer, i, 0)) for s in slabs],
        out_specs=[row(w) for w in widths]
                  + [pl.BlockSpec(s, lambda i: (i, 0)) for s in slabs],
        out_shape=[jax.ShapeDtypeStruct((t, w), dt) for w, dt in zip(widths, dtypes)]
                  + [jax.ShapeDtypeStruct(w.shape[1:], BF16) for w in flat],
        scratch_shapes=[pltpu.VMEM((tm, d), BF16),
                        pltpu.VMEM((3, DIL_OUT // LANES, tm, LANES), F32),
                        pltpu.VMEM(w_in.shape[1:], BF16)],
        compiler_params=_cparams(1),
        name="in_proj",
    )(x2d, g, w_in, cos_t, sin_t, *flat)
    experts_bf16 = [o.reshape(w.shape[1:]) for o, w in zip(outs[len(widths):], expert_ws)]
    return outs[:len(widths)], experts_bf16


def _dil_kernel(*refs):
    n_grp = len(DIL_CONFIGS)
    out_ref, o_scr, l_scr = refs[5 * n_grp:]
    for g, (_, dil) in enumerate(DIL_CONFIGS):
        _dil_group(dil, *refs[5 * g:5 * g + 5], o_scr.at[g], l_scr.at[g])
    lses = [l_scr[g] for g in range(n_grp)]
    m = functools.reduce(jnp.maximum, lses)
    es = [jnp.exp(l - m) for l in lses]
    num = sum(e * o_scr[g] for g, e in enumerate(es))
    out_ref[...] = (num / sum(es)).astype(BF16)


def _dil_group(dil, q_ref, kp_ref, kc_ref, vp_ref, vc_ref, o_ref, lse_ref):
    has_prev = pl.program_id(1) > 0
    n = DIL_BLOCK
    n_blocks = q_ref.shape[0] // (n * dil)
    qi = lax.broadcasted_iota(jnp.int32, (n, n), 0)
    kj = lax.broadcasted_iota(jnp.int32, (n, n), 1)
    mask_prev = kj >= qi
    mask_prev_first = jnp.logical_and(mask_prev, has_prev)
    mask_cur = kj <= qi
    lo_head = kj < HEAD_DIM
    ones = jnp.ones((2 * n, LANES), BF16)

    def rows(ref, r, j):
        if dil in DIL_PERMUTED:
            per = INPROJ_TILE // dil
            pieces = [ref[ti * INPROJ_TILE + r * per:ti * INPROJ_TILE + (r + 1) * per, :]
                      for ti in range(j * (n // per), (j + 1) * (n // per))]
            return jnp.concatenate(pieces, axis=0).astype(BF16)
        start = j * n * dil + r
        return ref[pl.ds(start, n, stride=dil) if dil > 1 else pl.ds(start, n), :].astype(BF16)

    units = {}

    def unit(r, j):
        if (r, j) not in units:
            kc, vc = rows(kc_ref, r, j), rows(vc_ref, r, j)
            if j == 0:
                kp, vp = rows(kp_ref, r, 0), rows(vp_ref, r, 0)
            else:
                kp, vp = units[(r, j - 1)]["kc"], units[(r, j - 1)]["vc"]
            vo = jnp.concatenate([jnp.concatenate([vp, vc], axis=0), ones], axis=1)
            units[(r, j)] = dict(q=rows(q_ref, r, j), kc=kc, vc=vc, kp=kp, vo=vo,
                                 pm=mask_prev_first if j == 0 else mask_prev,
                                 start=j * n * dil + r)
        return units[(r, j)]

    def stage_scores(c):
        u = unit(c["r"], c["j"])
        sel = lo_head if c["hh"] == 0 else jnp.logical_not(lo_head)
        qm = jnp.where(sel, u["q"], jnp.zeros((n, LANES), BF16))
        c["sp"] = _dot_nt(qm, u["kp"])
        c["sc"] = _dot_nt(qm, u["kc"])

    def stage_max(c):
        u = unit(c["r"], c["j"])
        c["sp"] = jnp.where(u["pm"], c["sp"], NEG_BIG)
        c["sc"] = jnp.where(mask_cur, c["sc"], NEG_BIG)
        c["m"] = jnp.max(jnp.maximum(c["sp"], c["sc"]), axis=-1, keepdims=True)

    def stage_pv(c):
        u = unit(c["r"], c["j"])
        ep = jnp.exp(c.pop("sp") - c["m"]).astype(BF16)
        ec = jnp.exp(c.pop("sc") - c["m"]).astype(BF16)
        res = _dot(jnp.concatenate([ep, ec], axis=1), u["vo"])
        c["acc"] = res[:, :LANES]
        c["den"] = res[:, LANES:]

    def stage_out(c, pairs):
        c["o"] = c.pop("acc") / c["den"]
        c["lse"] = c["m"] + jnp.log(c.pop("den"))
        key = (c["r"], c["j"])
        pairs.setdefault(key, []).append(c)
        if len(pairs[key]) == 2:
            c0, c1 = pairs.pop(key)
            u = unit(c["r"], c["j"])
            idx = pl.ds(u["start"], n, stride=dil) if dil > 1 else pl.ds(u["start"], n)
            o_ref[idx, :] = jnp.where(lo_head, c0["o"], c1["o"])
            lse_ref[idx, :] = jnp.where(lo_head, c0["lse"], c1["lse"])

    chains = [dict(r=r, j=j, hh=hh) for r in range(dil) for j in range(n_blocks)
              for hh in range(2)]
    stages = (stage_scores, stage_max, stage_pv)
    pairs = {}
    total = len(chains)
    for step in range(total + len(stages) * DIL_SKEW):
        for s, fn in enumerate(stages):
            i = step - s * DIL_SKEW
            if 0 <= i < total:
                fn(chains[i])
        i = step - len(stages) * DIL_SKEW
        if 0 <= i < total:
            stage_out(chains[i], pairs)


def _dilated(qa, ka, va, batch, seq):
    assert seq % DIL_CHUNK == 0
    nc = seq // DIL_CHUNK
    t = batch * seq
    n_pairs = DIL_OUT // LANES
    in_specs = []
    for g, (window, dil) in enumerate(DIL_CONFIGS):
        assert window // dil == DIL_BLOCK
        hist = DIL_BLOCK * dil
        cur = pl.BlockSpec((DIL_CHUNK, LANES),
                           lambda b, c, pr, g=g: (b * nc + c, g * n_pairs + pr))
        prev = pl.BlockSpec(
            (hist, LANES),
            lambda b, c, pr, g=g, hist=hist: (
                jnp.maximum((b * nc + c) * (DIL_CHUNK // hist) - 1, 0), g * n_pairs + pr))
        in_specs += [cur, prev, cur, prev, cur]
    n_grp = len(DIL_CONFIGS)
    return pl.pallas_call(
        _dil_kernel,
        grid=(batch, nc, n_pairs),
        in_specs=in_specs,
        out_specs=pl.BlockSpec((DIL_CHUNK, LANES), lambda b, c, pr: (b * nc + c, pr)),
        out_shape=jax.ShapeDtypeStruct((t, DIL_OUT), BF16),
        scratch_shapes=[pltpu.VMEM((n_grp, DIL_CHUNK, LANES), F32)] * 2,
        compiler_params=_cparams(3),
        name="dilated",
    )(*([qa, ka, ka, va, va] * n_grp))


def _sb_kernel(q_ref, k_ref, v_ref, o_ref, acc_scr, r_scr):
    blk = SB_BLOCK
    nq = q_ref.shape[0] // blk
    row = lax.broadcasted_iota(jnp.int32, (blk, blk), 0)
    col = lax.broadcasted_iota(jnp.int32, (blk, blk), 1)
    tri = col < row
    lo_head = col < HEAD_DIM
    cum_mat = jnp.where(row > col, 1.0, 0.0).astype(BF16)

    def head_q(q, hh):
        sel = lo_head if hh == 0 else jnp.logical_not(lo_head)
        return jnp.where(sel, q, jnp.zeros_like(q))

    def tile_scores(z, mask):
        sp = jnp.maximum(z, 0.0) + jnp.log(1.0 + jnp.exp(-jnp.abs(z)))
        lom = -sp
        if mask is not None:
            lom = jnp.where(mask, lom, 0.0)
        hi = lom.astype(BF16)
        lo = (lom - hi.astype(F32)).astype(BF16)
        after = _dot(hi, cum_mat) + _dot(lo, cum_mat)
        return z - sp + after, jnp.sum(lom, axis=-1, keepdims=True)

    def tile_weights(logit, mask, r_acc):
        a = jnp.exp(logit + r_acc)
        if mask is not None:
            a = jnp.where(mask, a, 0.0)
        return a.astype(BF16)

    def q_group(blocks):
        chains = []
        for qi, n_tiles in blocks:
            qoff = qi * blk
            koff = (qi - (n_tiles - 1)) * blk
            if not isinstance(qi, int):
                qoff, koff = pl.multiple_of(qoff, blk), pl.multiple_of(koff, blk)
            q = q_ref[pl.ds(qoff, blk), :]
            kw = k_ref[pl.ds(koff, n_tiles * blk), :]
            vw = v_ref[pl.ds(koff, n_tiles * blk), :]
            for hh in range(2):
                chains.append(dict(z=_dot_nt(head_q(q, hh), kw), vw=vw, n=n_tiles,
                                   r=jnp.zeros((blk, 1), F32), acc=jnp.zeros((blk, blk), F32)))
        max_tiles = max(c["n"] for c in chains)
        tiles = [(c, c["n"] - 1 - j) for j in range(max_tiles) for c in chains if j < c["n"]]
        pending = []
        for step in range(len(tiles) + SB_SKEW):
            if step < len(tiles):
                c, w = tiles[step]
                mask = tri if w == c["n"] - 1 else None
                rows = SB_FAR_ROWS if (w == 0 and c["n"] == SB_WINDOW) else blk
                logit, rowsum = tile_scores(c["z"][:rows, w * blk:(w + 1) * blk], mask)
                pending.append((c, w, mask, logit, rowsum, rows))
            if step >= SB_SKEW:
                c, w, mask, logit, rowsum, rows = pending[step - SB_SKEW]
                a = tile_weights(logit, mask, c["r"][:rows])
                pv = _dot(a, c["vw"][w * blk:(w + 1) * blk, :])
                if rows == blk:
                    c["acc"] = c["acc"] + pv
                    c["r"] = c["r"] + rowsum
                else:
                    c["acc"] = jnp.concatenate([c["acc"][:rows] + pv, c["acc"][rows:]], axis=0)
                    c["r"] = jnp.concatenate([c["r"][:rows] + rowsum, c["r"][rows:]], axis=0)
        for b, (qi, _) in enumerate(blocks):
            qoff = qi * blk if isinstance(qi, int) else pl.multiple_of(qi * blk, blk)
            c0, c1 = chains[2 * b], chains[2 * b + 1]
            r_scr[0, pl.ds(qoff, blk), :] = jnp.broadcast_to(c0["r"], (blk, blk))
            r_scr[1, pl.ds(qoff, blk), :] = jnp.broadcast_to(c1["r"], (blk, blk))
            acc_pair = jnp.where(lo_head, c0["acc"], c1["acc"])
            acc_scr[pl.ds(qoff, blk), :] = acc_pair
            o_ref[pl.ds(qoff, blk), :] = acc_pair.astype(BF16)

    first_tail = min(SB_WINDOW - 1, nq)
    n_static = SB_WINDOW + (nq - SB_WINDOW) % SB_GROUP
    q_group([(qi, min(qi + 1, SB_WINDOW)) for qi in range(n_static)])

    def q_loop(it, carry):
        q0 = n_static + it * SB_GROUP
        q_group([(q0 + j, SB_WINDOW) for j in range(SB_GROUP)])
        return carry

    lax.fori_loop(0, (nq - n_static) // SB_GROUP, q_loop, 0)


    @pl.when(jnp.max(r_scr[:, first_tail * blk:, :]) > SB_DEAD_LOG)
    def _tail():
        def q_tail(qi, carry):
            qoff = pl.multiple_of(qi * blk, blk)
            q = q_ref[pl.ds(qoff, blk), :]
            acc0 = acc_scr[pl.ds(qoff, blk), :]
            res = []
            for hh in range(2):
                qm = head_q(q, hh)
                r0 = r_scr[hh, pl.ds(qoff, blk), :]

                def cond(c):
                    return jnp.logical_and(c[0] >= 0, c[3] > SB_DEAD_LOG)

                def body(c, qm=qm):
                    off = pl.multiple_of(c[0] * blk, blk)
                    z = _dot_nt(qm, k_ref[pl.ds(off, blk), :])
                    todo = jnp.logical_or(row >= SB_FAR_ROWS, c[0] != qi - (SB_WINDOW - 1))
                    logit, rowsum = tile_scores(z, todo)
                    a = tile_weights(logit, todo, c[1])
                    r_new = c[1] + rowsum
                    acc_new = c[2] + _dot(a, v_ref[pl.ds(off, blk), :])
                    return c[0] - 1, r_new, acc_new, jnp.max(r_new)

                c0 = (qi - (SB_WINDOW - 1), r0, acc0, jnp.max(r0))
                res.append(lax.while_loop(cond, body, c0)[2])
            o_ref[pl.ds(qoff, blk), :] = jnp.where(lo_head, res[0], res[1]).astype(BF16)
            return carry

        lax.fori_loop(first_tail, nq, q_tail, 0)


def _stick_breaking(qb, kb, vb, batch, seq):
    view = lambda a: a.reshape(batch, seq, SB_WIDTH)
    spec = pl.BlockSpec((None, seq, LANES), lambda b, hp: (b, 0, hp))
    o = pl.pallas_call(
        _sb_kernel,
        grid=(batch, SB_WIDTH // LANES),
        in_specs=[spec, spec, spec],
        out_specs=spec,
        out_shape=jax.ShapeDtypeStruct((batch, seq, SB_WIDTH), BF16),
        scratch_shapes=[pltpu.VMEM((seq, LANES), F32), pltpu.VMEM((2, seq, LANES), F32)],
        compiler_params=_cparams(2),
        name="stick_breaking",
    )(view(qb), view(kb), view(vb))
    return o.reshape(batch * seq, SB_WIDTH)


def _merge_kernel(seq, x_ref, dil_ref, sb_ref,
                  uc_ref, uh_ref, g_ref, wg32_ref, bg_ref, wp32_ref, ps_ref,
                  wa32_ref, wb32_ref, wc32_ref, wo32_ref, out_ref,
                  h_scr, ub_scr, yc_scr, m_scr, wg_ref, wp_ref, wa_ref, wb_ref, wc_ref, wo_ref):
    i = pl.program_id(0)
    _cast_weights_once(i == 0, [(wg32_ref, wg_ref), (wp32_ref, wp_ref), (wa32_ref, wa_ref),
                                (wb32_ref, wb_ref), (wc32_ref, wc_ref), (wo32_ref, wo_ref)])
    tm, d = x_ref.shape
    x = x_ref[...]
    h_scr[...] = _rms(x, g_ref[...]).astype(BF16)
    cw = 256

    def gate_chunk(c):
        gates = []
        for br in range(N_BRANCH):
            gsl = slice(br * d + c * cw, br * d + (c + 1) * cw)
            gates.append(jax.nn.sigmoid(_dot(h_scr[...], wg_ref[:, gsl]) + bg_ref[:, gsl]))
        return gates

    dil = dil_ref[...]

    row0 = (i * tm) % seq
    halo = jnp.where(row0 == 0, 0.0, uh_ref[...])
    ub_scr[0:POOL_HALO, :] = halo
    ub_scr[POOL_HALO:POOL_HALO + tm, :] = uc_ref[...]
    t_in_seq = row0 + lax.broadcasted_iota(jnp.int32, (tm, 1), 0)
    for g, w in enumerate(POOL_WINDOWS):
        sl = slice(g * POOL_GROUP, (g + 1) * POOL_GROUP)
        u = ub_scr[POOL_HALO:POOL_HALO + tm, sl]
        acc = u
        for j in range(1, w):
            acc = acc + ub_scr[POOL_HALO - j:POOL_HALO - j + tm, sl]
        cnt = jnp.minimum(t_in_seq + 1, w).astype(F32)
        pooled = acc / cnt - u
        y = _dot(pooled.astype(BF16), wp_ref[g]) * ps_ref[:, sl]
        yc_scr[:, sl] = y.astype(BF16)

    sb = sb_ref[...]
    yc_in = yc_scr[...]
    for c in range(d // cw):
        sl = slice(c * cw, (c + 1) * cw)
        gates = gate_chunk(c)
        merged = (gates[0] * _dot(dil, wa_ref[:, sl]) + gates[1] * _dot(sb, wb_ref[:, sl])
                  + gates[2] * _dot(yc_in, wc_ref[:, sl]))
        m_scr[:, sl] = merged.astype(BF16)
    out_ref[...] = x + _dot(m_scr[...], wo_ref[...])


def _merge(x2d, dil_mix, sb_out, uc, seq, layer, g, w_gate, b_gate, w_pool, pool_scale,
           w_up_a, w_up_b, w_up_c, w_out):
    t, d = x2d.shape
    tm = MERGE_TILE
    row = lambda width: pl.BlockSpec((tm, width), lambda i: (i, 0))
    halo = pl.BlockSpec((POOL_HALO, POOL_WIDTH),
                        lambda i: (jnp.maximum(i * (tm // POOL_HALO) - 1, 0), 0))
    big = (w_gate, w_pool, w_up_a, w_up_b, w_up_c, w_out)
    lw = lambda w: _layer_weight_spec(w, layer)
    return pl.pallas_call(
        functools.partial(_merge_kernel, seq),
        grid=(t // tm,),
        in_specs=[row(d), row(DIL_OUT), row(SB_WIDTH), row(POOL_WIDTH), halo]
                 + [_const_spec(g.shape), lw(w_gate), _const_spec(b_gate.shape), lw(w_pool),
                    _const_spec(pool_scale.shape), lw(w_up_a), lw(w_up_b), lw(w_up_c),
                    lw(w_out)],
        out_specs=row(d),
        out_shape=jax.ShapeDtypeStruct((t, d), F32),
        scratch_shapes=[pltpu.VMEM((tm, d), BF16),
                        pltpu.VMEM((POOL_HALO + tm, POOL_WIDTH), F32),
                        pltpu.VMEM((tm, POOL_WIDTH), BF16),
                        pltpu.VMEM((tm, d), BF16)]
                       + [pltpu.VMEM(w.shape[1:], BF16) for w in big],
        compiler_params=_cparams(1),
        name="merge",
    )(x2d, dil_mix, sb_out, uc, uc, g, w_gate, b_gate, w_pool, pool_scale,
      w_up_a, w_up_b, w_up_c, w_out)


ROUTER_EXP_LANE0 = MOE_GROUPS
MOE_TILE = 1024
MOE_SUB = 256
MOE_SLOTS = 80


def _moe_kernel(final, x_ref, g_ref, wr_ref, br_ref, wg_ref, wu_ref, wd_ref,
                p_ref, gp_ref, wpg32_ref, wpi32_ref, gf_ref, out_ref,
                xn_scr, gate_scr, ghi_scr, glo_scr, sel_scr, selt_scr, acc_scr, ovf_smem,
                wpg_ref, wpi_ref):
    grp = pl.program_id(1)
    _cast_weights_once(jnp.logical_and(pl.program_id(0) == 0, grp == 0),
                       [(wpg32_ref, wpg_ref), (wpi32_ref, wpi_ref)])
    tm, d = x_ref.shape
    n_sub = tm // MOE_SUB
    lane_f = lax.broadcasted_iota(jnp.int32, (tm, LANES), 1).astype(F32)

    @pl.when(grp == 0)
    def _route():
        xn = _rms(x_ref[...], g_ref[...])
        a_hi = xn.astype(BF16)
        xn_scr[...] = a_hi
        a_lo = (xn - a_hi.astype(F32)).astype(BF16)
        r_hi = _dot(a_hi, wr_ref[...])
        logits = (r_hi[:, :LANES] + r_hi[:, LANES:] + _dot(a_lo, wr_ref[:, :LANES])
                  + br_ref[...])
        is_grp = lane_f < MOE_GROUPS
        lg = jnp.where(is_grp, logits, NEG_BIG)
        mg = jnp.max(lg, axis=-1, keepdims=True)
        p1 = 1.0 / jnp.sum(jnp.exp(lg - mg), axis=-1, keepdims=True)
        gi = jnp.min(jnp.where(lg == mg, lane_f, float(LANES)), axis=-1, keepdims=True)
        lane_grp = jnp.floor((lane_f - ROUTER_EXP_LANE0) * (1.0 / EXP_PER_GROUP))
        in_grp = jnp.logical_and(
            jnp.logical_and(lane_f >= ROUTER_EXP_LANE0, lane_f < ROUTER_EXP_LANE0 + N_EXPERTS),
            lane_grp == gi)
        le = jnp.where(in_grp, logits, NEG_BIG)
        v1 = jnp.max(le, axis=-1, keepdims=True)
        i1 = jnp.min(jnp.where(le == v1, lane_f, float(LANES)), axis=-1, keepdims=True)
        le2 = jnp.where(lane_f == i1, NEG_BIG, le)
        v2 = jnp.max(le2, axis=-1, keepdims=True)
        i2 = jnp.min(jnp.where(le2 == v2, lane_f, float(LANES)), axis=-1, keepdims=True)
        t = jnp.exp(v2 - v1)
        w1 = p1 / (1.0 + t)
        w2 = p1 * t / (1.0 + t)
        gate = jnp.where(lane_f == i1, w1, 0.0) + jnp.where(lane_f == i2, w2, 0.0)
        gate_scr[...] = gate
        g_hi = gate.astype(BF16)
        ghi_scr[...] = g_hi
        glo_scr[...] = (gate - g_hi.astype(F32)).astype(BF16)
        acc_scr[...] = jnp.zeros_like(acc_scr)

        onehot = jnp.where(lane_f == gi, 1.0, 0.0).astype(BF16)
        r = lax.broadcasted_iota(jnp.int32, (MOE_SUB, MOE_SUB), 0)
        c = lax.broadcasted_iota(jnp.int32, (MOE_SUB, MOE_SUB), 1)
        earlier = jnp.where(c < r, 1.0, 0.0).astype(BF16)
        fullest = jnp.zeros((1, LANES), F32)
        gi_b = jnp.broadcast_to(gi, (tm, LANES))
        for s in range(n_sub):
            rs = slice(s * MOE_SUB, (s + 1) * MOE_SUB)
            oh = onehot[rs, :]
            ahead = _dot(earlier, oh)
            slot = jnp.sum(jnp.where(oh > 0, ahead, 0.0), axis=-1, keepdims=True)
            sel_b = jnp.where(slot < LANES, slot + float(LANES) * gi_b[rs, :], -1.0)
            sel_scr[rs, :] = sel_b
            selt_scr[:, rs] = jnp.transpose(sel_b)[:8, :]
            fullest = jnp.maximum(fullest, jnp.sum(oh.astype(F32), axis=0, keepdims=True))
        lane1 = lax.broadcasted_iota(jnp.int32, (1, LANES), 1)
        for gidx in range(MOE_GROUPS):
            most = jnp.max(jnp.where(lane1 == gidx, fullest, 0.0))
            ovf_smem[gidx] = (most > MOE_SLOTS).astype(jnp.int32)

    def experts(rows, gates):
        lane_r = lax.broadcasted_iota(jnp.int32, gates.shape, 1)
        y = None
        for j in range(EXP_PER_GROUP):
            hg = _dot(rows, wg_ref[j])
            hu = _dot(rows, wu_ref[j])
            here = lane_r == ROUTER_EXP_LANE0 + EXP_PER_GROUP * grp + j
            gcol = jnp.sum(jnp.where(here, gates, 0.0), axis=-1, keepdims=True)
            hid = ((hg * jax.nn.sigmoid(hg)) * hu * gcol).astype(BF16)
            yj = _dot(hid, wd_ref[j])
            y = yj if y is None else y + yj
        return y

    key = (grp * LANES).astype(F32)

    @pl.when(ovf_smem[grp] == 0)
    def _compact():
        slot_rows = lax.broadcasted_iota(jnp.int32, (MOE_SLOTS, MOE_SUB), 0).astype(F32) + key
        slot_cols = lax.broadcasted_iota(jnp.int32, (MOE_SUB, LANES), 1).astype(F32) + key
        xs, gs = [], []
        for s in range(n_sub):
            rs = slice(s * MOE_SUB, (s + 1) * MOE_SUB)
            pick = jnp.where(selt_scr[0:1, rs] == slot_rows, 1.0, 0.0).astype(BF16)
            xs.append(_dot(pick, xn_scr[rs, :]).astype(BF16))
            gs.append(_dot(pick, ghi_scr[rs, :]) + _dot(pick, glo_scr[rs, :]))
        y = experts(jnp.concatenate(xs, axis=0), jnp.concatenate(gs, axis=0))
        y = jnp.concatenate([y, jnp.zeros((LANES - MOE_SLOTS, d), F32)], axis=0)
        y_hi = y.astype(BF16)
        y_lo = (y - y_hi.astype(F32)).astype(BF16)
        for s in range(n_sub):
            rs = slice(s * MOE_SUB, (s + 1) * MOE_SUB)
            ys = slice(s * MOE_SLOTS, s * MOE_SLOTS + LANES)
            place = jnp.where(sel_scr[rs, :] == slot_cols, 1.0, 0.0).astype(BF16)
            acc_scr[rs, :] += _dot(jnp.concatenate([place, place], axis=1),
                                   jnp.concatenate([y_hi[ys, :], y_lo[ys, :]], axis=0))

    @pl.when(ovf_smem[grp] != 0)
    def _dense():
        for s in range(n_sub):
            rs = slice(s * MOE_SUB, (s + 1) * MOE_SUB)
            acc_scr[rs, :] += experts(xn_scr[rs, :], gate_scr[rs, :])

    @pl.when(grp == MOE_GROUPS - 1)
    def _finish():
        for s in range(n_sub):
            rs = slice(s * MOE_SUB, (s + 1) * MOE_SUB)
            x = x_ref[rs, :] + acc_scr[rs, :]
            gate = jax.nn.sigmoid(_dot(_rms(x, gp_ref[...]).astype(BF16), wpg_ref[...]))
            y = x + _dot(p_ref[rs, :].astype(BF16), wpi_ref[...]) * gate
            if final:
                y = _rms(y, gf_ref[...])
            out_ref[rs, :] = y


def _moe_ple(x2d, g, w_router, b_router, w_exp_gate, w_exp_up, w_exp_down,
             p2d, layer, g_ple, w_ple_gate, w_ple_in, g_final, final):
    t, d = x2d.shape
    f = w_exp_gate.shape[-1]
    tm = MOE_TILE
    row = pl.BlockSpec((tm, d), lambda i, e: (i, 0))
    group_w = lambda a, b: pl.BlockSpec((EXP_PER_GROUP, a, b), lambda i, e: (e, 0, 0))
    return pl.pallas_call(
        functools.partial(_moe_kernel, final),
        grid=(t // tm, MOE_GROUPS),
        in_specs=[row, _const_spec((1, d)), _const_spec(w_router.shape),
                  _const_spec(b_router.shape),
                  group_w(d, f), group_w(d, f), group_w(f, d),
                  pl.BlockSpec((tm, p2d.shape[1]), lambda i, e: (layer * (t // tm) + i, 0)),
                  _const_spec((1, d)), _layer_weight_spec(w_ple_gate, layer),
                  _layer_weight_spec(w_ple_in, layer), _const_spec((1, d))],
        out_specs=row,
        out_shape=jax.ShapeDtypeStruct((t, d), F32),
        scratch_shapes=[pltpu.VMEM((tm, d), BF16), pltpu.VMEM((tm, LANES), F32),
                        pltpu.VMEM((tm, LANES), BF16), pltpu.VMEM((tm, LANES), BF16),
                        pltpu.VMEM((tm, LANES), F32), pltpu.VMEM((8, tm), F32),
                        pltpu.VMEM((tm, d), F32), pltpu.SMEM((MOE_GROUPS,), jnp.int32),
                        pltpu.VMEM(w_ple_gate.shape[1:], BF16),
                        pltpu.VMEM(w_ple_in.shape[1:], BF16)],
        compiler_params=_cparams(2),
        name="moe_ple",
    )(x2d, g, w_router, b_router, w_exp_gate, w_exp_up, w_exp_down,
      p2d, g_ple, w_ple_gate, w_ple_in, g_final)


def _router_weights(w_grp, b_grp, w_exp, b_exp):
    d = w_grp.shape[0]
    w = jnp.concatenate([w_grp, w_exp.transpose(1, 0, 2).reshape(d, N_EXPERTS)], axis=1)
    b = jnp.concatenate([b_grp, b_exp.reshape(N_EXPERTS)])
    pad = LANES - w.shape[1]
    w = jnp.pad(w, ((0, 0), (0, pad)))
    w_hi = w.astype(BF16)
    w_lo = (w - w_hi.astype(F32)).astype(BF16)
    return jnp.concatenate([w_hi, w_lo], axis=1), jnp.pad(b, (0, pad))[None, :]


def kernel(x, p, positions, norm_mix, w_in, w_gate, b_gate, w_pool, pool_scale, w_up_a, w_up_b, w_up_c, w_out, norm_moe, w_router_grp, b_router_grp, w_router_exp, b_router_exp, w_exp_gate, w_exp_up, w_exp_down, norm_ple, w_ple_in, w_ple_gate, norm_final):
    batch, seq, d = x.shape
    depth = w_in.shape[0]
    t = batch * seq
    x2d = x.reshape(t, d)
    cos_t, sin_t = _rope_tables(positions)
    for i in range(depth):
        (qa, ka, va, qb, kb, vb, uc), (e_gate, e_up, e_down) = _inproj(
            x2d, norm_mix[i][None, :], w_in, i, cos_t, sin_t,
            (w_exp_gate, w_exp_up, w_exp_down))
        dil_mix = _dilated(qa, ka, va, batch, seq)
        sb_out = _stick_breaking(qb, kb, vb, batch, seq)
        x2d = _merge(x2d, dil_mix, sb_out, uc, seq, i, norm_mix[i][None, :], w_gate,
                     b_gate[i][None, :], w_pool, pool_scale[i][None, :],
                     w_up_a, w_up_b, w_up_c, w_out)
        w_r, b_r = _router_weights(w_router_grp[i], b_router_grp[i],
                                   w_router_exp[i], b_router_exp[i])
        x2d = _moe_ple(x2d, norm_moe[i][None, :], w_r, b_r, e_gate, e_up, e_down,
                       p.reshape(depth * t, -1), i,
                       norm_ple[i][None, :], w_ple_gate, w_ple_in,
                       norm_final[None, :], i == depth - 1)
    return x2d.reshape(batch, seq, d)
```

```python
import functools

import jax
import jax.numpy as jnp
from jax import lax
from jax.experimental import pallas as pl
from jax.experimental.pallas import tpu as pltpu

F32 = jnp.float32
BF16 = jnp.bfloat16

HEAD_DIM = 64
ROPE_THETA = 10000.0
DIL_CONFIGS = ((128, 1), (512, 4), (2048, 16))
DIL_HEADS = 4
DIL_OUT = DIL_HEADS * HEAD_DIM
DIL_WIDTH = len(DIL_CONFIGS) * DIL_OUT
DIL_BLOCK = 128
DIL_CHUNK = 2048
DIL_SKEW = 3
INPROJ_TILE = 512
MERGE_TILE = 512
DIL_PERMUTED = (16,)
SB_HEADS = 8
SB_WIDTH = SB_HEADS * HEAD_DIM
SB_BLOCK = 128
POOL_WINDOWS = (2, 4, 8, 16)
POOL_GROUP = 128
POOL_WIDTH = len(POOL_WINDOWS) * POOL_GROUP
POOL_HALO = 16
N_BRANCH = 3
MOE_GROUPS = 4
EXP_PER_GROUP = 4
N_EXPERTS = MOE_GROUPS * EXP_PER_GROUP
EPS = 1e-6

LANES = 128
NEG_BIG = -1e30
SB_DEAD_LOG = -105.0
SB_WINDOW = 3
SB_GROUP = 28
SB_SKEW = 8
SB_FAR_ROWS = 64
VMEM_LIMIT = 56 * 1024 * 1024


def _cparams(n_axes):
    return pltpu.CompilerParams(
        dimension_semantics=("arbitrary",) * n_axes, vmem_limit_bytes=VMEM_LIMIT)


def _rms(x, g):
    ms = jnp.mean(x * x, axis=-1, keepdims=True)
    return x * lax.rsqrt(ms + EPS) * g


def _dot(a, b):
    return jnp.dot(a, b, preferred_element_type=F32)


def _dot_nt(a, b):
    return lax.dot_general(a, b, (((1,), (1,)), ((), ())), preferred_element_type=F32)


def _const_spec(shape):
    zeros = (0,) * len(shape)
    return pl.BlockSpec(shape, lambda *_: zeros)


def _layer_weight_spec(w, layer):
    tail = (0,) * (w.ndim - 1)
    return pl.BlockSpec((None,) + w.shape[1:], lambda *_: (layer,) + tail,
                        pipeline_mode=pl.Buffered(1))


def _cast_weights_once(first_step, pairs):
    @pl.when(first_step)
    def _cast():
        for src, dst in pairs:
            if len(src.shape) == 3:
                for k in range(src.shape[0]):
                    dst[k] = src[k].astype(BF16)
            else:
                step = 2 * LANES
                for c in range(0, src.shape[1], step):
                    dst[:, c:c + step] = src[:, c:c + step].astype(BF16)


def _rope_table_kernel(pos_ref, inv_ref, sgn_ref, cos_ref, sin_ref):
    ang = inv_ref[...] * pos_ref[...].astype(F32)
    reps = LANES // ang.shape[0]
    cos_ref[...] = jnp.transpose(jnp.concatenate([jnp.cos(ang)] * reps, axis=0))
    sin_ref[...] = jnp.transpose(jnp.concatenate([jnp.sin(ang)] * reps, axis=0) * sgn_ref[...])


def _rope_tables(positions):
    t = positions.size
    half = HEAD_DIM // 2
    inv_col = (ROPE_THETA ** (-jnp.arange(half, dtype=F32) / half))[:, None]
    sgn_col = jnp.tile(jnp.concatenate([-jnp.ones(half, F32), jnp.ones(half, F32)]),
                       LANES // HEAD_DIM)[:, None]
    tm = 2048
    return pl.pallas_call(
        _rope_table_kernel,
        grid=(t // tm,),
        in_specs=[pl.BlockSpec((None, 1, tm), lambda i: (i, 0, 0)),
                  _const_spec((half, 1)), _const_spec((LANES, 1))],
        out_specs=[pl.BlockSpec((tm, LANES), lambda i: (i, 0))] * 2,
        out_shape=[jax.ShapeDtypeStruct((t, LANES), F32)] * 2,
        compiler_params=_cparams(1),
        name="rope_table",
    )(positions.reshape(t // tm, 1, tm), inv_col, sgn_col)


def _inproj_kernel(x_ref, g_ref, w32_ref, cos_ref, sin_ref, eg32_ref, eu32_ref, ed32_ref,
                   qa_ref, ka_ref, va_ref, qb_ref, kb_ref, vb_ref, uc_ref,
                   eg_ref, eu_ref, ed_ref, h_scr, perm_scr, w_ref):
    _cast_weights_once(pl.program_id(0) == 0, [(w32_ref, w_ref)])
    eg_ref[...] = eg32_ref[...].astype(BF16)
    eu_ref[...] = eu32_ref[...].astype(BF16)
    ed_ref[...] = ed32_ref[...].astype(BF16)
    h_scr[...] = _rms(x_ref[...], g_ref[...]).astype(BF16)
    cos = cos_ref[...]
    sin = sin_ref[...]
    tm = cos.shape[0]
    lane = lax.broadcasted_iota(jnp.int32, (tm, LANES), 1)
    first_half = (lane & (HEAD_DIM // 2)) == 0
    scale = HEAD_DIM ** -0.5

    def rope(t):
        partner = jnp.where(first_half, pltpu.roll(t, LANES - HEAD_DIM // 2, 1),
                            pltpu.roll(t, HEAD_DIM // 2, 1))
        return t * cos + partner * sin

    def store_dilated(ref, which, c, val):
        sl = slice(c * LANES, (c + 1) * LANES)
        dil = DIL_CONFIGS[c // (DIL_OUT // LANES)][1]
        if dil not in DIL_PERMUTED:
            ref[:, sl] = val
            return
        per = tm // dil
        perm_scr[which, c % (DIL_OUT // LANES)] = val
        for r in range(dil):
            ref[r * per:(r + 1) * per, sl] = perm_scr[which, c % (DIL_OUT // LANES),
                                                      pl.ds(r, per, stride=dil), :]

    h = h_scr[...]
    col = 0
    pq = _dot(h, w_ref[:, col:col + DIL_WIDTH])
    for c in range(DIL_WIDTH // LANES):
        store_dilated(qa_ref, 0, c, rope(pq[:, c * LANES:(c + 1) * LANES]) * scale)
    col += DIL_WIDTH
    pk = _dot(h, w_ref[:, col:col + DIL_WIDTH])
    for c in range(DIL_WIDTH // LANES):
        store_dilated(ka_ref, 1, c, rope(pk[:, c * LANES:(c + 1) * LANES]))
    col += DIL_WIDTH
    pv = _dot(h, w_ref[:, col:col + DIL_WIDTH])
    for c in range(DIL_WIDTH // LANES):
        store_dilated(va_ref, 2, c, pv[:, c * LANES:(c + 1) * LANES])
    col += DIL_WIDTH
    qb_ref[...] = (_dot(h, w_ref[:, col:col + SB_WIDTH]) * scale).astype(BF16)
    col += SB_WIDTH
    kb_ref[...] = _dot(h, w_ref[:, col:col + SB_WIDTH]).astype(BF16)
    col += SB_WIDTH
    vb_ref[...] = _dot(h, w_ref[:, col:col + SB_WIDTH]).astype(BF16)
    col += SB_WIDTH
    uc_ref[...] = _dot(h, w_ref[:, col:col + POOL_WIDTH])


def _inproj(x2d, g, w_in, layer, cos_t, sin_t, expert_ws):
    t, d = x2d.shape
    tm = INPROJ_TILE
    steps = t // tm
    row = lambda width: pl.BlockSpec((tm, width), lambda i: (i, 0))
    widths = (DIL_WIDTH,) * 3 + (SB_WIDTH,) * 3 + (POOL_WIDTH,)
    dtypes = (F32,) * 3 + (BF16,) * 3 + (F32,)
    flat = [w.reshape(w.shape[0], -1, w.shape[-1]) for w in expert_ws]
    slabs = [(w.shape[1] // steps, w.shape[2]) for w in flat]
    outs = pl.pallas_call(
        _inproj_kernel,
        grid=(steps,),
        in_specs=[row(d), _const_spec((1, d)), _layer_weight_spec(w_in, layer),
                  row(LANES), row(LANES)]
                 + [pl.BlockSpec((None,) + s, lambda i: (layer, i, 0)) for s in slabs],
        out_specs=[row(w) for w in widths]
                  + [pl.BlockSpec(s, lambda i: (i, 0)) for s in slabs],
        out_shape=[jax.ShapeDtypeStruct((t, w), dt) for w, dt in zip(widths, dtypes)]
                  + [jax.ShapeDtypeStruct(w.shape[1:], BF16) for w in flat],
        scratch_shapes=[pltpu.VMEM((tm, d), BF16),
                        pltpu.VMEM((3, DIL_OUT // LANES, tm, LANES), F32),
                        pltpu.VMEM(w_in.shape[1:], BF16)],
        compiler_params=_cparams(1),
        name="in_proj",
    )(x2d, g, w_in, cos_t, sin_t, *flat)
    experts_bf16 = [o.reshape(w.shape[1:]) for o, w in zip(outs[len(widths):], expert_ws)]
    return outs[:len(widths)], experts_bf16


def _dil_kernel(*refs):
    n_grp = len(DIL_CONFIGS)
    out_ref, o_scr, l_scr = refs[5 * n_grp:]
    for g, (_, dil) in enumerate(DIL_CONFIGS):
        _dil_group(dil, *refs[5 * g:5 * g + 5], o_scr.at[g], l_scr.at[g])
    lses = [l_scr[g] for g in range(n_grp)]
    m = functools.reduce(jnp.maximum, lses)
    es = [jnp.exp(l - m) for l in lses]
    num = sum(e * o_scr[g] for g, e in enumerate(es))
    out_ref[...] = (num / sum(es)).astype(BF16)


def _dil_group(dil, q_ref, kp_ref, kc_ref, vp_ref, vc_ref, o_ref, lse_ref):
    has_prev = pl.program_id(1) > 0
    n = DIL_BLOCK
    n_blocks = q_ref.shape[0] // (n * dil)
    qi = lax.broadcasted_iota(jnp.int32, (n, n), 0)
    kj = lax.broadcasted_iota(jnp.int32, (n, n), 1)
    mask_prev = kj >= qi
    mask_prev_first = jnp.logical_and(mask_prev, has_prev)
    mask_cur = kj <= qi
    lo_head = kj < HEAD_DIM
    ones = jnp.ones((2 * n, LANES), BF16)

    def rows(ref, r, j):
        if dil in DIL_PERMUTED:
            per = INPROJ_TILE // dil
            pieces = [ref[ti * INPROJ_TILE + r * per:ti * INPROJ_TILE + (r + 1) * per, :]
                      for ti in range(j * (n // per), (j + 1) * (n // per))]
            return jnp.concatenate(pieces, axis=0).astype(BF16)
        start = j * n * dil + r
        return ref[pl.ds(start, n, stride=dil) if dil > 1 else pl.ds(start, n), :].astype(BF16)

    units = {}

    def unit(r, j):
        if (r, j) not in units:
            kc, vc = rows(kc_ref, r, j), rows(vc_ref, r, j)
            if j == 0:
                kp, vp = rows(kp_ref, r, 0), rows(vp_ref, r, 0)
            else:
                kp, vp = units[(r, j - 1)]["kc"], units[(r, j - 1)]["vc"]
            vo = jnp.concatenate([jnp.concatenate([vp, vc], axis=0), ones], axis=1)
            units[(r, j)] = dict(q=rows(q_ref, r, j), kc=kc, vc=vc, kp=kp, vo=vo,
                                 pm=mask_prev_first if j == 0 else mask_prev,
                                 start=j * n * dil + r)
        return units[(r, j)]

    def stage_scores(c):
        u = unit(c["r"], c["j"])
        sel = lo_head if c["hh"] == 0 else jnp.logical_not(lo_head)
        qm = jnp.where(sel, u["q"], jnp.zeros((n, LANES), BF16))
        c["sp"] = _dot_nt(qm, u["kp"])
        c["sc"] = _dot_nt(qm, u["kc"])

    def stage_max(c):
        u = unit(c["r"], c["j"])
        c["sp"] = jnp.where(u["pm"], c["sp"], NEG_BIG)
        c["sc"] = jnp.where(mask_cur, c["sc"], NEG_BIG)
        c["m"] = jnp.max(jnp.maximum(c["sp"], c["sc"]), axis=-1, keepdims=True)

    def stage_pv(c):
        u = unit(c["r"], c["j"])
        ep = jnp.exp(c.pop("sp") - c["m"]).astype(BF16)
        ec = jnp.exp(c.pop("sc") - c["m"]).astype(BF16)
        res = _dot(jnp.concatenate([ep, ec], axis=1), u["vo"])
        c["acc"] = res[:, :LANES]
        c["den"] = res[:, LANES:]

    def stage_out(c, pairs):
        c["o"] = c.pop("acc") / c["den"]
        c["lse"] = c["m"] + jnp.log(c.pop("den"))
        key = (c["r"], c["j"])
        pairs.setdefault(key, []).append(c)
        if len(pairs[key]) == 2:
            c0, c1 = pairs.pop(key)
            u = unit(c["r"], c["j"])
            idx = pl.ds(u["start"], n, stride=dil) if dil > 1 else pl.ds(u["start"], n)
            o_ref[idx, :] = jnp.where(lo_head, c0["o"], c1["o"])
            lse_ref[idx, :] = jnp.where(lo_head, c0["lse"], c1["lse"])

    chains = [dict(r=r, j=j, hh=hh) for r in range(dil) for j in range(n_blocks)
              for hh in range(2)]
    stages = (stage_scores, stage_max, stage_pv)
    pairs = {}
    total = len(chains)
    for step in range(total + len(stages) * DIL_SKEW):
        for s, fn in enumerate(stages):
            i = step - s * DIL_SKEW
            if 0 <= i < total:
                fn(chains[i])
        i = step - len(stages) * DIL_SKEW
        if 0 <= i < total:
            stage_out(chains[i], pairs)


def _dilated(qa, ka, va, batch, seq):
    assert seq % DIL_CHUNK == 0
    nc = seq // DIL_CHUNK
    t = batch * seq
    n_pairs = DIL_OUT // LANES
    in_specs = []
    for g, (window, dil) in enumerate(DIL_CONFIGS):
        assert window // dil == DIL_BLOCK
        hist = DIL_BLOCK * dil
        cur = pl.BlockSpec((DIL_CHUNK, LANES),
                           lambda b, c, pr, g=g: (b * nc + c, g * n_pairs + pr))
        prev = pl.BlockSpec(
            (hist, LANES),
            lambda b, c, pr, g=g, hist=hist: (
                jnp.maximum((b * nc + c) * (DIL_CHUNK // hist) - 1, 0), g * n_pairs + pr))
        in_specs += [cur, prev, cur, prev, cur]
    n_grp = len(DIL_CONFIGS)
    return pl.pallas_call(
        _dil_kernel,
        grid=(batch, nc, n_pairs),
        in_specs=in_specs,
        out_specs=pl.BlockSpec((DIL_CHUNK, LANES), lambda b, c, pr: (b * nc + c, pr)),
        out_shape=jax.ShapeDtypeStruct((t, DIL_OUT), BF16),
        scratch_shapes=[pltpu.VMEM((n_grp, DIL_CHUNK, LANES), F32)] * 2,
        compiler_params=_cparams(3),
        name="dilated",
    )(*([qa, ka, ka, va, va] * n_grp))


def _sb_kernel(q_ref, k_ref, v_ref, o_ref, acc_scr, r_scr):
    blk = SB_BLOCK
    nq = q_ref.shape[0] // blk
    row = lax.broadcasted_iota(jnp.int32, (blk, blk), 0)
    col = lax.broadcasted_iota(jnp.int32, (blk, blk), 1)
    tri = col < row
    lo_head = col < HEAD_DIM
    neg_upper = jnp.where(row > col, -1.0, 0.0).astype(BF16)
    neg_cum = jnp.concatenate([neg_upper, neg_upper], axis=0)
    tri2 = jnp.concatenate([tri, tri], axis=0)

    def head_q(q, hh):
        sel = lo_head if hh == 0 else jnp.logical_not(lo_head)
        return jnp.where(sel, q, jnp.zeros_like(q))

    def tile_scores(z, mask):
        sp = jnp.maximum(z, 0.0) + jnp.log(1.0 + jnp.exp(-jnp.abs(z)))
        spm = sp if mask is None else jnp.where(mask, sp, 0.0)
        hi = spm.astype(BF16)
        lo = (spm - hi.astype(F32)).astype(BF16)
        after = _dot(jnp.concatenate([hi, lo], axis=1), neg_cum)
        return z - sp + after, -jnp.sum(spm, axis=-1, keepdims=True)

    def tile_weights(logit, mask, r_acc):
        a = jnp.exp(logit + r_acc)
        if mask is not None:
            a = jnp.where(mask, a, 0.0)
        return a.astype(BF16)

    def q_group(blocks):
        chains = []
        for qi, n_tiles in blocks:
            qoff = qi * blk
            koff = (qi - (n_tiles - 1)) * blk
            if not isinstance(qi, int):
                qoff, koff = pl.multiple_of(qoff, blk), pl.multiple_of(koff, blk)
            q = q_ref[pl.ds(qoff, blk), :]
            kw = k_ref[pl.ds(koff, n_tiles * blk), :]
            vw = v_ref[pl.ds(koff, n_tiles * blk), :]
            q2 = jnp.concatenate([head_q(q, 0), head_q(q, 1)], axis=0)
            chains.append(dict(z=_dot_nt(q2, kw), vw=vw, n=n_tiles, qi=qi,
                               r=jnp.zeros((2 * blk, 1), F32),
                               acc=jnp.zeros((2 * blk, blk), F32)))
        max_tiles = max(c["n"] for c in chains)
        far = SB_FAR_ROWS

        def far_rows(a):
            return jnp.concatenate([a[:far], a[blk:blk + far]], axis=0)

        def add_far(a, upd):
            return jnp.concatenate([a[:far] + upd[:far], a[far:blk],
                                    a[blk:blk + far] + upd[far:], a[blk + far:]], axis=0)

        tiles = [(c, c["n"] - 1 - j) for j in range(max_tiles) for c in chains if j < c["n"]]
        pending = []
        for step in range(len(tiles) + SB_SKEW):
            if step < len(tiles):
                c, w = tiles[step]
                mask = tri2 if w == c["n"] - 1 else None
                is_far = w == 0 and c["n"] == SB_WINDOW
                zt = c["z"][:, w * blk:(w + 1) * blk]
                logit, rowsum = tile_scores(far_rows(zt) if is_far else zt, mask)
                pending.append((c, w, mask, logit, rowsum, is_far))
            if step >= SB_SKEW:
                c, w, mask, logit, rowsum, is_far = pending[step - SB_SKEW]
                a = tile_weights(logit, mask, far_rows(c["r"]) if is_far else c["r"])
                pv = _dot(a, c["vw"][w * blk:(w + 1) * blk, :])
                if is_far:
                    c["acc"] = add_far(c["acc"], pv)
                    c["r"] = add_far(c["r"], rowsum)
                else:
                    c["acc"] = c["acc"] + pv
                    c["r"] = c["r"] + rowsum
        for c in chains:
            qi = c["qi"]
            qoff = qi * blk if isinstance(qi, int) else pl.multiple_of(qi * blk, blk)
            r_scr[0, pl.ds(qoff, blk), :] = jnp.broadcast_to(c["r"][:blk], (blk, blk))
            r_scr[1, pl.ds(qoff, blk), :] = jnp.broadcast_to(c["r"][blk:], (blk, blk))
            acc_pair = jnp.where(lo_head, c["acc"][:blk], c["acc"][blk:])
            acc_scr[pl.ds(qoff, blk), :] = acc_pair
            o_ref[pl.ds(qoff, blk), :] = acc_pair.astype(BF16)

    first_tail = min(SB_WINDOW - 1, nq)
    n_static = SB_WINDOW + (nq - SB_WINDOW) % SB_GROUP
    q_group([(qi, min(qi + 1, SB_WINDOW)) for qi in range(n_static)])

    def q_loop(it, carry):
        q0 = n_static + it * SB_GROUP
        q_group([(q0 + j, SB_WINDOW) for j in range(SB_GROUP)])
        return carry

    lax.fori_loop(0, (nq - n_static) // SB_GROUP, q_loop, 0)


    @pl.when(jnp.max(r_scr[:, first_tail * blk:, :]) > SB_DEAD_LOG)
    def _tail():
        def q_tail(qi, carry):
            qoff = pl.multiple_of(qi * blk, blk)
            q = q_ref[pl.ds(qoff, blk), :]
            acc0 = acc_scr[pl.ds(qoff, blk), :]
            res = []
            for hh in range(2):
                qm = head_q(q, hh)
                r0 = r_scr[hh, pl.ds(qoff, blk), :]

                def cond(c):
                    return jnp.logical_and(c[0] >= 0, c[3] > SB_DEAD_LOG)

                def body(c, qm=qm):
                    off = pl.multiple_of(c[0] * blk, blk)
                    z = _dot_nt(qm, k_ref[pl.ds(off, blk), :])
                    todo = jnp.logical_or(row >= SB_FAR_ROWS, c[0] != qi - (SB_WINDOW - 1))
                    logit, rowsum = tile_scores(z, todo)
                    a = tile_weights(logit, todo, c[1])
                    r_new = c[1] + rowsum
                    acc_new = c[2] + _dot(a, v_ref[pl.ds(off, blk), :])
                    return c[0] - 1, r_new, acc_new, jnp.max(r_new)

                c0 = (qi - (SB_WINDOW - 1), r0, acc0, jnp.max(r0))
                res.append(lax.while_loop(cond, body, c0)[2])
            o_ref[pl.ds(qoff, blk), :] = jnp.where(lo_head, res[0], res[1]).astype(BF16)
            return carry

        lax.fori_loop(first_tail, nq, q_tail, 0)


def _stick_breaking(qb, kb, vb, batch, seq):
    view = lambda a: a.reshape(batch, seq, SB_WIDTH)
    spec = pl.BlockSpec((None, seq, LANES), lambda b, hp: (b, 0, hp))
    o = pl.pallas_call(
        _sb_kernel,
        grid=(batch, SB_WIDTH // LANES),
        in_specs=[spec, spec, spec],
        out_specs=spec,
        out_shape=jax.ShapeDtypeStruct((batch, seq, SB_WIDTH), BF16),
        scratch_shapes=[pltpu.VMEM((seq, LANES), F32), pltpu.VMEM((2, seq, LANES), F32)],
        compiler_params=_cparams(2),
        name="stick_breaking",
    )(view(qb), view(kb), view(vb))
    return o.reshape(batch * seq, SB_WIDTH)


def _merge_kernel(seq, x_ref, dil_ref, sb_ref,
                  uc_ref, uh_ref, g_ref, wg32_ref, bg_ref, wp32_ref, ps_ref,
                  wa32_ref, wb32_ref, wc32_ref, wo32_ref, out_ref,
                  h_scr, ub_scr, yc_scr, m_scr, wg_ref, wp_ref, wa_ref, wb_ref, wc_ref, wo_ref):
    i = pl.program_id(0)
    _cast_weights_once(i == 0, [(wg32_ref, wg_ref), (wp32_ref, wp_ref), (wa32_ref, wa_ref),
                                (wb32_ref, wb_ref), (wc32_ref, wc_ref), (wo32_ref, wo_ref)])
    tm, d = x_ref.shape
    x = x_ref[...]
    h_scr[...] = _rms(x, g_ref[...]).astype(BF16)
    cw = 256

    def gate_chunk(c):
        gates = []
        for br in range(N_BRANCH):
            gsl = slice(br * d + c * cw, br * d + (c + 1) * cw)
            gates.append(jax.nn.sigmoid(_dot(h_scr[...], wg_ref[:, gsl]) + bg_ref[:, gsl]))
        return gates

    dil = dil_ref[...]

    row0 = (i * tm) % seq
    halo = jnp.where(row0 == 0, 0.0, uh_ref[...])
    ub_scr[0:POOL_HALO, :] = halo
    ub_scr[POOL_HALO:POOL_HALO + tm, :] = uc_ref[...]
    t_in_seq = row0 + lax.broadcasted_iota(jnp.int32, (tm, 1), 0)
    for g, w in enumerate(POOL_WINDOWS):
        sl = slice(g * POOL_GROUP, (g + 1) * POOL_GROUP)
        u = ub_scr[POOL_HALO:POOL_HALO + tm, sl]
        acc = u
        for j in range(1, w):
            acc = acc + ub_scr[POOL_HALO - j:POOL_HALO - j + tm, sl]
        cnt = jnp.minimum(t_in_seq + 1, w).astype(F32)
        pooled = acc / cnt - u
        y = _dot(pooled.astype(BF16), wp_ref[g]) * ps_ref[:, sl]
        yc_scr[:, sl] = y.astype(BF16)

    sb = sb_ref[...]
    yc_in = yc_scr[...]
    for c in range(d // cw):
        sl = slice(c * cw, (c + 1) * cw)
        gates = gate_chunk(c)
        merged = (gates[0] * _dot(dil, wa_ref[:, sl]) + gates[1] * _dot(sb, wb_ref[:, sl])
                  + gates[2] * _dot(yc_in, wc_ref[:, sl]))
        m_scr[:, sl] = merged.astype(BF16)
    out_ref[...] = x + _dot(m_scr[...], wo_ref[...])


def _merge(x2d, dil_mix, sb_out, uc, seq, layer, g, w_gate, b_gate, w_pool, pool_scale,
           w_up_a, w_up_b, w_up_c, w_out):
    t, d = x2d.shape
    tm = MERGE_TILE
    row = lambda width: pl.BlockSpec((tm, width), lambda i: (i, 0))
    halo = pl.BlockSpec((POOL_HALO, POOL_WIDTH),
                        lambda i: (jnp.maximum(i * (tm // POOL_HALO) - 1, 0), 0))
    big = (w_gate, w_pool, w_up_a, w_up_b, w_up_c, w_out)
    lw = lambda w: _layer_weight_spec(w, layer)
    return pl.pallas_call(
        functools.partial(_merge_kernel, seq),
        grid=(t // tm,),
        in_specs=[row(d), row(DIL_OUT), row(SB_WIDTH), row(POOL_WIDTH), halo]
                 + [_const_spec(g.shape), lw(w_gate), _const_spec(b_gate.shape), lw(w_pool),
                    _const_spec(pool_scale.shape), lw(w_up_a), lw(w_up_b), lw(w_up_c),
                    lw(w_out)],
        out_specs=row(d),
        out_shape=jax.ShapeDtypeStruct((t, d), F32),
        scratch_shapes=[pltpu.VMEM((tm, d), BF16),
                        pltpu.VMEM((POOL_HALO + tm, POOL_WIDTH), F32),
                        pltpu.VMEM((tm, POOL_WIDTH), BF16),
                        pltpu.VMEM((tm, d), BF16)]
                       + [pltpu.VMEM(w.shape[1:], BF16) for w in big],
        compiler_params=_cparams(1),
        name="merge",
    )(x2d, dil_mix, sb_out, uc, uc, g, w_gate, b_gate, w_pool, pool_scale,
      w_up_a, w_up_b, w_up_c, w_out)


ROUTER_EXP_LANE0 = MOE_GROUPS
MOE_TILE = 1024
MOE_SUB = 256
MOE_SLOTS = 80


def _moe_kernel(final, x_ref, g_ref, wr_ref, br_ref, wg_ref, wu_ref, wd_ref,
                p_ref, gp_ref, wpg32_ref, wpi32_ref, gf_ref, out_ref,
                xn_scr, gate_scr, ghi_scr, glo_scr, sel_scr, selt_scr, acc_scr, ovf_smem,
                wpg_ref, wpi_ref):
    grp = pl.program_id(1)
    _cast_weights_once(jnp.logical_and(pl.program_id(0) == 0, grp == 0),
                       [(wpg32_ref, wpg_ref), (wpi32_ref, wpi_ref)])
    tm, d = x_ref.shape
    n_sub = tm // MOE_SUB
    lane_f = lax.broadcasted_iota(jnp.int32, (tm, LANES), 1).astype(F32)

    @pl.when(grp == 0)
    def _route():
        xn = _rms(x_ref[...], g_ref[...])
        a_hi = xn.astype(BF16)
        xn_scr[...] = a_hi
        a_lo = (xn - a_hi.astype(F32)).astype(BF16)
        r_hi = _dot(a_hi, wr_ref[...])
        logits = (r_hi[:, :LANES] + r_hi[:, LANES:] + _dot(a_lo, wr_ref[:, :LANES])
                  + br_ref[...])
        is_grp = lane_f < MOE_GROUPS
        lg = jnp.where(is_grp, logits, NEG_BIG)
        mg = jnp.max(lg, axis=-1, keepdims=True)
        p1 = 1.0 / jnp.sum(jnp.exp(lg - mg), axis=-1, keepdims=True)
        gi = jnp.min(jnp.where(lg == mg, lane_f, float(LANES)), axis=-1, keepdims=True)
        lane_grp = jnp.floor((lane_f - ROUTER_EXP_LANE0) * (1.0 / EXP_PER_GROUP))
        in_grp = jnp.logical_and(
            jnp.logical_and(lane_f >= ROUTER_EXP_LANE0, lane_f < ROUTER_EXP_LANE0 + N_EXPERTS),
            lane_grp == gi)
        le = jnp.where(in_grp, logits, NEG_BIG)
        v1 = jnp.max(le, axis=-1, keepdims=True)
        i1 = jnp.min(jnp.where(le == v1, lane_f, float(LANES)), axis=-1, keepdims=True)
        le2 = jnp.where(lane_f == i1, NEG_BIG, le)
        v2 = jnp.max(le2, axis=-1, keepdims=True)
        i2 = jnp.min(jnp.where(le2 == v2, lane_f, float(LANES)), axis=-1, keepdims=True)
        t = jnp.exp(v2 - v1)
        w1 = p1 / (1.0 + t)
        w2 = p1 * t / (1.0 + t)
        gate = jnp.where(lane_f == i1, w1, 0.0) + jnp.where(lane_f == i2, w2, 0.0)
        gate_scr[...] = gate
        g_hi = gate.astype(BF16)
        ghi_scr[...] = g_hi
        glo_scr[...] = (gate - g_hi.astype(F32)).astype(BF16)
        acc_scr[...] = jnp.zeros_like(acc_scr)

        onehot = jnp.where(lane_f == gi, 1.0, 0.0).astype(BF16)
        r = lax.broadcasted_iota(jnp.int32, (MOE_SUB, MOE_SUB), 0)
        c = lax.broadcasted_iota(jnp.int32, (MOE_SUB, MOE_SUB), 1)
        earlier = jnp.where(c < r, 1.0, 0.0).astype(BF16)
        fullest = jnp.zeros((1, LANES), F32)
        gi_b = jnp.broadcast_to(gi, (tm, LANES))
        for s in range(n_sub):
            rs = slice(s * MOE_SUB, (s + 1) * MOE_SUB)
            oh = onehot[rs, :]
            ahead = _dot(earlier, oh)
            slot = jnp.sum(jnp.where(oh > 0, ahead, 0.0), axis=-1, keepdims=True)
            sel_b = jnp.where(slot < LANES, slot + float(LANES) * gi_b[rs, :], -1.0)
            sel_scr[rs, :] = sel_b
            selt_scr[:, rs] = jnp.transpose(sel_b)[:8, :]
            fullest = jnp.maximum(fullest, jnp.sum(oh.astype(F32), axis=0, keepdims=True))
        lane1 = lax.broadcasted_iota(jnp.int32, (1, LANES), 1)
        for gidx in range(MOE_GROUPS):
            most = jnp.max(jnp.where(lane1 == gidx, fullest, 0.0))
            ovf_smem[gidx] = (most > MOE_SLOTS).astype(jnp.int32)

    def experts(rows, gates):
        lane_r = lax.broadcasted_iota(jnp.int32, gates.shape, 1)
        y = None
        for j in range(EXP_PER_GROUP):
            hg = _dot(rows, wg_ref[j])
            hu = _dot(rows, wu_ref[j])
            here = lane_r == ROUTER_EXP_LANE0 + EXP_PER_GROUP * grp + j
            gcol = jnp.sum(jnp.where(here, gates, 0.0), axis=-1, keepdims=True)
            hid = ((hg * jax.nn.sigmoid(hg)) * hu * gcol).astype(BF16)
            yj = _dot(hid, wd_ref[j])
            y = yj if y is None else y + yj
        return y

    key = (grp * LANES).astype(F32)

    @pl.when(ovf_smem[grp] == 0)
    def _compact():
        slot_rows = lax.broadcasted_iota(jnp.int32, (MOE_SLOTS, MOE_SUB), 0).astype(F32) + key
        slot_cols = lax.broadcasted_iota(jnp.int32, (MOE_SUB, LANES), 1).astype(F32) + key
        xs, gs = [], []
        for s in range(n_sub):
            rs = slice(s * MOE_SUB, (s + 1) * MOE_SUB)
            pick = jnp.where(selt_scr[0:1, rs] == slot_rows, 1.0, 0.0).astype(BF16)
            xs.append(_dot(pick, xn_scr[rs, :]).astype(BF16))
            gs.append(_dot(pick, ghi_scr[rs, :]) + _dot(pick, glo_scr[rs, :]))
        y = experts(jnp.concatenate(xs, axis=0), jnp.concatenate(gs, axis=0))
        y = jnp.concatenate([y, jnp.zeros((LANES - MOE_SLOTS, d), F32)], axis=0)
        y_hi = y.astype(BF16)
        y_lo = (y - y_hi.astype(F32)).astype(BF16)
        for s in range(n_sub):
            rs = slice(s * MOE_SUB, (s + 1) * MOE_SUB)
            ys = slice(s * MOE_SLOTS, s * MOE_SLOTS + LANES)
            place = jnp.where(sel_scr[rs, :] == slot_cols, 1.0, 0.0).astype(BF16)
            acc_scr[rs, :] += _dot(jnp.concatenate([place, place], axis=1),
                                   jnp.concatenate([y_hi[ys, :], y_lo[ys, :]], axis=0))

    @pl.when(ovf_smem[grp] != 0)
    def _dense():
        for s in range(n_sub):
            rs = slice(s * MOE_SUB, (s + 1) * MOE_SUB)
            acc_scr[rs, :] += experts(xn_scr[rs, :], gate_scr[rs, :])

    @pl.when(grp == MOE_GROUPS - 1)
    def _finish():
        for s in range(n_sub):
            rs = slice(s * MOE_SUB, (s + 1) * MOE_SUB)
            x = x_ref[rs, :] + acc_scr[rs, :]
            gate = jax.nn.sigmoid(_dot(_rms(x, gp_ref[...]).astype(BF16), wpg_ref[...]))
            y = x + _dot(p_ref[rs, :].astype(BF16), wpi_ref[...]) * gate
            if final:
                y = _rms(y, gf_ref[...])
            out_ref[rs, :] = y


def _moe_ple(x2d, g, w_router, b_router, w_exp_gate, w_exp_up, w_exp_down,
             p2d, layer, g_ple, w_ple_gate, w_ple_in, g_final, final):
    t, d = x2d.shape
    f = w_exp_gate.shape[-1]
    tm = MOE_TILE
    row = pl.BlockSpec((tm, d), lambda i, e: (i, 0))
    group_w = lambda a, b: pl.BlockSpec((EXP_PER_GROUP, a, b), lambda i, e: (e, 0, 0))
    return pl.pallas_call(
        functools.partial(_moe_kernel, final),
        grid=(t // tm, MOE_GROUPS),
        in_specs=[row, _const_spec((1, d)), _const_spec(w_router.shape),
                  _const_spec(b_router.shape),
                  group_w(d, f), group_w(d, f), group_w(f, d),
                  pl.BlockSpec((tm, p2d.shape[1]), lambda i, e: (layer * (t // tm) + i, 0)),
                  _const_spec((1, d)), _layer_weight_spec(w_ple_gate, layer),
                  _layer_weight_spec(w_ple_in, layer), _const_spec((1, d))],
        out_specs=row,
        out_shape=jax.ShapeDtypeStruct((t, d), F32),
        scratch_shapes=[pltpu.VMEM((tm, d), BF16), pltpu.VMEM((tm, LANES), F32),
                        pltpu.VMEM((tm, LANES), BF16), pltpu.VMEM((tm, LANES), BF16),
                        pltpu.VMEM((tm, LANES), F32), pltpu.VMEM((8, tm), F32),
                        pltpu.VMEM((tm, d), F32), pltpu.SMEM((MOE_GROUPS,), jnp.int32),
                        pltpu.VMEM(w_ple_gate.shape[1:], BF16),
                        pltpu.VMEM(w_ple_in.shape[1:], BF16)],
        compiler_params=_cparams(2),
        name="moe_ple",
    )(x2d, g, w_router, b_router, w_exp_gate, w_exp_up, w_exp_down,
      p2d, g_ple, w_ple_gate, w_ple_in, g_final)


def _router_weights(w_grp, b_grp, w_exp, b_exp):
    d = w_grp.shape[0]
    w = jnp.concatenate([w_grp, w_exp.transpose(1, 0, 2).reshape(d, N_EXPERTS)], axis=1)
    b = jnp.concatenate([b_grp, b_exp.reshape(N_EXPERTS)])
    pad = LANES - w.shape[1]
    w = jnp.pad(w, ((0, 0), (0, pad)))
    w_hi = w.astype(BF16)
    w_lo = (w - w_hi.astype(F32)).astype(BF16)
    return jnp.concatenate([w_hi, w_lo], axis=1), jnp.pad(b, (0, pad))[None, :]


def kernel(x, p, positions, norm_mix, w_in, w_gate, b_gate, w_pool, pool_scale, w_up_a, w_up_b, w_up_c, w_out, norm_moe, w_router_grp, b_router_grp, w_router_exp, b_router_exp, w_exp_gate, w_exp_up, w_exp_down, norm_ple, w_ple_in, w_ple_gate, norm_final):
    batch, seq, d = x.shape
    depth = w_in.shape[0]
    t = batch * seq
    x2d = x.reshape(t, d)
    cos_t, sin_t = _rope_tables(positions)
    for i in range(depth):
        (qa, ka, va, qb, kb, vb, uc), (e_gate, e_up, e_down) = _inproj(
            x2d, norm_mix[i][None, :], w_in, i, cos_t, sin_t,
            (w_exp_gate, w_exp_up, w_exp_down))
        dil_mix = _dilated(qa, ka, va, batch, seq)
        sb_out = _stick_breaking(qb, kb, vb, batch, seq)
        x2d = _merge(x2d, dil_mix, sb_out, uc, seq, i, norm_mix[i][None, :], w_gate,
                     b_gate[i][None, :], w_pool, pool_scale[i][None, :],
                     w_up_a, w_up_b, w_up_c, w_out)
        w_r, b_r = _router_weights(w_router_grp[i], b_router_grp[i],
                                   w_router_exp[i], b_router_exp[i])
        x2d = _moe_ple(x2d, norm_moe[i][None, :], w_r, b_r, e_gate, e_up, e_down,
                       p.reshape(depth * t, -1), i,
                       norm_ple[i][None, :], w_ple_gate, w_ple_in,
                       norm_final[None, :], i == depth - 1)
    return x2d.reshape(batch, seq, d)
```

```python
import functools

import jax
import jax.numpy as jnp
from jax import lax
from jax.experimental import pallas as pl
from jax.experimental.pallas import tpu as pltpu

F32 = jnp.float32
BF16 = jnp.bfloat16

HEAD_DIM = 64
ROPE_THETA = 10000.0
DIL_CONFIGS = ((128, 1), (512, 4), (2048, 16))
DIL_HEADS = 4
DIL_OUT = DIL_HEADS * HEAD_DIM
DIL_WIDTH = len(DIL_CONFIGS) * DIL_OUT
DIL_BLOCK = 128
DIL_CHUNK = 2048
DIL_SKEW = 3
INPROJ_TILE = 512
MERGE_TILE = 512
DIL_PERMUTED = (16,)
SB_HEADS = 8
SB_WIDTH = SB_HEADS * HEAD_DIM
SB_BLOCK = 128
POOL_WINDOWS = (2, 4, 8, 16)
POOL_GROUP = 128
POOL_WIDTH = len(POOL_WINDOWS) * POOL_GROUP
POOL_HALO = 16
N_BRANCH = 3
MOE_GROUPS = 4
EXP_PER_GROUP = 4
N_EXPERTS = MOE_GROUPS * EXP_PER_GROUP
EPS = 1e-6

LANES = 128
NEG_BIG = -1e30
SB_DEAD_LOG = -105.0
SB_WINDOW = 3
SB_GROUP = 28
SB_SKEW = 8
SB_FAR_ROWS = 64
VMEM_LIMIT = 56 * 1024 * 1024


def _cparams(n_axes):
    return pltpu.CompilerParams(
        dimension_semantics=("arbitrary",) * n_axes, vmem_limit_bytes=VMEM_LIMIT)


def _rms(x, g):
    ms = jnp.mean(x * x, axis=-1, keepdims=True)
    return x * lax.rsqrt(ms + EPS) * g


def _dot(a, b):
    return jnp.dot(a, b, preferred_element_type=F32)


def _dot_nt(a, b):
    return lax.dot_general(a, b, (((1,), (1,)), ((), ())), preferred_element_type=F32)


def _const_spec(shape):
    zeros = (0,) * len(shape)
    return pl.BlockSpec(shape, lambda *_: zeros)


def _layer_weight_spec(w, layer):
    tail = (0,) * (w.ndim - 1)
    return pl.BlockSpec((None,) + w.shape[1:], lambda *_: (layer,) + tail,
                        pipeline_mode=pl.Buffered(1))


def _cast_weights_once(first_step, pairs):
    @pl.when(first_step)
    def _cast():
        for src, dst in pairs:
            if len(src.shape) == 3:
                for k in range(src.shape[0]):
                    dst[k] = src[k].astype(BF16)
            else:
                step = 2 * LANES
                for c in range(0, src.shape[1], step):
                    dst[:, c:c + step] = src[:, c:c + step].astype(BF16)


def _rope_table_kernel(pos_ref, inv_ref, sgn_ref, cos_ref, sin_ref):
    ang = inv_ref[...] * pos_ref[...].astype(F32)
    reps = LANES // ang.shape[0]
    cos_ref[...] = jnp.transpose(jnp.concatenate([jnp.cos(ang)] * reps, axis=0))
    sin_ref[...] = jnp.transpose(jnp.concatenate([jnp.sin(ang)] * reps, axis=0) * sgn_ref[...])


def _rope_tables(positions):
    t = positions.size
    half = HEAD_DIM // 2
    inv_col = (ROPE_THETA ** (-jnp.arange(half, dtype=F32) / half))[:, None]
    sgn_col = jnp.tile(jnp.concatenate([-jnp.ones(half, F32), jnp.ones(half, F32)]),
                       LANES // HEAD_DIM)[:, None]
    tm = 2048
    return pl.pallas_call(
        _rope_table_kernel,
        grid=(t // tm,),
        in_specs=[pl.BlockSpec((None, 1, tm), lambda i: (i, 0, 0)),
                  _const_spec((half, 1)), _const_spec((LANES, 1))],
        out_specs=[pl.BlockSpec((tm, LANES), lambda i: (i, 0))] * 2,
        out_shape=[jax.ShapeDtypeStruct((t, LANES), F32)] * 2,
        compiler_params=_cparams(1),
        name="rope_table",
    )(positions.reshape(t // tm, 1, tm), inv_col, sgn_col)


def _inproj_kernel(x_ref, g_ref, w32_ref, cos_ref, sin_ref, eg32_ref, eu32_ref, ed32_ref,
                   qa_ref, ka_ref, va_ref, qb_ref, kb_ref, vb_ref, uc_ref,
                   eg_ref, eu_ref, ed_ref, h_scr, perm_scr, w_ref):
    _cast_weights_once(pl.program_id(0) == 0, [(w32_ref, w_ref)])
    eg_ref[...] = eg32_ref[...].astype(BF16)
    eu_ref[...] = eu32_ref[...].astype(BF16)
    ed_ref[...] = ed32_ref[...].astype(BF16)
    h_scr[...] = _rms(x_ref[...], g_ref[...]).astype(BF16)
    cos = cos_ref[...]
    sin = sin_ref[...]
    tm = cos.shape[0]
    lane = lax.broadcasted_iota(jnp.int32, (tm, LANES), 1)
    first_half = (lane & (HEAD_DIM // 2)) == 0
    scale = HEAD_DIM ** -0.5

    def rope(t):
        partner = jnp.where(first_half, pltpu.roll(t, LANES - HEAD_DIM // 2, 1),
                            pltpu.roll(t, HEAD_DIM // 2, 1))
        return t * cos + partner * sin

    def store_dilated(ref, which, c, val):
        sl = slice(c * LANES, (c + 1) * LANES)
        dil = DIL_CONFIGS[c // (DIL_OUT // LANES)][1]
        if dil not in DIL_PERMUTED:
            ref[:, sl] = val
            return
        per = tm // dil
        perm_scr[which, c % (DIL_OUT // LANES)] = val
        for r in range(dil):
            ref[r * per:(r + 1) * per, sl] = perm_scr[which, c % (DIL_OUT // LANES),
                                                      pl.ds(r, per, stride=dil), :]

    h = h_scr[...]
    col = 0
    pq = _dot(h, w_ref[:, col:col + DIL_WIDTH])
    for c in range(DIL_WIDTH // LANES):
        store_dilated(qa_ref, 0, c, rope(pq[:, c * LANES:(c + 1) * LANES]) * scale)
    col += DIL_WIDTH
    pk = _dot(h, w_ref[:, col:col + DIL_WIDTH])
    for c in range(DIL_WIDTH // LANES):
        store_dilated(ka_ref, 1, c, rope(pk[:, c * LANES:(c + 1) * LANES]))
    col += DIL_WIDTH
    pv = _dot(h, w_ref[:, col:col + DIL_WIDTH])
    for c in range(DIL_WIDTH // LANES):
        store_dilated(va_ref, 2, c, pv[:, c * LANES:(c + 1) * LANES])
    col += DIL_WIDTH
    qb_ref[...] = (_dot(h, w_ref[:, col:col + SB_WIDTH]) * scale).astype(BF16)
    col += SB_WIDTH
    kb_ref[...] = _dot(h, w_ref[:, col:col + SB_WIDTH]).astype(BF16)
    col += SB_WIDTH
    vb_ref[...] = _dot(h, w_ref[:, col:col + SB_WIDTH]).astype(BF16)
    col += SB_WIDTH
    uc_ref[...] = _dot(h, w_ref[:, col:col + POOL_WIDTH])


def _inproj(x2d, g, w_in, layer, cos_t, sin_t, expert_ws):
    t, d = x2d.shape
    tm = INPROJ_TILE
    steps = t // tm
    row = lambda width: pl.BlockSpec((tm, width), lambda i: (i, 0))
    widths = (DIL_WIDTH,) * 3 + (SB_WIDTH,) * 3 + (POOL_WIDTH,)
    dtypes = (F32,) * 3 + (BF16,) * 3 + (F32,)
    flat = [w.reshape(w.shape[0], -1, w.shape[-1]) for w in expert_ws]
    slabs = [(w.shape[1] // steps, w.shape[2]) for w in flat]
    outs = pl.pallas_call(
        _inproj_kernel,
        grid=(steps,),
        in_specs=[row(d), _const_spec((1, d)), _layer_weight_spec(w_in, layer),
                  row(LANES), row(LANES)]
                 + [pl.BlockSpec((None,) + s, lambda i: (layer, i, 0)) for s in slabs],
        out_specs=[row(w) for w in widths]
                  + [pl.BlockSpec(s, lambda i: (i, 0)) for s in slabs],
        out_shape=[jax.ShapeDtypeStruct((t, w), dt) for w, dt in zip(widths, dtypes)]
                  + [jax.ShapeDtypeStruct(w.shape[1:], BF16) for w in flat],
        scratch_shapes=[pltpu.VMEM((tm, d), BF16),
                        pltpu.VMEM((3, DIL_OUT // LANES, tm, LANES), F32),
                        pltpu.VMEM(w_in.shape[1:], BF16)],
        compiler_params=_cparams(1),
        name="in_proj",
    )(x2d, g, w_in, cos_t, sin_t, *flat)
    experts_bf16 = [o.reshape(w.shape[1:]) for o, w in zip(outs[len(widths):], expert_ws)]
    return outs[:len(widths)], experts_bf16


def _dil_kernel(*refs):
    n_grp = len(DIL_CONFIGS)
    out_ref, o_scr, l_scr = refs[5 * n_grp:]
    for g, (_, dil) in enumerate(DIL_CONFIGS):
        _dil_group(dil, *refs[5 * g:5 * g + 5], o_scr.at[g], l_scr.at[g])
    lses = [l_scr[g] for g in range(n_grp)]
    m = functools.reduce(jnp.maximum, lses)
    es = [jnp.exp(l - m) for l in lses]
    num = sum(e * o_scr[g] for g, e in enumerate(es))
    out_ref[...] = (num / sum(es)).astype(BF16)


def _dil_group(dil, q_ref, kp_ref, kc_ref, vp_ref, vc_ref, o_ref, lse_ref):
    has_prev = pl.program_id(1) > 0
    n = DIL_BLOCK
    n_blocks = q_ref.shape[0] // (n * dil)
    qi = lax.broadcasted_iota(jnp.int32, (n, n), 0)
    kj = lax.broadcasted_iota(jnp.int32, (n, n), 1)
    mask_prev = kj >= qi
    mask_prev_first = jnp.logical_and(mask_prev, has_prev)
    mask_cur = kj <= qi
    lo_head = kj < HEAD_DIM
    ones = jnp.ones((2 * n, LANES), BF16)

    def rows(ref, r, j):
        if dil in DIL_PERMUTED:
            per = INPROJ_TILE // dil
            pieces = [ref[ti * INPROJ_TILE + r * per:ti * INPROJ_TILE + (r + 1) * per, :]
                      for ti in range(j * (n // per), (j + 1) * (n // per))]
            return jnp.concatenate(pieces, axis=0).astype(BF16)
        start = j * n * dil + r
        return ref[pl.ds(start, n, stride=dil) if dil > 1 else pl.ds(start, n), :].astype(BF16)

    units = {}

    def unit(r, j):
        if (r, j) not in units:
            kc, vc = rows(kc_ref, r, j), rows(vc_ref, r, j)
            if j == 0:
                kp, vp = rows(kp_ref, r, 0), rows(vp_ref, r, 0)
            else:
                kp, vp = units[(r, j - 1)]["kc"], units[(r, j - 1)]["vc"]
            vo = jnp.concatenate([jnp.concatenate([vp, vc], axis=0), ones], axis=1)
            units[(r, j)] = dict(q=rows(q_ref, r, j), kc=kc, vc=vc, kp=kp, vo=vo,
                                 pm=mask_prev_first if j == 0 else mask_prev,
                                 start=j * n * dil + r)
        return units[(r, j)]

    def stage_scores(c):
        u = unit(c["r"], c["j"])
        sel = lo_head if c["hh"] == 0 else jnp.logical_not(lo_head)
        qm = jnp.where(sel, u["q"], jnp.zeros((n, LANES), BF16))
        c["sp"] = _dot_nt(qm, u["kp"])
        c["sc"] = _dot_nt(qm, u["kc"])

    def stage_max(c):
        u = unit(c["r"], c["j"])
        c["sp"] = jnp.where(u["pm"], c["sp"], NEG_BIG)
        c["sc"] = jnp.where(mask_cur, c["sc"], NEG_BIG)
        c["m"] = jnp.max(jnp.maximum(c["sp"], c["sc"]), axis=-1, keepdims=True)

    def stage_pv(c):
        u = unit(c["r"], c["j"])
        ep = jnp.exp(c.pop("sp") - c["m"]).astype(BF16)
        ec = jnp.exp(c.pop("sc") - c["m"]).astype(BF16)
        res = _dot(jnp.concatenate([ep, ec], axis=1), u["vo"])
        c["acc"] = res[:, :LANES]
        c["den"] = res[:, LANES:]

    def stage_out(c, pairs):
        c["o"] = c.pop("acc") / c["den"]
        c["lse"] = c["m"] + jnp.log(c.pop("den"))
        key = (c["r"], c["j"])
        pairs.setdefault(key, []).append(c)
        if len(pairs[key]) == 2:
            c0, c1 = pairs.pop(key)
            u = unit(c["r"], c["j"])
            idx = pl.ds(u["start"], n, stride=dil) if dil > 1 else pl.ds(u["start"], n)
            o_ref[idx, :] = jnp.where(lo_head, c0["o"], c1["o"])
            lse_ref[idx, :] = jnp.where(lo_head, c0["lse"], c1["lse"])

    chains = [dict(r=r, j=j, hh=hh) for r in range(dil) for j in range(n_blocks)
              for hh in range(2)]
    stages = (stage_scores, stage_max, stage_pv)
    pairs = {}
    total = len(chains)
    for step in range(total + len(stages) * DIL_SKEW):
        for s, fn in enumerate(stages):
            i = step - s * DIL_SKEW
            if 0 <= i < total:
                fn(chains[i])
        i = step - len(stages) * DIL_SKEW
        if 0 <= i < total:
            stage_out(chains[i], pairs)


def _dilated(qa, ka, va, batch, seq):
    assert seq % DIL_CHUNK == 0
    nc = seq // DIL_CHUNK
    t = batch * seq
    n_pairs = DIL_OUT // LANES
    in_specs = []
    for g, (window, dil) in enumerate(DIL_CONFIGS):
        assert window // dil == DIL_BLOCK
        hist = DIL_BLOCK * dil
        cur = pl.BlockSpec((DIL_CHUNK, LANES),
                           lambda b, c, pr, g=g: (b * nc + c, g * n_pairs + pr))
        prev = pl.BlockSpec(
            (hist, LANES),
            lambda b, c, pr, g=g, hist=hist: (
                jnp.maximum((b * nc + c) * (DIL_CHUNK // hist) - 1, 0), g * n_pairs + pr))
        in_specs += [cur, prev, cur, prev, cur]
    n_grp = len(DIL_CONFIGS)
    return pl.pallas_call(
        _dil_kernel,
        grid=(batch, nc, n_pairs),
        in_specs=in_specs,
        out_specs=pl.BlockSpec((DIL_CHUNK, LANES), lambda b, c, pr: (b * nc + c, pr)),
        out_shape=jax.ShapeDtypeStruct((t, DIL_OUT), BF16),
        scratch_shapes=[pltpu.VMEM((n_grp, DIL_CHUNK, LANES), F32)] * 2,
        compiler_params=_cparams(3),
        name="dilated",
    )(*([qa, ka, ka, va, va] * n_grp))


def _sb_kernel(q_ref, k_ref, v_ref, o_ref, acc_scr, r_scr):
    blk = SB_BLOCK
    nq = q_ref.shape[0] // blk
    row = lax.broadcasted_iota(jnp.int32, (blk, blk), 0)
    col = lax.broadcasted_iota(jnp.int32, (blk, blk), 1)
    tri = col < row
    lo_head = col < HEAD_DIM
    neg_upper = jnp.where(row > col, -1.0, 0.0).astype(BF16)
    neg_cum = jnp.concatenate([neg_upper, neg_upper], axis=0)
    tri2 = jnp.concatenate([tri, tri], axis=0)

    def head_q(q, hh):
        sel = lo_head if hh == 0 else jnp.logical_not(lo_head)
        return jnp.where(sel, q, jnp.zeros_like(q))

    def tile_scores(z, mask):
        sp = jnp.maximum(z, 0.0) + jnp.log(1.0 + jnp.exp(-jnp.abs(z)))
        spm = sp if mask is None else jnp.where(mask, sp, 0.0)
        hi = spm.astype(BF16)
        lo = (spm - hi.astype(F32)).astype(BF16)
        after = _dot(jnp.concatenate([hi, lo], axis=1), neg_cum)
        return z - sp + after, -jnp.sum(spm, axis=-1, keepdims=True)

    def tile_weights(logit, mask, r_acc):
        a = jnp.exp(logit + r_acc)
        if mask is not None:
            a = jnp.where(mask, a, 0.0)
        return a.astype(BF16)

    def q_group(blocks):
        chains = []
        for qi, n_tiles in blocks:
            qoff = qi * blk
            koff = (qi - (n_tiles - 1)) * blk
            if not isinstance(qi, int):
                qoff, koff = pl.multiple_of(qoff, blk), pl.multiple_of(koff, blk)
            q = q_ref[pl.ds(qoff, blk), :]
            kw = k_ref[pl.ds(koff, n_tiles * blk), :]
            vw = v_ref[pl.ds(koff, n_tiles * blk), :]
            q2 = jnp.concatenate([head_q(q, 0), head_q(q, 1)], axis=0)
            chains.append(dict(z=_dot_nt(q2, kw), vw=vw, n=n_tiles, qi=qi,
                               r=jnp.zeros((2 * blk, 1), F32),
                               acc=jnp.zeros((2 * blk, blk), F32)))
        max_tiles = max(c["n"] for c in chains)
        far = SB_FAR_ROWS

        def far_rows(a):
            return jnp.concatenate([a[:far], a[blk:blk + far]], axis=0)

        def add_far(a, upd):
            return jnp.concatenate([a[:far] + upd[:far], a[far:blk],
                                    a[blk:blk + far] + upd[far:], a[blk + far:]], axis=0)

        tiles = [(c, c["n"] - 1 - j) for j in range(max_tiles) for c in chains if j < c["n"]]
        pending = []
        for step in range(len(tiles) + SB_SKEW):
            if step < len(tiles):
                c, w = tiles[step]
                mask = tri2 if w == c["n"] - 1 else None
                is_far = w == 0 and c["n"] == SB_WINDOW
                zt = c["z"][:, w * blk:(w + 1) * blk]
                logit, rowsum = tile_scores(far_rows(zt) if is_far else zt, mask)
                pending.append((c, w, mask, logit, rowsum, is_far))
            if step >= SB_SKEW:
                c, w, mask, logit, rowsum, is_far = pending[step - SB_SKEW]
                a = tile_weights(logit, mask, far_rows(c["r"]) if is_far else c["r"])
                pv = _dot(a, c["vw"][w * blk:(w + 1) * blk, :])
                if is_far:
                    c["acc"] = add_far(c["acc"], pv)
                    c["r"] = add_far(c["r"], rowsum)
                else:
                    c["acc"] = c["acc"] + pv
                    c["r"] = c["r"] + rowsum
        for c in chains:
            qi = c["qi"]
            qoff = qi * blk if isinstance(qi, int) else pl.multiple_of(qi * blk, blk)
            r_scr[0, pl.ds(qoff, blk), :] = jnp.broadcast_to(c["r"][:blk], (blk, blk))
            r_scr[1, pl.ds(qoff, blk), :] = jnp.broadcast_to(c["r"][blk:], (blk, blk))
            acc_pair = jnp.where(lo_head, c["acc"][:blk], c["acc"][blk:])
            acc_scr[pl.ds(qoff, blk), :] = acc_pair
            o_ref[pl.ds(qoff, blk), :] = acc_pair.astype(BF16)

    first_tail = min(SB_WINDOW - 1, nq)
    n_static = SB_WINDOW + (nq - SB_WINDOW) % SB_GROUP
    q_group([(qi, min(qi + 1, SB_WINDOW)) for qi in range(n_static)])

    def q_loop(it, carry):
        q0 = n_static + it * SB_GROUP
        q_group([(q0 + j, SB_WINDOW) for j in range(SB_GROUP)])
        return carry

    lax.fori_loop(0, (nq - n_static) // SB_GROUP, q_loop, 0)


    @pl.when(jnp.max(r_scr[:, first_tail * blk:, :]) > SB_DEAD_LOG)
    def _tail():
        def q_tail(qi, carry):
            qoff = pl.multiple_of(qi * blk, blk)
            q = q_ref[pl.ds(qoff, blk), :]
            acc0 = acc_scr[pl.ds(qoff, blk), :]
            res = []
            for hh in range(2):
                qm = head_q(q, hh)
                r0 = r_scr[hh, pl.ds(qoff, blk), :]

                def cond(c):
                    return jnp.logical_and(c[0] >= 0, c[3] > SB_DEAD_LOG)

                def body(c, qm=qm):
                    off = pl.multiple_of(c[0] * blk, blk)
                    z = _dot_nt(qm, k_ref[pl.ds(off, blk), :])
                    todo = jnp.logical_or(row >= SB_FAR_ROWS, c[0] != qi - (SB_WINDOW - 1))
                    logit, rowsum = tile_scores(z, todo)
                    a = tile_weights(logit, todo, c[1])
                    r_new = c[1] + rowsum
                    acc_new = c[2] + _dot(a, v_ref[pl.ds(off, blk), :])
                    return c[0] - 1, r_new, acc_new, jnp.max(r_new)

                c0 = (qi - (SB_WINDOW - 1), r0, acc0, jnp.max(r0))
                res.append(lax.while_loop(cond, body, c0)[2])
            o_ref[pl.ds(qoff, blk), :] = jnp.where(lo_head, res[0], res[1]).astype(BF16)
            return carry

        lax.fori_loop(first_tail, nq, q_tail, 0)


def _stick_breaking(qb, kb, vb, batch, seq):
    view = lambda a: a.reshape(batch, seq, SB_WIDTH)
    spec = pl.BlockSpec((None, seq, LANES), lambda b, hp: (b, 0, hp))
    o = pl.pallas_call(
        _sb_kernel,
        grid=(batch, SB_WIDTH // LANES),
        in_specs=[spec, spec, spec],
        out_specs=spec,
        out_shape=jax.ShapeDtypeStruct((batch, seq, SB_WIDTH), BF16),
        scratch_shapes=[pltpu.VMEM((seq, LANES), F32), pltpu.VMEM((2, seq, LANES), F32)],
        compiler_params=_cparams(2),
        name="stick_breaking",
    )(view(qb), view(kb), view(vb))
    return o.reshape(batch * seq, SB_WIDTH)


def _merge_kernel(seq, x_ref, dil_ref, sb_ref,
                  uc_ref, uh_ref, g_ref, wg32_ref, bg_ref, wp32_ref, ps_ref,
                  wa32_ref, wb32_ref, wc32_ref, wo32_ref, out_ref,
                  h_scr, ub_scr, yc_scr, m_scr, wg_ref, wp_ref, wa_ref, wb_ref, wc_ref, wo_ref):
    i = pl.program_id(0)
    _cast_weights_once(i == 0, [(wg32_ref, wg_ref), (wp32_ref, wp_ref), (wa32_ref, wa_ref),
                                (wb32_ref, wb_ref), (wc32_ref, wc_ref), (wo32_ref, wo_ref)])
    tm, d = x_ref.shape
    x = x_ref[...]
    h_scr[...] = _rms(x, g_ref[...]).astype(BF16)
    cw = 256

    def gate_chunk(c):
        gates = []
        for br in range(N_BRANCH):
            gsl = slice(br * d + c * cw, br * d + (c + 1) * cw)
            gates.append(jax.nn.sigmoid(_dot(h_scr[...], wg_ref[:, gsl]) + bg_ref[:, gsl]))
        return gates

    dil = dil_ref[...]

    row0 = (i * tm) % seq
    halo = jnp.where(row0 == 0, 0.0, uh_ref[...])
    ub_scr[0:POOL_HALO, :] = halo
    ub_scr[POOL_HALO:POOL_HALO + tm, :] = uc_ref[...]
    t_in_seq = row0 + lax.broadcasted_iota(jnp.int32, (tm, 1), 0)
    for g, w in enumerate(POOL_WINDOWS):
        sl = slice(g * POOL_GROUP, (g + 1) * POOL_GROUP)
        u = ub_scr[POOL_HALO:POOL_HALO + tm, sl]
        acc = u
        for j in range(1, w):
            acc = acc + ub_scr[POOL_HALO - j:POOL_HALO - j + tm, sl]
        cnt = jnp.minimum(t_in_seq + 1, w).astype(F32)
        pooled = acc / cnt - u
        y = _dot(pooled.astype(BF16), wp_ref[g]) * ps_ref[:, sl]
        yc_scr[:, sl] = y.astype(BF16)

    sb = sb_ref[...]
    yc_in = yc_scr[...]
    for c in range(d // cw):
        sl = slice(c * cw, (c + 1) * cw)
        gates = gate_chunk(c)
        merged = (gates[0] * _dot(dil, wa_ref[:, sl]) + gates[1] * _dot(sb, wb_ref[:, sl])
                  + gates[2] * _dot(yc_in, wc_ref[:, sl]))
        m_scr[:, sl] = merged.astype(BF16)
    out_ref[...] = x + _dot(m_scr[...], wo_ref[...])


def _merge(x2d, dil_mix, sb_out, uc, seq, layer, g, w_gate, b_gate, w_pool, pool_scale,
           w_up_a, w_up_b, w_up_c, w_out):
    t, d = x2d.shape
    tm = MERGE_TILE
    row = lambda width: pl.BlockSpec((tm, width), lambda i: (i, 0))
    halo = pl.BlockSpec((POOL_HALO, POOL_WIDTH),
                        lambda i: (jnp.maximum(i * (tm // POOL_HALO) - 1, 0), 0))
    big = (w_gate, w_pool, w_up_a, w_up_b, w_up_c, w_out)
    lw = lambda w: _layer_weight_spec(w, layer)
    return pl.pallas_call(
        functools.partial(_merge_kernel, seq),
        grid=(t // tm,),
        in_specs=[row(d), row(DIL_OUT), row(SB_WIDTH), row(POOL_WIDTH), halo]
                 + [_const_spec(g.shape), lw(w_gate), _const_spec(b_gate.shape), lw(w_pool),
                    _const_spec(pool_scale.shape), lw(w_up_a), lw(w_up_b), lw(w_up_c),
                    lw(w_out)],
        out_specs=row(d),
        out_shape=jax.ShapeDtypeStruct((t, d), F32),
        scratch_shapes=[pltpu.VMEM((tm, d), BF16),
                        pltpu.VMEM((POOL_HALO + tm, POOL_WIDTH), F32),
                        pltpu.VMEM((tm, POOL_WIDTH), BF16),
                        pltpu.VMEM((tm, d), BF16)]
                       + [pltpu.VMEM(w.shape[1:], BF16) for w in big],
        compiler_params=_cparams(1),
        name="merge",
    )(x2d, dil_mix, sb_out, uc, uc, g, w_gate, b_gate, w_pool, pool_scale,
      w_up_a, w_up_b, w_up_c, w_out)


ROUTER_EXP_LANE0 = MOE_GROUPS
ROUTER_ROWS = 24
MOE_TILE = 1024
MOE_SUB = 256
MOE_SLOTS = 80


def _moe_kernel(final, x_ref, g_ref, wr_ref, br_ref, wg_ref, wu_ref, wd_ref,
                p_ref, gp_ref, wpg32_ref, wpi32_ref, gf_ref, out_ref,
                xn_scr, gate_scr, ghi_scr, glo_scr, sel_scr, selt_scr, acc_scr, ovf_smem,
                wpg_ref, wpi_ref):
    grp = pl.program_id(1)
    _cast_weights_once(jnp.logical_and(pl.program_id(0) == 0, grp == 0),
                       [(wpg32_ref, wpg_ref), (wpi32_ref, wpi_ref)])
    tm, d = x_ref.shape
    n_sub = tm // MOE_SUB

    @pl.when(grp == 0)
    def _route():
        xn = _rms(x_ref[...], g_ref[...])
        a_hi = xn.astype(BF16)
        xn_scr[...] = a_hi
        a_lo = (xn - a_hi.astype(F32)).astype(BF16)
        r_hi = _dot_nt(wr_ref[...], a_hi)
        r_lo = _dot_nt(wr_ref[:LANES, :], a_lo)
        nr = ROUTER_ROWS
        logits = r_hi[:nr] + r_hi[LANES:LANES + nr] + r_lo[:nr] + br_ref[:nr, :]
        row_f = lax.broadcasted_iota(jnp.int32, (nr, tm), 0).astype(F32)
        is_grp = row_f < MOE_GROUPS
        lg = jnp.where(is_grp, logits, NEG_BIG)
        mg = jnp.max(lg, axis=0, keepdims=True)
        p1 = 1.0 / jnp.sum(jnp.exp(lg - mg), axis=0, keepdims=True)
        gi = jnp.min(jnp.where(lg == mg, row_f, float(LANES)), axis=0, keepdims=True)
        row_grp = jnp.floor((row_f - ROUTER_EXP_LANE0) * (1.0 / EXP_PER_GROUP))
        in_grp = jnp.logical_and(
            jnp.logical_and(row_f >= ROUTER_EXP_LANE0, row_f < ROUTER_EXP_LANE0 + N_EXPERTS),
            row_grp == gi)
        le = jnp.where(in_grp, logits, NEG_BIG)
        v1 = jnp.max(le, axis=0, keepdims=True)
        i1 = jnp.min(jnp.where(le == v1, row_f, float(LANES)), axis=0, keepdims=True)
        le2 = jnp.where(row_f == i1, NEG_BIG, le)
        v2 = jnp.max(le2, axis=0, keepdims=True)
        i2 = jnp.min(jnp.where(le2 == v2, row_f, float(LANES)), axis=0, keepdims=True)
        t = jnp.exp(v2 - v1)
        w1 = p1 / (1.0 + t)
        w2 = p1 * t / (1.0 + t)
        gate_t = jnp.where(row_f == i1, w1, 0.0) + jnp.where(row_f == i2, w2, 0.0)
        gate = jnp.transpose(
            jnp.concatenate([gate_t, jnp.zeros((LANES - nr, tm), F32)], axis=0))
        gate_scr[...] = gate
        g_hi = gate.astype(BF16)
        ghi_scr[...] = g_hi
        glo_scr[...] = (gate - g_hi.astype(F32)).astype(BF16)
        acc_scr[...] = jnp.zeros_like(acc_scr)

        oh_rows = 16
        row_o = lax.broadcasted_iota(jnp.int32, (oh_rows, tm), 0).astype(F32)
        onehot = jnp.where(row_o == gi, 1.0, 0.0).astype(BF16)
        r = lax.broadcasted_iota(jnp.int32, (MOE_SUB, MOE_SUB), 0)
        c = lax.broadcasted_iota(jnp.int32, (MOE_SUB, MOE_SUB), 1)
        earlier = jnp.where(r < c, 1.0, 0.0).astype(BF16)
        fullest = jnp.zeros((oh_rows, 1), F32)
        for s in range(n_sub):
            rs = slice(s * MOE_SUB, (s + 1) * MOE_SUB)
            oh = onehot[:, rs]
            ahead = _dot(oh, earlier)
            slot = jnp.sum(jnp.where(oh > 0, ahead, 0.0), axis=0, keepdims=True)
            sel = jnp.where(slot < LANES, slot + float(LANES) * gi[:, rs], -1.0)
            selt_scr[:, rs] = jnp.broadcast_to(sel, (8, MOE_SUB))
            sel_scr[rs, :] = jnp.transpose(jnp.broadcast_to(sel, (LANES, MOE_SUB)))
            fullest = jnp.maximum(fullest, jnp.sum(oh.astype(F32), axis=1, keepdims=True))
        row1 = lax.broadcasted_iota(jnp.int32, (oh_rows, 1), 0)
        for gidx in range(MOE_GROUPS):
            most = jnp.max(jnp.where(row1 == gidx, fullest, 0.0))
            ovf_smem[gidx] = (most > MOE_SLOTS).astype(jnp.int32)

    def experts(rows, gates):
        lane_r = lax.broadcasted_iota(jnp.int32, gates.shape, 1)
        y = None
        for j in range(EXP_PER_GROUP):
            hg = _dot(rows, wg_ref[j])
            hu = _dot(rows, wu_ref[j])
            here = lane_r == ROUTER_EXP_LANE0 + EXP_PER_GROUP * grp + j
            gcol = jnp.sum(jnp.where(here, gates, 0.0), axis=-1, keepdims=True)
            hid = ((hg * jax.nn.sigmoid(hg)) * hu * gcol).astype(BF16)
            yj = _dot(hid, wd_ref[j])
            y = yj if y is None else y + yj
        return y

    key = (grp * LANES).astype(F32)

    @pl.when(ovf_smem[grp] == 0)
    def _compact():
        slot_rows = lax.broadcasted_iota(jnp.int32, (MOE_SLOTS, MOE_SUB), 0).astype(F32) + key
        slot_cols = lax.broadcasted_iota(jnp.int32, (MOE_SUB, LANES), 1).astype(F32) + key
        xs, gs = [], []
        for s in range(n_sub):
            rs = slice(s * MOE_SUB, (s + 1) * MOE_SUB)
            pick = jnp.where(selt_scr[0:1, rs] == slot_rows, 1.0, 0.0).astype(BF16)
            xs.append(_dot(pick, xn_scr[rs, :]).astype(BF16))
            gs.append(_dot(pick, ghi_scr[rs, :]) + _dot(pick, glo_scr[rs, :]))
        y = experts(jnp.concatenate(xs, axis=0), jnp.concatenate(gs, axis=0))
        y = jnp.concatenate([y, jnp.zeros((LANES - MOE_SLOTS, d), F32)], axis=0)
        y_hi = y.astype(BF16)
        y_lo = (y - y_hi.astype(F32)).astype(BF16)
        for s in range(n_sub):
            rs = slice(s * MOE_SUB, (s + 1) * MOE_SUB)
            ys = slice(s * MOE_SLOTS, s * MOE_SLOTS + LANES)
            place = jnp.where(sel_scr[rs, :] == slot_cols, 1.0, 0.0).astype(BF16)
            acc_scr[rs, :] += _dot(jnp.concatenate([place, place], axis=1),
                                   jnp.concatenate([y_hi[ys, :], y_lo[ys, :]], axis=0))

    @pl.when(ovf_smem[grp] != 0)
    def _dense():
        for s in range(n_sub):
            rs = slice(s * MOE_SUB, (s + 1) * MOE_SUB)
            acc_scr[rs, :] += experts(xn_scr[rs, :], gate_scr[rs, :])

    @pl.when(grp == MOE_GROUPS - 1)
    def _finish():
        for s in range(n_sub):
            rs = slice(s * MOE_SUB, (s + 1) * MOE_SUB)
            x = x_ref[rs, :] + acc_scr[rs, :]
            gate = jax.nn.sigmoid(_dot(_rms(x, gp_ref[...]).astype(BF16), wpg_ref[...]))
            y = x + _dot(p_ref[rs, :].astype(BF16), wpi_ref[...]) * gate
            if final:
                y = _rms(y, gf_ref[...])
            out_ref[rs, :] = y


def _moe_ple(x2d, g, w_router, b_router, w_exp_gate, w_exp_up, w_exp_down,
             p2d, layer, g_ple, w_ple_gate, w_ple_in, g_final, final):
    t, d = x2d.shape
    f = w_exp_gate.shape[-1]
    tm = MOE_TILE
    row = pl.BlockSpec((tm, d), lambda i, e: (i, 0))
    group_w = lambda a, b: pl.BlockSpec((EXP_PER_GROUP, a, b), lambda i, e: (e, 0, 0))
    return pl.pallas_call(
        functools.partial(_moe_kernel, final),
        grid=(t // tm, MOE_GROUPS),
        in_specs=[row, _const_spec((1, d)), _const_spec(w_router.shape),
                  _const_spec(b_router.shape),
                  group_w(d, f), group_w(d, f), group_w(f, d),
                  pl.BlockSpec((tm, p2d.shape[1]), lambda i, e: (layer * (t // tm) + i, 0)),
                  _const_spec((1, d)), _layer_weight_spec(w_ple_gate, layer),
                  _layer_weight_spec(w_ple_in, layer), _const_spec((1, d))],
        out_specs=row,
        out_shape=jax.ShapeDtypeStruct((t, d), F32),
        scratch_shapes=[pltpu.VMEM((tm, d), BF16), pltpu.VMEM((tm, LANES), F32),
                        pltpu.VMEM((tm, LANES), BF16), pltpu.VMEM((tm, LANES), BF16),
                        pltpu.VMEM((tm, LANES), F32), pltpu.VMEM((8, tm), F32),
                        pltpu.VMEM((tm, d), F32), pltpu.SMEM((MOE_GROUPS,), jnp.int32),
                        pltpu.VMEM(w_ple_gate.shape[1:], BF16),
                        pltpu.VMEM(w_ple_in.shape[1:], BF16)],
        compiler_params=_cparams(2),
        name="moe_ple",
    )(x2d, g, w_router, b_router, w_exp_gate, w_exp_up, w_exp_down,
      p2d, g_ple, w_ple_gate, w_ple_in, g_final)


def _router_weights(w_grp, b_grp, w_exp, b_exp):
    d = w_grp.shape[0]
    w = jnp.concatenate([w_grp, w_exp.transpose(1, 0, 2).reshape(d, N_EXPERTS)], axis=1)
    b = jnp.concatenate([b_grp, b_exp.reshape(N_EXPERTS)])
    pad = LANES - w.shape[1]
    w = jnp.pad(w, ((0, 0), (0, pad))).T
    w_hi = w.astype(BF16)
    w_lo = (w - w_hi.astype(F32)).astype(BF16)
    return jnp.concatenate([w_hi, w_lo], axis=0), jnp.pad(b, (0, pad))[:, None]


def kernel(x, p, positions, norm_mix, w_in, w_gate, b_gate, w_pool, pool_scale, w_up_a, w_up_b, w_up_c, w_out, norm_moe, w_router_grp, b_router_grp, w_router_exp, b_router_exp, w_exp_gate, w_exp_up, w_exp_down, norm_ple, w_ple_in, w_ple_gate, norm_final):
    batch, seq, d = x.shape
    depth = w_in.shape[0]
    t = batch * seq
    x2d = x.reshape(t, d)
    cos_t, sin_t = _rope_tables(positions)
    for i in range(depth):
        (qa, ka, va, qb, kb, vb, uc), (e_gate, e_up, e_down) = _inproj(
            x2d, norm_mix[i][None, :], w_in, i, cos_t, sin_t,
            (w_exp_gate, w_exp_up, w_exp_down))
        dil_mix = _dilated(qa, ka, va, batch, seq)
        sb_out = _stick_breaking(qb, kb, vb, batch, seq)
        x2d = _merge(x2d, dil_mix, sb_out, uc, seq, i, norm_mix[i][None, :], w_gate,
                     b_gate[i][None, :], w_pool, pool_scale[i][None, :],
                     w_up_a, w_up_b, w_up_c, w_out)
        w_r, b_r = _router_weights(w_router_grp[i], b_router_grp[i],
                                   w_router_exp[i], b_router_exp[i])
        x2d = _moe_ple(x2d, norm_moe[i][None, :], w_r, b_r, e_gate, e_up, e_down,
                       p.reshape(depth * t, -1), i,
                       norm_ple[i][None, :], w_ple_gate, w_ple_in,
                       norm_final[None, :], i == depth - 1)
    return x2d.reshape(batch, seq, d)
```

```python
import functools

import jax
import jax.numpy as jnp
from jax import lax
from jax.experimental import pallas as pl
from jax.experimental.pallas import tpu as pltpu

F32 = jnp.float32
BF16 = jnp.bfloat16

HEAD_DIM = 64
ROPE_THETA = 10000.0
DIL_CONFIGS = ((128, 1), (512, 4), (2048, 16))
DIL_HEADS = 4
DIL_OUT = DIL_HEADS * HEAD_DIM
DIL_WIDTH = len(DIL_CONFIGS) * DIL_OUT
DIL_BLOCK = 128
DIL_CHUNK = 2048
DIL_SKEW = 3
INPROJ_TILE = 512
MERGE_TILE = 512
DIL_PERMUTED = (16,)
SB_HEADS = 8
SB_WIDTH = SB_HEADS * HEAD_DIM
SB_BLOCK = 128
POOL_WINDOWS = (2, 4, 8, 16)
POOL_GROUP = 128
POOL_WIDTH = len(POOL_WINDOWS) * POOL_GROUP
POOL_HALO = 16
N_BRANCH = 3
MOE_GROUPS = 4
EXP_PER_GROUP = 4
N_EXPERTS = MOE_GROUPS * EXP_PER_GROUP
EPS = 1e-6

LANES = 128
NEG_BIG = -1e30
SB_DEAD_LOG = -105.0
SB_WINDOW = 3
SB_GROUP = 28
SB_SKEW = 8
SB_FAR_ROWS = 64
VMEM_LIMIT = 56 * 1024 * 1024


def _cparams(n_axes):
    return pltpu.CompilerParams(
        dimension_semantics=("arbitrary",) * n_axes, vmem_limit_bytes=VMEM_LIMIT)


def _rms(x, g):
    ms = jnp.mean(x * x, axis=-1, keepdims=True)
    return x * lax.rsqrt(ms + EPS) * g


def _dot(a, b):
    return jnp.dot(a, b, preferred_element_type=F32)


def _dot_nt(a, b):
    return lax.dot_general(a, b, (((1,), (1,)), ((), ())), preferred_element_type=F32)


def _const_spec(shape):
    zeros = (0,) * len(shape)
    return pl.BlockSpec(shape, lambda *_: zeros)


def _layer_weight_spec(w, layer):
    tail = (0,) * (w.ndim - 1)
    return pl.BlockSpec((None,) + w.shape[1:], lambda *_: (layer,) + tail,
                        pipeline_mode=pl.Buffered(1))


def _cast_weights_once(first_step, pairs):
    @pl.when(first_step)
    def _cast():
        for src, dst in pairs:
            if len(src.shape) == 3:
                for k in range(src.shape[0]):
                    dst[k] = src[k].astype(BF16)
            else:
                step = 2 * LANES
                for c in range(0, src.shape[1], step):
                    dst[:, c:c + step] = src[:, c:c + step].astype(BF16)


def _rope_table_kernel(pos_ref, inv_ref, sgn_ref, cos_ref, sin_ref):
    ang = inv_ref[...] * pos_ref[...].astype(F32)
    reps = LANES // ang.shape[0]
    cos_ref[...] = jnp.transpose(jnp.concatenate([jnp.cos(ang)] * reps, axis=0))
    sin_ref[...] = jnp.transpose(jnp.concatenate([jnp.sin(ang)] * reps, axis=0) * sgn_ref[...])


def _rope_tables(positions):
    t = positions.size
    half = HEAD_DIM // 2
    inv_col = (ROPE_THETA ** (-jnp.arange(half, dtype=F32) / half))[:, None]
    sgn_col = jnp.tile(jnp.concatenate([-jnp.ones(half, F32), jnp.ones(half, F32)]),
                       LANES // HEAD_DIM)[:, None]
    tm = 2048
    return pl.pallas_call(
        _rope_table_kernel,
        grid=(t // tm,),
        in_specs=[pl.BlockSpec((None, 1, tm), lambda i: (i, 0, 0)),
                  _const_spec((half, 1)), _const_spec((LANES, 1))],
        out_specs=[pl.BlockSpec((tm, LANES), lambda i: (i, 0))] * 2,
        out_shape=[jax.ShapeDtypeStruct((t, LANES), F32)] * 2,
        compiler_params=_cparams(1),
        name="rope_table",
    )(positions.reshape(t // tm, 1, tm), inv_col, sgn_col)


def _inproj_kernel(seq, x_ref, g_ref, w32_ref, cos_ref, sin_ref, wp32_ref, ps_ref,
                   eg32_ref, eu32_ref, ed32_ref,
                   qa_ref, ka_ref, va_ref, qb_ref, kb_ref, vb_ref, yc_ref,
                   eg_ref, eu_ref, ed_ref, h_scr, perm_scr, w_ref, wp_ref, ub_scr):
    _cast_weights_once(pl.program_id(0) == 0, [(w32_ref, w_ref), (wp32_ref, wp_ref)])

    @pl.when(pl.program_id(0) == 0)
    def _init_carry():
        ub_scr[0:POOL_HALO, :] = jnp.zeros((POOL_HALO, POOL_WIDTH), F32)

    eg_ref[...] = eg32_ref[...].astype(BF16)
    eu_ref[...] = eu32_ref[...].astype(BF16)
    ed_ref[...] = ed32_ref[...].astype(BF16)
    h_scr[...] = _rms(x_ref[...], g_ref[...]).astype(BF16)
    cos = cos_ref[...]
    sin = sin_ref[...]
    tm = cos.shape[0]
    lane = lax.broadcasted_iota(jnp.int32, (tm, LANES), 1)
    first_half = (lane & (HEAD_DIM // 2)) == 0
    scale = HEAD_DIM ** -0.5

    def rope(t):
        partner = jnp.where(first_half, pltpu.roll(t, LANES - HEAD_DIM // 2, 1),
                            pltpu.roll(t, HEAD_DIM // 2, 1))
        return t * cos + partner * sin

    def store_dilated(ref, which, c, val):
        sl = slice(c * LANES, (c + 1) * LANES)
        dil = DIL_CONFIGS[c // (DIL_OUT // LANES)][1]
        if dil not in DIL_PERMUTED:
            ref[:, sl] = val
            return
        per = tm // dil
        perm_scr[which, c % (DIL_OUT // LANES)] = val
        for r in range(dil):
            ref[r * per:(r + 1) * per, sl] = perm_scr[which, c % (DIL_OUT // LANES),
                                                      pl.ds(r, per, stride=dil), :]

    h = h_scr[...]

    uc = _dot(h, w_ref[:, w_ref.shape[1] - POOL_WIDTH:])
    row0 = (pl.program_id(0) * tm) % seq
    ub_scr[0:POOL_HALO, :] = jnp.where(row0 == 0, 0.0, ub_scr[0:POOL_HALO, :])
    ub_scr[POOL_HALO:POOL_HALO + tm, :] = uc
    t_in_seq = row0 + lax.broadcasted_iota(jnp.int32, (tm, 1), 0)

    def pool_group(g):
        w = POOL_WINDOWS[g]
        sl = slice(g * POOL_GROUP, (g + 1) * POOL_GROUP)
        u = ub_scr[POOL_HALO:POOL_HALO + tm, sl]
        acc = u
        for j in range(1, w):
            acc = acc + ub_scr[POOL_HALO - j:POOL_HALO - j + tm, sl]
        cnt = jnp.minimum(t_in_seq + 1, w).astype(F32)
        pooled = acc / cnt - u
        yc_ref[:, sl] = (_dot(pooled.astype(BF16), wp_ref[g]) * ps_ref[:, sl]).astype(BF16)

    col = 0
    pq = _dot(h, w_ref[:, col:col + DIL_WIDTH])
    for c in range(DIL_WIDTH // LANES):
        store_dilated(qa_ref, 0, c, rope(pq[:, c * LANES:(c + 1) * LANES]) * scale)
    pool_group(0)
    col += DIL_WIDTH
    pk = _dot(h, w_ref[:, col:col + DIL_WIDTH])
    for c in range(DIL_WIDTH // LANES):
        store_dilated(ka_ref, 1, c, rope(pk[:, c * LANES:(c + 1) * LANES]))
    pool_group(1)
    col += DIL_WIDTH
    pv = _dot(h, w_ref[:, col:col + DIL_WIDTH])
    for c in range(DIL_WIDTH // LANES):
        store_dilated(va_ref, 2, c, pv[:, c * LANES:(c + 1) * LANES])
    pool_group(2)
    col += DIL_WIDTH
    qb_ref[...] = (_dot(h, w_ref[:, col:col + SB_WIDTH]) * scale).astype(BF16)
    pool_group(3)
    col += SB_WIDTH
    kb_ref[...] = _dot(h, w_ref[:, col:col + SB_WIDTH]).astype(BF16)
    col += SB_WIDTH
    vb_ref[...] = _dot(h, w_ref[:, col:col + SB_WIDTH]).astype(BF16)
    ub_scr[0:POOL_HALO, :] = uc[tm - POOL_HALO:, :]


def _inproj(x2d, g, w_in, layer, cos_t, sin_t, w_pool, pool_scale, expert_ws, seq):
    t, d = x2d.shape
    tm = INPROJ_TILE
    assert seq % tm == 0 and tm >= POOL_HALO
    steps = t // tm
    row = lambda width: pl.BlockSpec((tm, width), lambda i: (i, 0))
    widths = (DIL_WIDTH,) * 3 + (SB_WIDTH,) * 3 + (POOL_WIDTH,)
    dtypes = (F32,) * 3 + (BF16,) * 4
    flat = [w.reshape(w.shape[0], -1, w.shape[-1]) for w in expert_ws]
    slabs = [(w.shape[1] // steps, w.shape[2]) for w in flat]
    outs = pl.pallas_call(
        functools.partial(_inproj_kernel, seq),
        grid=(steps,),
        in_specs=[row(d), _const_spec((1, d)), _layer_weight_spec(w_in, layer),
                  row(LANES), row(LANES), _layer_weight_spec(w_pool, layer),
                  _const_spec(pool_scale.shape)]
                 + [pl.BlockSpec((None,) + s, lambda i: (layer, i, 0)) for s in slabs],
        out_specs=[row(w) for w in widths]
                  + [pl.BlockSpec(s, lambda i: (i, 0)) for s in slabs],
        out_shape=[jax.ShapeDtypeStruct((t, w), dt) for w, dt in zip(widths, dtypes)]
                  + [jax.ShapeDtypeStruct(w.shape[1:], BF16) for w in flat],
        scratch_shapes=[pltpu.VMEM((tm, d), BF16),
                        pltpu.VMEM((3, DIL_OUT // LANES, tm, LANES), F32),
                        pltpu.VMEM(w_in.shape[1:], BF16),
                        pltpu.VMEM(w_pool.shape[1:], BF16),
                        pltpu.VMEM((POOL_HALO + tm, POOL_WIDTH), F32)],
        compiler_params=_cparams(1),
        name="in_proj",
    )(x2d, g, w_in, cos_t, sin_t, w_pool, pool_scale, *flat)
    experts_bf16 = [o.reshape(w.shape[1:]) for o, w in zip(outs[len(widths):], expert_ws)]
    return outs[:len(widths)], experts_bf16


def _dil_kernel(*refs):
    n_grp = len(DIL_CONFIGS)
    out_ref, o_scr, l_scr = refs[5 * n_grp:]
    for g, (_, dil) in enumerate(DIL_CONFIGS):
        _dil_group(dil, *refs[5 * g:5 * g + 5], o_scr.at[g], l_scr.at[g])
    lses = [l_scr[g] for g in range(n_grp)]
    m = functools.reduce(jnp.maximum, lses)
    es = [jnp.exp(l - m) for l in lses]
    num = sum(e * o_scr[g] for g, e in enumerate(es))
    out_ref[...] = (num / sum(es)).astype(BF16)


def _dil_group(dil, q_ref, kp_ref, kc_ref, vp_ref, vc_ref, o_ref, lse_ref):
    has_prev = pl.program_id(1) > 0
    n = DIL_BLOCK
    n_blocks = q_ref.shape[0] // (n * dil)
    qi = lax.broadcasted_iota(jnp.int32, (n, n), 0)
    kj = lax.broadcasted_iota(jnp.int32, (n, n), 1)
    mask_prev = kj >= qi
    mask_prev_first = jnp.logical_and(mask_prev, has_prev)
    mask_cur = kj <= qi
    lo_head = kj < HEAD_DIM
    ones = jnp.ones((2 * n, LANES), BF16)

    def rows(ref, r, j):
        if dil in DIL_PERMUTED:
            per = INPROJ_TILE // dil
            pieces = [ref[ti * INPROJ_TILE + r * per:ti * INPROJ_TILE + (r + 1) * per, :]
                      for ti in range(j * (n // per), (j + 1) * (n // per))]
            return jnp.concatenate(pieces, axis=0).astype(BF16)
        start = j * n * dil + r
        return ref[pl.ds(start, n, stride=dil) if dil > 1 else pl.ds(start, n), :].astype(BF16)

    units = {}

    def unit(r, j):
        if (r, j) not in units:
            kc, vc = rows(kc_ref, r, j), rows(vc_ref, r, j)
            if j == 0:
                kp, vp = rows(kp_ref, r, 0), rows(vp_ref, r, 0)
            else:
                kp, vp = units[(r, j - 1)]["kc"], units[(r, j - 1)]["vc"]
            vo = jnp.concatenate([jnp.concatenate([vp, vc], axis=0), ones], axis=1)
            units[(r, j)] = dict(q=rows(q_ref, r, j), kc=kc, vc=vc, kp=kp, vo=vo,
                                 pm=mask_prev_first if j == 0 else mask_prev,
                                 start=j * n * dil + r)
        return units[(r, j)]

    def stage_scores(c):
        u = unit(c["r"], c["j"])
        sel = lo_head if c["hh"] == 0 else jnp.logical_not(lo_head)
        qm = jnp.where(sel, u["q"], jnp.zeros((n, LANES), BF16))
        c["sp"] = _dot_nt(qm, u["kp"])
        c["sc"] = _dot_nt(qm, u["kc"])

    def stage_max(c):
        u = unit(c["r"], c["j"])
        c["sp"] = jnp.where(u["pm"], c["sp"], NEG_BIG)
        c["sc"] = jnp.where(mask_cur, c["sc"], NEG_BIG)
        c["m"] = jnp.max(jnp.maximum(c["sp"], c["sc"]), axis=-1, keepdims=True)

    def stage_pv(c):
        u = unit(c["r"], c["j"])
        ep = jnp.exp(c.pop("sp") - c["m"]).astype(BF16)
        ec = jnp.exp(c.pop("sc") - c["m"]).astype(BF16)
        res = _dot(jnp.concatenate([ep, ec], axis=1), u["vo"])
        c["acc"] = res[:, :LANES]
        c["den"] = res[:, LANES:]

    def stage_out(c, pairs):
        c["o"] = c.pop("acc") / c["den"]
        c["lse"] = c["m"] + jnp.log(c.pop("den"))
        key = (c["r"], c["j"])
        pairs.setdefault(key, []).append(c)
        if len(pairs[key]) == 2:
            c0, c1 = pairs.pop(key)
            u = unit(c["r"], c["j"])
            idx = pl.ds(u["start"], n, stride=dil) if dil > 1 else pl.ds(u["start"], n)
            o_ref[idx, :] = jnp.where(lo_head, c0["o"], c1["o"])
            lse_ref[idx, :] = jnp.where(lo_head, c0["lse"], c1["lse"])

    chains = [dict(r=r, j=j, hh=hh) for r in range(dil) for j in range(n_blocks)
              for hh in range(2)]
    stages = (stage_scores, stage_max, stage_pv)
    pairs = {}
    total = len(chains)
    for step in range(total + len(stages) * DIL_SKEW):
        for s, fn in enumerate(stages):
            i = step - s * DIL_SKEW
            if 0 <= i < total:
                fn(chains[i])
        i = step - len(stages) * DIL_SKEW
        if 0 <= i < total:
            stage_out(chains[i], pairs)


def _dilated(qa, ka, va, batch, seq):
    assert seq % DIL_CHUNK == 0
    nc = seq // DIL_CHUNK
    t = batch * seq
    n_pairs = DIL_OUT // LANES
    in_specs = []
    for g, (window, dil) in enumerate(DIL_CONFIGS):
        assert window // dil == DIL_BLOCK
        hist = DIL_BLOCK * dil
        cur = pl.BlockSpec((DIL_CHUNK, LANES),
                           lambda b, c, pr, g=g: (b * nc + c, g * n_pairs + pr))
        prev = pl.BlockSpec(
            (hist, LANES),
            lambda b, c, pr, g=g, hist=hist: (
                jnp.maximum((b * nc + c) * (DIL_CHUNK // hist) - 1, 0), g * n_pairs + pr))
        in_specs += [cur, prev, cur, prev, cur]
    n_grp = len(DIL_CONFIGS)
    return pl.pallas_call(
        _dil_kernel,
        grid=(batch, nc, n_pairs),
        in_specs=in_specs,
        out_specs=pl.BlockSpec((DIL_CHUNK, LANES), lambda b, c, pr: (b * nc + c, pr)),
        out_shape=jax.ShapeDtypeStruct((t, DIL_OUT), BF16),
        scratch_shapes=[pltpu.VMEM((n_grp, DIL_CHUNK, LANES), F32)] * 2,
        compiler_params=_cparams(3),
        name="dilated",
    )(*([qa, ka, ka, va, va] * n_grp))


def _sb_kernel(q_ref, k_ref, v_ref, o_ref, acc_scr, r_scr):
    blk = SB_BLOCK
    nq = q_ref.shape[0] // blk
    row = lax.broadcasted_iota(jnp.int32, (blk, blk), 0)
    col = lax.broadcasted_iota(jnp.int32, (blk, blk), 1)
    tri = col < row
    lo_head = col < HEAD_DIM
    neg_upper = jnp.where(row > col, -1.0, 0.0).astype(BF16)
    neg_cum = jnp.concatenate([neg_upper, neg_upper], axis=0)
    tri2 = jnp.concatenate([tri, tri], axis=0)

    def head_q(q, hh):
        sel = lo_head if hh == 0 else jnp.logical_not(lo_head)
        return jnp.where(sel, q, jnp.zeros_like(q))

    def tile_scores(z, mask):
        sp = jnp.maximum(z, 0.0) + jnp.log(1.0 + jnp.exp(-jnp.abs(z)))
        spm = sp if mask is None else jnp.where(mask, sp, 0.0)
        hi = spm.astype(BF16)
        lo = (spm - hi.astype(F32)).astype(BF16)
        after = _dot(jnp.concatenate([hi, lo], axis=1), neg_cum)
        return z - sp + after, -jnp.sum(spm, axis=-1, keepdims=True)

    def tile_weights(logit, mask, r_acc):
        a = jnp.exp(logit + r_acc)
        if mask is not None:
            a = jnp.where(mask, a, 0.0)
        return a.astype(BF16)

    def q_group(blocks):
        chains = []
        for qi, n_tiles in blocks:
            qoff = qi * blk
            koff = (qi - (n_tiles - 1)) * blk
            if not isinstance(qi, int):
                qoff, koff = pl.multiple_of(qoff, blk), pl.multiple_of(koff, blk)
            q = q_ref[pl.ds(qoff, blk), :]
            kw = k_ref[pl.ds(koff, n_tiles * blk), :]
            vw = v_ref[pl.ds(koff, n_tiles * blk), :]
            q2 = jnp.concatenate([head_q(q, 0), head_q(q, 1)], axis=0)
            chains.append(dict(z=_dot_nt(q2, kw), vw=vw, n=n_tiles, qi=qi,
                               r=jnp.zeros((2 * blk, 1), F32),
                               acc=jnp.zeros((2 * blk, blk), F32)))
        max_tiles = max(c["n"] for c in chains)
        far = SB_FAR_ROWS

        def far_rows(a):
            return jnp.concatenate([a[:far], a[blk:blk + far]], axis=0)

        def add_far(a, upd):
            return jnp.concatenate([a[:far] + upd[:far], a[far:blk],
                                    a[blk:blk + far] + upd[far:], a[blk + far:]], axis=0)

        tiles = [(c, c["n"] - 1 - j) for j in range(max_tiles) for c in chains if j < c["n"]]
        pending = []
        for step in range(len(tiles) + SB_SKEW):
            if step < len(tiles):
                c, w = tiles[step]
                mask = tri2 if w == c["n"] - 1 else None
                is_far = w == 0 and c["n"] == SB_WINDOW
                zt = c["z"][:, w * blk:(w + 1) * blk]
                logit, rowsum = tile_scores(far_rows(zt) if is_far else zt, mask)
                pending.append((c, w, mask, logit, rowsum, is_far))
            if step >= SB_SKEW:
                c, w, mask, logit, rowsum, is_far = pending[step - SB_SKEW]
                a = tile_weights(logit, mask, far_rows(c["r"]) if is_far else c["r"])
                pv = _dot(a, c["vw"][w * blk:(w + 1) * blk, :])
                if is_far:
                    c["acc"] = add_far(c["acc"], pv)
                    c["r"] = add_far(c["r"], rowsum)
                else:
                    c["acc"] = c["acc"] + pv
                    c["r"] = c["r"] + rowsum
        for c in chains:
            qi = c["qi"]
            qoff = qi * blk if isinstance(qi, int) else pl.multiple_of(qi * blk, blk)
            r_scr[0, pl.ds(qoff, blk), :] = jnp.broadcast_to(c["r"][:blk], (blk, blk))
            r_scr[1, pl.ds(qoff, blk), :] = jnp.broadcast_to(c["r"][blk:], (blk, blk))
            acc_pair = jnp.where(lo_head, c["acc"][:blk], c["acc"][blk:])
            acc_scr[pl.ds(qoff, blk), :] = acc_pair
            o_ref[pl.ds(qoff, blk), :] = acc_pair.astype(BF16)

    first_tail = min(SB_WINDOW - 1, nq)
    n_static = SB_WINDOW + (nq - SB_WINDOW) % SB_GROUP
    q_group([(qi, min(qi + 1, SB_WINDOW)) for qi in range(n_static)])

    def q_loop(it, carry):
        q0 = n_static + it * SB_GROUP
        q_group([(q0 + j, SB_WINDOW) for j in range(SB_GROUP)])
        return carry

    lax.fori_loop(0, (nq - n_static) // SB_GROUP, q_loop, 0)


    @pl.when(jnp.max(r_scr[:, first_tail * blk:, :]) > SB_DEAD_LOG)
    def _tail():
        def q_tail(qi, carry):
            qoff = pl.multiple_of(qi * blk, blk)
            q = q_ref[pl.ds(qoff, blk), :]
            acc0 = acc_scr[pl.ds(qoff, blk), :]
            res = []
            for hh in range(2):
                qm = head_q(q, hh)
                r0 = r_scr[hh, pl.ds(qoff, blk), :]

                def cond(c):
                    return jnp.logical_and(c[0] >= 0, c[3] > SB_DEAD_LOG)

                def body(c, qm=qm):
                    off = pl.multiple_of(c[0] * blk, blk)
                    z = _dot_nt(qm, k_ref[pl.ds(off, blk), :])
                    todo = jnp.logical_or(row >= SB_FAR_ROWS, c[0] != qi - (SB_WINDOW - 1))
                    logit, rowsum = tile_scores(z, todo)
                    a = tile_weights(logit, todo, c[1])
                    r_new = c[1] + rowsum
                    acc_new = c[2] + _dot(a, v_ref[pl.ds(off, blk), :])
                    return c[0] - 1, r_new, acc_new, jnp.max(r_new)

                c0 = (qi - (SB_WINDOW - 1), r0, acc0, jnp.max(r0))
                res.append(lax.while_loop(cond, body, c0)[2])
            o_ref[pl.ds(qoff, blk), :] = jnp.where(lo_head, res[0], res[1]).astype(BF16)
            return carry

        lax.fori_loop(first_tail, nq, q_tail, 0)


def _stick_breaking(qb, kb, vb, batch, seq):
    view = lambda a: a.reshape(batch, seq, SB_WIDTH)
    spec = pl.BlockSpec((None, seq, LANES), lambda b, hp: (b, 0, hp))
    o = pl.pallas_call(
        _sb_kernel,
        grid=(batch, SB_WIDTH // LANES),
        in_specs=[spec, spec, spec],
        out_specs=spec,
        out_shape=jax.ShapeDtypeStruct((batch, seq, SB_WIDTH), BF16),
        scratch_shapes=[pltpu.VMEM((seq, LANES), F32), pltpu.VMEM((2, seq, LANES), F32)],
        compiler_params=_cparams(2),
        name="stick_breaking",
    )(view(qb), view(kb), view(vb))
    return o.reshape(batch * seq, SB_WIDTH)


def _merge_kernel(x_ref, dil_ref, sb_ref, yc_ref, g_ref, wg32_ref, bg_ref,
                  wa32_ref, wb32_ref, wc32_ref, wo32_ref, out_ref,
                  h_scr, m_scr, wg_ref, wa_ref, wb_ref, wc_ref, wo_ref):
    _cast_weights_once(pl.program_id(0) == 0,
                       [(wg32_ref, wg_ref), (wa32_ref, wa_ref), (wb32_ref, wb_ref),
                        (wc32_ref, wc_ref), (wo32_ref, wo_ref)])
    tm, d = x_ref.shape
    x = x_ref[...]
    h_scr[...] = _rms(x, g_ref[...]).astype(BF16)
    cw = 256

    def gate_chunk(c):
        gates = []
        for br in range(N_BRANCH):
            gsl = slice(br * d + c * cw, br * d + (c + 1) * cw)
            gates.append(jax.nn.sigmoid(_dot(h_scr[...], wg_ref[:, gsl]) + bg_ref[:, gsl]))
        return gates

    dil = dil_ref[...]
    sb = sb_ref[...]
    yc_in = yc_ref[...]
    for c in range(d // cw):
        sl = slice(c * cw, (c + 1) * cw)
        gates = gate_chunk(c)
        merged = (gates[0] * _dot(dil, wa_ref[:, sl]) + gates[1] * _dot(sb, wb_ref[:, sl])
                  + gates[2] * _dot(yc_in, wc_ref[:, sl]))
        m_scr[:, sl] = merged.astype(BF16)
    out_ref[...] = x + _dot(m_scr[...], wo_ref[...])


def _merge(x2d, dil_mix, sb_out, yc, layer, g, w_gate, b_gate, w_up_a, w_up_b, w_up_c, w_out):
    t, d = x2d.shape
    tm = MERGE_TILE
    row = lambda width: pl.BlockSpec((tm, width), lambda i: (i, 0))
    big = (w_gate, w_up_a, w_up_b, w_up_c, w_out)
    lw = lambda w: _layer_weight_spec(w, layer)
    return pl.pallas_call(
        _merge_kernel,
        grid=(t // tm,),
        in_specs=[row(d), row(DIL_OUT), row(SB_WIDTH), row(POOL_WIDTH),
                  _const_spec(g.shape), lw(w_gate), _const_spec(b_gate.shape),
                  lw(w_up_a), lw(w_up_b), lw(w_up_c), lw(w_out)],
        out_specs=row(d),
        out_shape=jax.ShapeDtypeStruct((t, d), F32),
        scratch_shapes=[pltpu.VMEM((tm, d), BF16), pltpu.VMEM((tm, d), BF16)]
                       + [pltpu.VMEM(w.shape[1:], BF16) for w in big],
        compiler_params=_cparams(1),
        name="merge",
    )(x2d, dil_mix, sb_out, yc, g, w_gate, b_gate, w_up_a, w_up_b, w_up_c, w_out)


ROUTER_EXP_LANE0 = MOE_GROUPS
ROUTER_ROWS = 24
MOE_TILE = 1024
MOE_SUB = 256
MOE_SLOTS = 80


def _moe_kernel(final, x_ref, g_ref, wr_ref, br_ref, wg_ref, wu_ref, wd_ref,
                p_ref, gp_ref, wpg32_ref, wpi32_ref, gf_ref, out_ref,
                xn_scr, gate_scr, ghi_scr, glo_scr, sel_scr, selt_scr, acc_scr, ovf_smem,
                wpg_ref, wpi_ref):
    grp = pl.program_id(1)
    _cast_weights_once(jnp.logical_and(pl.program_id(0) == 0, grp == 0),
                       [(wpg32_ref, wpg_ref), (wpi32_ref, wpi_ref)])
    tm, d = x_ref.shape
    n_sub = tm // MOE_SUB

    @pl.when(grp == 0)
    def _route():
        xn = _rms(x_ref[...], g_ref[...])
        a_hi = xn.astype(BF16)
        xn_scr[...] = a_hi
        a_lo = (xn - a_hi.astype(F32)).astype(BF16)
        r_hi = _dot_nt(wr_ref[...], a_hi)
        r_lo = _dot_nt(wr_ref[:LANES, :], a_lo)
        nr = ROUTER_ROWS
        logits = r_hi[:nr] + r_hi[LANES:LANES + nr] + r_lo[:nr] + br_ref[:nr, :]
        row_f = lax.broadcasted_iota(jnp.int32, (nr, tm), 0).astype(F32)
        is_grp = row_f < MOE_GROUPS
        lg = jnp.where(is_grp, logits, NEG_BIG)
        mg = jnp.max(lg, axis=0, keepdims=True)
        p1 = 1.0 / jnp.sum(jnp.exp(lg - mg), axis=0, keepdims=True)
        gi = jnp.min(jnp.where(lg == mg, row_f, float(LANES)), axis=0, keepdims=True)
        row_grp = jnp.floor((row_f - ROUTER_EXP_LANE0) * (1.0 / EXP_PER_GROUP))
        in_grp = jnp.logical_and(
            jnp.logical_and(row_f >= ROUTER_EXP_LANE0, row_f < ROUTER_EXP_LANE0 + N_EXPERTS),
            row_grp == gi)
        le = jnp.where(in_grp, logits, NEG_BIG)
        v1 = jnp.max(le, axis=0, keepdims=True)
        i1 = jnp.min(jnp.where(le == v1, row_f, float(LANES)), axis=0, keepdims=True)
        le2 = jnp.where(row_f == i1, NEG_BIG, le)
        v2 = jnp.max(le2, axis=0, keepdims=True)
        i2 = jnp.min(jnp.where(le2 == v2, row_f, float(LANES)), axis=0, keepdims=True)
        t = jnp.exp(v2 - v1)
        w1 = p1 / (1.0 + t)
        w2 = p1 * t / (1.0 + t)
        gate_t = jnp.where(row_f == i1, w1, 0.0) + jnp.where(row_f == i2, w2, 0.0)
        gate = jnp.transpose(
            jnp.concatenate([gate_t, jnp.zeros((LANES - nr, tm), F32)], axis=0))
        gate_scr[...] = gate
        g_hi = gate.astype(BF16)
        ghi_scr[...] = g_hi
        glo_scr[...] = (gate - g_hi.astype(F32)).astype(BF16)
        acc_scr[...] = jnp.zeros_like(acc_scr)

        oh_rows = 16
        row_o = lax.broadcasted_iota(jnp.int32, (oh_rows, tm), 0).astype(F32)
        onehot = jnp.where(row_o == gi, 1.0, 0.0).astype(BF16)
        r = lax.broadcasted_iota(jnp.int32, (MOE_SUB, MOE_SUB), 0)
        c = lax.broadcasted_iota(jnp.int32, (MOE_SUB, MOE_SUB), 1)
        earlier = jnp.where(r < c, 1.0, 0.0).astype(BF16)
        fullest = jnp.zeros((oh_rows, 1), F32)
        for s in range(n_sub):
            rs = slice(s * MOE_SUB, (s + 1) * MOE_SUB)
            oh = onehot[:, rs]
            ahead = _dot(oh, earlier)
            slot = jnp.sum(jnp.where(oh > 0, ahead, 0.0), axis=0, keepdims=True)
            sel = jnp.where(slot < LANES, slot + float(LANES) * gi[:, rs], -1.0)
            selt_scr[:, rs] = jnp.broadcast_to(sel, (8, MOE_SUB))
            sel_scr[rs, :] = jnp.transpose(jnp.broadcast_to(sel, (LANES, MOE_SUB)))
            fullest = jnp.maximum(fullest, jnp.sum(oh.astype(F32), axis=1, keepdims=True))
        row1 = lax.broadcasted_iota(jnp.int32, (oh_rows, 1), 0)
        for gidx in range(MOE_GROUPS):
            most = jnp.max(jnp.where(row1 == gidx, fullest, 0.0))
            ovf_smem[gidx] = (most > MOE_SLOTS).astype(jnp.int32)

    def experts(rows, gates):
        lane_r = lax.broadcasted_iota(jnp.int32, gates.shape, 1)
        y = None
        for j in range(EXP_PER_GROUP):
            hg = _dot(rows, wg_ref[j])
            hu = _dot(rows, wu_ref[j])
            here = lane_r == ROUTER_EXP_LANE0 + EXP_PER_GROUP * grp + j
            gcol = jnp.sum(jnp.where(here, gates, 0.0), axis=-1, keepdims=True)
            hid = ((hg * jax.nn.sigmoid(hg)) * hu * gcol).astype(BF16)
            yj = _dot(hid, wd_ref[j])
            y = yj if y is None else y + yj
        return y

    key = (grp * LANES).astype(F32)

    @pl.when(ovf_smem[grp] == 0)
    def _compact():
        slot_rows = lax.broadcasted_iota(jnp.int32, (MOE_SLOTS, MOE_SUB), 0).astype(F32) + key
        slot_cols = lax.broadcasted_iota(jnp.int32, (MOE_SUB, LANES), 1).astype(F32) + key
        xs, gs = [], []
        for s in range(n_sub):
            rs = slice(s * MOE_SUB, (s + 1) * MOE_SUB)
            pick = jnp.where(selt_scr[0:1, rs] == slot_rows, 1.0, 0.0).astype(BF16)
            xs.append(_dot(pick, xn_scr[rs, :]).astype(BF16))
            gs.append(_dot(pick, ghi_scr[rs, :]) + _dot(pick, glo_scr[rs, :]))
        y = experts(jnp.concatenate(xs, axis=0), jnp.concatenate(gs, axis=0))
        y = jnp.concatenate([y, jnp.zeros((LANES - MOE_SLOTS, d), F32)], axis=0)
        y_hi = y.astype(BF16)
        y_lo = (y - y_hi.astype(F32)).astype(BF16)
        for s in range(n_sub):
            rs = slice(s * MOE_SUB, (s + 1) * MOE_SUB)
            ys = slice(s * MOE_SLOTS, s * MOE_SLOTS + LANES)
            place = jnp.where(sel_scr[rs, :] == slot_cols, 1.0, 0.0).astype(BF16)
            acc_scr[rs, :] += _dot(jnp.concatenate([place, place], axis=1),
                                   jnp.concatenate([y_hi[ys, :], y_lo[ys, :]], axis=0))

    @pl.when(ovf_smem[grp] != 0)
    def _dense():
        for s in range(n_sub):
            rs = slice(s * MOE_SUB, (s + 1) * MOE_SUB)
            acc_scr[rs, :] += experts(xn_scr[rs, :], gate_scr[rs, :])

    @pl.when(grp == MOE_GROUPS - 1)
    def _finish():
        for s in range(n_sub):
            rs = slice(s * MOE_SUB, (s + 1) * MOE_SUB)
            x = x_ref[rs, :] + acc_scr[rs, :]
            gate = jax.nn.sigmoid(_dot(_rms(x, gp_ref[...]).astype(BF16), wpg_ref[...]))
            y = x + _dot(p_ref[rs, :].astype(BF16), wpi_ref[...]) * gate
            if final:
                y = _rms(y, gf_ref[...])
            out_ref[rs, :] = y


def _moe_ple(x2d, g, w_router, b_router, w_exp_gate, w_exp_up, w_exp_down,
             p2d, layer, g_ple, w_ple_gate, w_ple_in, g_final, final):
    t, d = x2d.shape
    f = w_exp_gate.shape[-1]
    tm = MOE_TILE
    row = pl.BlockSpec((tm, d), lambda i, e: (i, 0))
    group_w = lambda a, b: pl.BlockSpec((EXP_PER_GROUP, a, b), lambda i, e: (e, 0, 0))
    return pl.pallas_call(
        functools.partial(_moe_kernel, final),
        grid=(t // tm, MOE_GROUPS),
        in_specs=[row, _const_spec((1, d)), _const_spec(w_router.shape),
                  _const_spec(b_router.shape),
                  group_w(d, f), group_w(d, f), group_w(f, d),
                  pl.BlockSpec((tm, p2d.shape[1]), lambda i, e: (layer * (t // tm) + i, 0)),
                  _const_spec((1, d)), _layer_weight_spec(w_ple_gate, layer),
                  _layer_weight_spec(w_ple_in, layer), _const_spec((1, d))],
        out_specs=row,
        out_shape=jax.ShapeDtypeStruct((t, d), F32),
        scratch_shapes=[pltpu.VMEM((tm, d), BF16), pltpu.VMEM((tm, LANES), F32),
                        pltpu.VMEM((tm, LANES), BF16), pltpu.VMEM((tm, LANES), BF16),
                        pltpu.VMEM((tm, LANES), F32), pltpu.VMEM((8, tm), F32),
                        pltpu.VMEM((tm, d), F32), pltpu.SMEM((MOE_GROUPS,), jnp.int32),
                        pltpu.VMEM(w_ple_gate.shape[1:], BF16),
                        pltpu.VMEM(w_ple_in.shape[1:], BF16)],
        compiler_params=_cparams(2),
        name="moe_ple",
    )(x2d, g, w_router, b_router, w_exp_gate, w_exp_up, w_exp_down,
      p2d, g_ple, w_ple_gate, w_ple_in, g_final)


def _router_weights(w_grp, b_grp, w_exp, b_exp):
    d = w_grp.shape[0]
    w = jnp.concatenate([w_grp, w_exp.transpose(1, 0, 2).reshape(d, N_EXPERTS)], axis=1)
    b = jnp.concatenate([b_grp, b_exp.reshape(N_EXPERTS)])
    pad = LANES - w.shape[1]
    w = jnp.pad(w, ((0, 0), (0, pad))).T
    w_hi = w.astype(BF16)
    w_lo = (w - w_hi.astype(F32)).astype(BF16)
    return jnp.concatenate([w_hi, w_lo], axis=0), jnp.pad(b, (0, pad))[:, None]


def kernel(x, p, positions, norm_mix, w_in, w_gate, b_gate, w_pool, pool_scale, w_up_a, w_up_b, w_up_c, w_out, norm_moe, w_router_grp, b_router_grp, w_router_exp, b_router_exp, w_exp_gate, w_exp_up, w_exp_down, norm_ple, w_ple_in, w_ple_gate, norm_final):
    batch, seq, d = x.shape
    depth = w_in.shape[0]
    t = batch * seq
    x2d = x.reshape(t, d)
    cos_t, sin_t = _rope_tables(positions)
    for i in range(depth):
        (qa, ka, va, qb, kb, vb, yc), (e_gate, e_up, e_down) = _inproj(
            x2d, norm_mix[i][None, :], w_in, i, cos_t, sin_t, w_pool,
            pool_scale[i][None, :], (w_exp_gate, w_exp_up, w_exp_down), seq)
        dil_mix = _dilated(qa, ka, va, batch, seq)
        sb_out = _stick_breaking(qb, kb, vb, batch, seq)
        x2d = _merge(x2d, dil_mix, sb_out, yc, i, norm_mix[i][None, :], w_gate,
                     b_gate[i][None, :], w_up_a, w_up_b, w_up_c, w_out)
        w_r, b_r = _router_weights(w_router_grp[i], b_router_grp[i],
                                   w_router_exp[i], b_router_exp[i])
        x2d = _moe_ple(x2d, norm_moe[i][None, :], w_r, b_r, e_gate, e_up, e_down,
                       p.reshape(depth * t, -1), i,
                       norm_ple[i][None, :], w_ple_gate, w_ple_in,
                       norm_final[None, :], i == depth - 1)
    return x2d.reshape(batch, seq, d)
```
